```python
import jax, jax.numpy as jnp
from jax import lax
import numpy as np

D_MODEL = 2048
BATCH = 16
SEQ = 256
DEPTH = 4
DEC_BATCH = 4
DEC_SEQ = 4096
PAST_LEN = 256

GRID_W = 64
HEAD_DIM = 128
N_HEADS = D_MODEL // HEAD_DIM
N_KV_B = 4
N_MIXERS = 2
N_LAYERS_A = (DEPTH + 1) // 2
N_LAYERS_B = DEPTH // 2
WIN_R = 8
WIN_C = 16
Q_BLOCK = 128
ROPE_THETA = 10000.0
N_GROUPS = 4
EXPERTS_PER_GROUP = 8
N_EXPERTS = N_GROUPS * EXPERTS_PER_GROUP
TOP_K = 2
D_EXPERT = 768
MOE_BLOCK = 128
EPS = 1e-6
NEG_INF = -1e30

kernel_name = "hybrid_natten_gqa_hmoe_diffusion_step"


def rms_norm(x, g):
    xf = x.astype(jnp.float32)
    y = xf * lax.rsqrt(jnp.mean(xf * xf, axis=-1, keepdims=True) + EPS)
    return (y * g.astype(jnp.float32)).astype(x.dtype)


def modulation(cond, w_mod, b_mod):
    m = jax.nn.silu(cond) @ w_mod + b_mod
    return jnp.split(m, 6, axis=-1)


def softmax_attend(q, k, v, bias=None):
    s = jnp.einsum('bqkgd,bskd->bkgqs', q, k).astype(jnp.float32) * (HEAD_DIM ** -0.5)
    if bias is not None:
        s = s + bias
    p = jax.nn.softmax(s, axis=-1).astype(v.dtype)
    return jnp.einsum('bkgqs,bskd->bqkgd', p, v)


def dense_attention(q, k, v):
    B, T, H, hd = q.shape
    kvh = k.shape[2]
    nb = T // Q_BLOCK
    qb = q.reshape(B, nb, Q_BLOCK, kvh, H // kvh, hd).transpose(1, 0, 2, 3, 4, 5)
    out = lax.map(lambda qi: softmax_attend(qi, k, v), qb)
    return out.transpose(1, 0, 2, 3, 4, 5).reshape(B, T, H, hd)


def axial_rope_tables(n_tokens):
    t = jnp.arange(n_tokens)
    quarter = HEAD_DIM // 4
    inv = ROPE_THETA ** (-jnp.arange(quarter, dtype=jnp.float32) / quarter)
    ang_r = (t // GRID_W).astype(jnp.float32)[:, None] * inv
    ang_c = (t % GRID_W).astype(jnp.float32)[:, None] * inv
    return (jnp.cos(ang_r), jnp.sin(ang_r), jnp.cos(ang_c), jnp.sin(ang_c))


def _rotate(xa, cos, sin):
    x1, x2 = jnp.split(xa, 2, axis=-1)
    cos = cos[None, :, None, :]
    sin = sin[None, :, None, :]
    return jnp.concatenate([x1 * cos - x2 * sin, x2 * cos + x1 * sin], axis=-1)


def apply_axial_rope(x, tables):
    cr, sr, cc, sc = tables
    xr, xc = jnp.split(x.astype(jnp.float32), 2, axis=-1)
    return jnp.concatenate([_rotate(xr, cr, sr), _rotate(xc, cc, sc)], axis=-1).astype(x.dtype)


def qkv_heads(h, w_qkv, qn, kn, n_kv):
    B, T, _ = h.shape
    kv_dim = n_kv * HEAD_DIM
    qkv = h @ w_qkv
    q = qkv[..., :D_MODEL].reshape(B, T, N_HEADS, HEAD_DIM)
    k = qkv[..., D_MODEL:D_MODEL + kv_dim].reshape(B, T, n_kv, HEAD_DIM)
    v = qkv[..., D_MODEL + kv_dim:].reshape(B, T, n_kv, HEAD_DIM)
    return rms_norm(q, qn), rms_norm(k, kn), v


def context_attention(h, w_qkv, w_o, qn, kn, n_kv):
    B, L, _ = h.shape
    q, k, v = qkv_heads(h, w_qkv, qn, kn, n_kv)
    o = dense_attention(q, k, v)
    return o.reshape(B, L, D_MODEL) @ w_o, k, v


def neighbourhood_attention(q, k, v, k_ctx, v_ctx, rpb):
    B, T, H, hd = q.shape
    rows = T // GRID_W
    wr = min(WIN_R, rows)
    L = k_ctx.shape[1]
    qg = q.reshape(B, rows, GRID_W, H, hd).transpose(1, 0, 2, 3, 4)
    kg = k.reshape(B, rows, GRID_W, H, hd)
    vg = v.reshape(B, rows, GRID_W, H, hd)
    col = jnp.arange(GRID_W)
    col_start = jnp.clip(col - WIN_C // 2, 0, GRID_W - WIN_C)
    col_mask = (col[None, :] >= col_start[:, None]) & (col[None, :] < col_start[:, None] + WIN_C)
    dc_idx = jnp.clip(col[None, :] - col[:, None] + WIN_C - 1, 0, 2 * WIN_C - 2)
    rpb32 = rpb.astype(jnp.float32)
    ctx_bias = jnp.zeros((H, GRID_W, L), jnp.float32)

    def one_row(args):
        qr, r = args
        r0 = jnp.clip(r - wr // 2, 0, rows - wr)
        kb = lax.dynamic_slice_in_dim(kg, r0, wr, axis=1).reshape(B, wr * GRID_W, H, hd)
        vb = lax.dynamic_slice_in_dim(vg, r0, wr, axis=1).reshape(B, wr * GRID_W, H, hd)
        dr_idx = r0 + jnp.arange(wr) - r + WIN_R - 1
        bias = rpb32[:, dr_idx[None, :, None], dc_idx[:, None, :]]
        bias = jnp.where(col_mask[:, None, :], bias, NEG_INF).reshape(H, GRID_W, wr * GRID_W)
        bias = jnp.concatenate([bias, ctx_bias], axis=-1)
        k_all = jnp.concatenate([kb, k_ctx], axis=1)
        v_all = jnp.concatenate([vb, v_ctx], axis=1)
        return softmax_attend(qr[:, :, :, None, :], k_all, v_all, bias[None, :, None])

    out = lax.map(one_row, (qg, jnp.arange(rows)))
    return out.transpose(1, 0, 2, 3, 4, 5).reshape(B, T, H, hd)


def latent_neighbourhood_mixer(h, k_ctx, v_ctx, w_qkv, w_o, qn, kn, rpb):
    B, T, _ = h.shape
    q, k, v = qkv_heads(h, w_qkv, qn, kn, N_HEADS)
    o = neighbourhood_attention(q, k, v, k_ctx, v_ctx, rpb)
    return o.reshape(B, T, D_MODEL) @ w_o


def latent_axial_gqa_mixer(h, k_ctx, v_ctx, w_qkv, w_o, qn, kn, rope):
    B, T, _ = h.shape
    q, k, v = qkv_heads(h, w_qkv, qn, kn, N_KV_B)
    q = apply_axial_rope(q, rope)
    k = apply_axial_rope(k, rope)
    k_all = jnp.concatenate([k, k_ctx], axis=1)
    v_all = jnp.concatenate([v, v_ctx], axis=1)
    o = dense_attention(q, k_all, v_all)
    return o.reshape(B, T, D_MODEL) @ w_o


def hierarchical_moe(x, w_rg, b_rg, w_re, b_re, w_gate, w_up, w_down):
    shp = x.shape
    xt = x.reshape(-1, D_MODEL)
    N = xt.shape[0]
    g_prob = jax.nn.softmax((xt @ w_rg).astype(jnp.float32) + b_rg.astype(jnp.float32), axis=-1)
    g_top_p, g_top = lax.top_k(g_prob, 1)
    e_logits = ((xt @ w_re).astype(jnp.float32) + b_re.astype(jnp.float32)).reshape(N, N_GROUPS, EXPERTS_PER_GROUP)
    e_logits = jnp.take_along_axis(e_logits, g_top[:, :, None], axis=1)[:, 0]
    e_prob = jax.nn.softmax(e_logits, axis=-1)
    e_top_p, e_top = lax.top_k(e_prob, TOP_K)
    gates = g_top_p * e_top_p / jnp.sum(e_top_p, axis=-1, keepdims=True)
    experts = g_top * EXPERTS_PER_GROUP + e_top

    A = N * TOP_K
    flat_e = experts.reshape(A)
    flat_tok = jnp.repeat(jnp.arange(N, dtype=jnp.int32), TOP_K)
    flat_g = gates.reshape(A)
    order = jnp.argsort(flat_e)
    sorted_e = flat_e[order]
    counts = jnp.bincount(flat_e, length=N_EXPERTS)
    offs = jnp.cumsum(counts) - counts
    padded = (counts + MOE_BLOCK - 1) // MOE_BLOCK * MOE_BLOCK
    ends = jnp.cumsum(padded)
    poffs = ends - padded
    dest = poffs[sorted_e] + (jnp.arange(A) - offs[sorted_e])
    P = (A + N_EXPERTS * (MOE_BLOCK - 1) + MOE_BLOCK - 1) // MOE_BLOCK * MOE_BLOCK
    nb = P // MOE_BLOCK
    tok_buf = jnp.full((P,), N, jnp.int32).at[dest].set(flat_tok[order])
    gate_buf = jnp.zeros((P,), x.dtype).at[dest].set(flat_g[order].astype(x.dtype))
    block_e = jnp.clip(jnp.searchsorted(ends, jnp.arange(nb) * MOE_BLOCK, side='right'), 0, N_EXPERTS - 1)
    x_pad = jnp.concatenate([xt, jnp.zeros((1, D_MODEL), xt.dtype)], axis=0)
    xb = x_pad[tok_buf].reshape(nb, MOE_BLOCK, D_MODEL)

    def expert_block(args):
        xi, e = args
        return (jax.nn.silu(xi @ w_gate[e]) * (xi @ w_up[e])) @ w_down[e]

    yb = lax.map(expert_block, (xb, block_e)).reshape(P, D_MODEL)
    y = jnp.zeros((N + 1, D_MODEL), x.dtype).at[tok_buf].add(yb * gate_buf[:, None])[:N]
    return y.reshape(shp)


def setup_inputs(seed: int = 0) -> dict:
    key = jax.random.key(seed)
    ks = jax.random.split(key, 32)
    f32 = jnp.float32
    D = D_MODEL
    s_d = D ** -0.5
    qkv_a = 3 * D
    qkv_b = D + 2 * N_KV_B * HEAD_DIM

    def nrm(k, shape, scale):
        return jax.random.normal(k, shape, f32) * scale

    return {
        "x_prompt": nrm(ks[0], (BATCH, SEQ, D), 1.0),
        "x_sample": nrm(ks[1], (DEC_BATCH, DEC_SEQ, D), 1.0),
        "c": nrm(ks[2], (DEC_BATCH, D), 1.0),
        "cache_k_a": nrm(ks[3], (DEC_BATCH, N_LAYERS_A, PAST_LEN, N_HEADS, HEAD_DIM), 1.0),
        "cache_v_a": nrm(ks[4], (DEC_BATCH, N_LAYERS_A, PAST_LEN, N_HEADS, HEAD_DIM), 1.0),
        "cache_k_b": nrm(ks[5], (DEC_BATCH, N_LAYERS_B, PAST_LEN, N_KV_B, HEAD_DIM), 1.0),
        "cache_v_b": nrm(ks[6], (DEC_BATCH, N_LAYERS_B, PAST_LEN, N_KV_B, HEAD_DIM), 1.0),
        "c_ctx": nrm(ks[7], (D,), 1.0),
        "w_mod": nrm(ks[8], (DEPTH, D, 6 * D), 0.5 * s_d),
        "b_mod": nrm(ks[9], (DEPTH, 6 * D), 0.02),
        "norm1_g": 1.0 + nrm(ks[10], (DEPTH, D), 0.1),
        "norm2_g": 1.0 + nrm(ks[11], (DEPTH, D), 0.1),
        "w_qkv_a": nrm(ks[12], (N_LAYERS_A, D, qkv_a), s_d),
        "w_o_a": nrm(ks[13], (N_LAYERS_A, D, D), s_d),
        "q_norm_a": 1.0 + nrm(ks[14], (N_LAYERS_A, HEAD_DIM), 0.1),
        "k_norm_a": 1.0 + nrm(ks[15], (N_LAYERS_A, HEAD_DIM), 0.1),
        "rpb_a": nrm(ks[16], (N_LAYERS_A, N_HEADS, 2 * WIN_R - 1, 2 * WIN_C - 1), 0.5),
        "w_qkv_b": nrm(ks[17], (N_LAYERS_B, D, qkv_b), s_d),
        "w_o_b": nrm(ks[18], (N_LAYERS_B, D, D), s_d),
        "q_norm_b": 1.0 + nrm(ks[19], (N_LAYERS_B, HEAD_DIM), 0.1),
        "k_norm_b": 1.0 + nrm(ks[20], (N_LAYERS_B, HEAD_DIM), 0.1),
        "w_router_group": nrm(ks[21], (DEPTH, D, N_GROUPS), s_d),
        "b_router_group": nrm(ks[22], (DEPTH, N_GROUPS), 0.01),
        "w_router_expert": nrm(ks[23], (DEPTH, D, N_EXPERTS), s_d),
        "b_router_expert": nrm(ks[24], (DEPTH, N_EXPERTS), 0.01),
        "w_gate": nrm(ks[25], (DEPTH, N_EXPERTS, D, D_EXPERT), s_d),
        "w_up": nrm(ks[26], (DEPTH, N_EXPERTS, D, D_EXPERT), s_d),
        "w_down": nrm(ks[27], (DEPTH, N_EXPERTS, D_EXPERT, D), D_EXPERT ** -0.5),
    }


def reference(x_prompt, x_sample, c, cache_k_a, cache_v_a, cache_k_b, cache_v_b, c_ctx,
              w_mod, b_mod, norm1_g, norm2_g,
              w_qkv_a, w_o_a, q_norm_a, k_norm_a, rpb_a,
              w_qkv_b, w_o_b, q_norm_b, k_norm_b,
              w_router_group, b_router_group, w_router_expert, b_router_expert,
              w_gate, w_up, w_down):
    rope = axial_rope_tables(x_sample.shape[1])
    xp = x_prompt
    xs = x_sample
    new_k_a, new_v_a, new_k_b, new_v_b = [], [], [], []
    for i in range(DEPTH):
        j = i // N_MIXERS
        sh1p, sc1p, g1p, sh2p, sc2p, g2p = modulation(c_ctx, w_mod[i], b_mod[i])
        sh1s, sc1s, g1s, sh2s, sc2s, g2s = [m[:, None, :] for m in modulation(c, w_mod[i], b_mod[i])]
        hp = rms_norm(xp, norm1_g[i]) * (1.0 + sc1p) + sh1p
        hs = rms_norm(xs, norm1_g[i]) * (1.0 + sc1s) + sh1s
        if i % N_MIXERS == 0:
            op, kp, vp = context_attention(hp, w_qkv_a[j], w_o_a[j], q_norm_a[j], k_norm_a[j], N_HEADS)
            new_k_a.append(kp)
            new_v_a.append(vp)
            os_ = latent_neighbourhood_mixer(hs, cache_k_a[:, j], cache_v_a[:, j], w_qkv_a[j], w_o_a[j],
                                             q_norm_a[j], k_norm_a[j], rpb_a[j])
        else:
            op, kp, vp = context_attention(hp, w_qkv_b[j], w_o_b[j], q_norm_b[j], k_norm_b[j], N_KV_B)
            new_k_b.append(kp)
            new_v_b.append(vp)
            os_ = latent_axial_gqa_mixer(hs, cache_k_b[:, j], cache_v_b[:, j], w_qkv_b[j], w_o_b[j],
                                         q_norm_b[j], k_norm_b[j], rope)
        xp = xp + g1p * op
        xs = xs + g1s * os_
        hp = rms_norm(xp, norm2_g[i]) * (1.0 + sc2p) + sh2p
        hs = rms_norm(xs, norm2_g[i]) * (1.0 + sc2s) + sh2s
        moe_w = (w_router_group[i], b_router_group[i], w_router_expert[i], b_router_expert[i],
                 w_gate[i], w_up[i], w_down[i])
        xp = xp + g2p * hierarchical_moe(hp, *moe_w)
        xs = xs + g2s * hierarchical_moe(hs, *moe_w)
    return (xp, xs, jnp.stack(new_k_a, axis=1), jnp.stack(new_v_a, axis=1),
            jnp.stack(new_k_b, axis=1), jnp.stack(new_v_b, axis=1))
```

```python
import functools

import jax
import jax.numpy as jnp
from jax import lax
from jax.experimental import pallas as pl
from jax.experimental.pallas import tpu as pltpu

D_MODEL = 2048
HEAD_DIM = 128
N_HEADS = D_MODEL // HEAD_DIM
N_KV_B = 4
GRID_W = 64
WIN_R = 8
WIN_C = 16
ROPE_THETA = 10000.0
N_GROUPS = 4
EXPERTS_PER_GROUP = 8
N_EXPERTS = N_GROUPS * EXPERTS_PER_GROUP
D_EXPERT = 768
EPS = 1e-6
NEG_INF = -1e30

LANES = 128
MOD_ROWS = 8
GROUP_ROWS = 4096
MOE_ROWS = 256
NA_Q_ROWS = 8
NA_K_ROWS = 16
NA_TABLE = 30
VMEM_LIMIT = 56 * 1024 * 1024

_BF16 = jnp.bfloat16
_F32 = jnp.float32


def _params(sem, vmem=VMEM_LIMIT):
    return pltpu.CompilerParams(dimension_semantics=sem, vmem_limit_bytes=vmem)


def _dot(a, b):
    return jnp.dot(a, b, preferred_element_type=_F32)


def _dot_nt(a, b):
    return lax.dot_general(a, b, (((1,), (1,)), ((), ())), preferred_element_type=_F32)


def _silu(x):
    return x / (1.0 + jnp.exp(-x))


def _mod_kernel(cond_ref, w_ref, b_ref, o_ref):
    s = _silu(cond_ref[...]).astype(_BF16)
    o_ref[...] = _dot(s, w_ref[...].astype(_BF16)) + b_ref[...]


def _modulation(cond, w_mod, b_mod):
    depth, d, n = w_mod.shape
    tn = 1024
    return pl.pallas_call(
        _mod_kernel,
        grid=(depth, n // tn),
        in_specs=[
            pl.BlockSpec((MOD_ROWS, d), lambda l, j: (0, 0)),
            pl.BlockSpec((None, d, tn), lambda l, j: (l, 0, j)),
            pl.BlockSpec((None, 1, tn), lambda l, j: (l, 0, j)),
        ],
        out_specs=pl.BlockSpec((None, MOD_ROWS, tn), lambda l, j: (l, 0, j)),
        out_shape=jax.ShapeDtypeStruct((depth, MOD_ROWS, n), _F32),
        compiler_params=_params(("arbitrary", "arbitrary")),
        name="modulation",
    )(cond, w_mod, b_mod.reshape(depth, 1, n))


def _mod_slices(mod_ref, which):
    return mod_ref[0, :, which * D_MODEL:(which + 1) * D_MODEL]


def _norm_modulate(x, g, shift, scale):
    y = x * lax.rsqrt(jnp.mean(x * x, axis=-1, keepdims=True) + EPS)
    return (y * g) * (1.0 + scale) + shift


def _head_norm(a, g):
    return a * lax.rsqrt(jnp.mean(a * a, axis=-1, keepdims=True) + EPS) * g


def _rope(a, cos, sin):
    lane = lax.broadcasted_iota(jnp.int32, a.shape, 1)
    first = (lane % 64) < 32
    swapped = jnp.where(first, pltpu.roll(a, 96, axis=1), pltpu.roll(a, 32, axis=1))
    return a * cos + swapped * sin


def _qkv_kernel(*refs, n_q, n_k, heads_per_tile, rope, emit_kv):
    x_ref, mod_ref, g_ref, w_ref, qn_ref, kn_ref = refs[:6]
    refs = refs[6:]
    if rope:
        cos_ref, sin_ref = refs[:2]
        refs = refs[2:]
    o_ref = refs[0]
    kv_ref = refs[1] if emit_kv else None
    h_scr = refs[-1]
    j = pl.program_id(1)

    @pl.when(j == 0)
    def _():
        h = _norm_modulate(x_ref[...], g_ref[...], _mod_slices(mod_ref, 0), _mod_slices(mod_ref, 1))
        h_scr[...] = h.astype(_BF16)

    acc = _dot(h_scr[...], w_ref[...])

    def heads(gain_ref, scale):
        outs = []
        for hh in range(heads_per_tile):
            a = _head_norm(acc[:, hh * HEAD_DIM:(hh + 1) * HEAD_DIM], gain_ref[...])
            if rope:
                a = _rope(a, cos_ref[...], sin_ref[...])
            outs.append(a * scale if scale != 1.0 else a)
        return jnp.concatenate(outs, axis=1)

    @pl.when(j < n_q)
    def _():
        o_ref[...] = heads(qn_ref, HEAD_DIM ** -0.5).astype(_BF16)

    @pl.when((j >= n_q) & (j < n_q + n_k))
    def _():
        k = heads(kn_ref, 1.0)
        o_ref[...] = k.astype(_BF16)
        if emit_kv:
            kv_ref[...] = k

    @pl.when(j >= n_q + n_k)
    def _():
        o_ref[...] = acc.astype(_BF16)
        if emit_kv:
            kv_ref[...] = acc


def _qkv(x, mod, g, w, qn, kn, n_kv, *, row0, m, rope_tables, emit_kv):
    d = x.shape[1]
    n = w.shape[1]
    tm, tn = 1024, 512
    heads_per_tile = tn // HEAD_DIM
    n_q = D_MODEL // tn
    n_k = n_kv * HEAD_DIM // tn
    rope = rope_tables is not None
    blocks_per_group = GROUP_ROWS // tm
    blk0 = row0 // tm

    def group(i):
        return (blk0 + i) // blocks_per_group

    in_specs = [
        pl.BlockSpec((tm, d), lambda i, j: (blk0 + i, 0)),
        pl.BlockSpec((1, 1, 6 * d), lambda i, j: (group(i), 0, 0)),
        pl.BlockSpec((1, d), lambda i, j: (0, 0)),
        pl.BlockSpec((d, tn), lambda i, j: (0, j)),
        pl.BlockSpec((1, HEAD_DIM), lambda i, j: (0, 0)),
        pl.BlockSpec((1, HEAD_DIM), lambda i, j: (0, 0)),
    ]
    args = [x, mod, g, w, qn, kn]
    if rope:
        pos = lambda i, j: (i % blocks_per_group, 0)
        in_specs += [pl.BlockSpec((tm, HEAD_DIM), pos), pl.BlockSpec((tm, HEAD_DIM), pos)]
        args += list(rope_tables)
    out_specs = [pl.BlockSpec((tm, tn), lambda i, j: (i, j))]
    out_shape = [jax.ShapeDtypeStruct((m, n), _BF16)]
    if emit_kv:
        out_specs.append(pl.BlockSpec((tm, tn), lambda i, j: (i, jnp.maximum(j - n_q, 0))))
        out_shape.append(jax.ShapeDtypeStruct((m, n - D_MODEL), _F32))
    return pl.pallas_call(
        functools.partial(_qkv_kernel, n_q=n_q, n_k=n_k, heads_per_tile=heads_per_tile,
                          rope=rope, emit_kv=emit_kv),
        grid=(m // tm, n // tn),
        in_specs=in_specs,
        out_specs=out_specs,
        out_shape=out_shape,
        scratch_shapes=[pltpu.VMEM((tm, d), _BF16)],
        compiler_params=_params(("arbitrary", "arbitrary")),
        name="qkv_rope" if rope else "qkv",
    )(*args)


def _rope_tables(n_tokens):
    t = jnp.arange(n_tokens)
    quarter = HEAD_DIM // 4
    inv = ROPE_THETA ** (-jnp.arange(quarter, dtype=_F32) / quarter)
    ang_r = (t // GRID_W).astype(_F32)[:, None] * inv
    ang_c = (t % GRID_W).astype(_F32)[:, None] * inv
    cr, sr, cc, sc = jnp.cos(ang_r), jnp.sin(ang_r), jnp.cos(ang_c), jnp.sin(ang_c)
    return (jnp.concatenate([cr, cr, cc, cc], axis=1), jnp.concatenate([-sr, sr, -sc, sc], axis=1))


def _softmax_pv(scores, values):
    m = functools.reduce(jnp.maximum, [jnp.max(s, axis=-1, keepdims=True) for s in scores])
    ps = [jnp.exp(s - m) for s in scores]
    l = functools.reduce(jnp.add, [jnp.sum(p, axis=-1, keepdims=True) for p in ps])
    o = functools.reduce(jnp.add, [_dot(p.astype(_BF16), v) for p, v in zip(ps, values)])
    return o / l


def _ctx_attn_kernel(q_ref, k_ref, v_ref, o_ref, *, group):
    for h in range(N_HEADS):
        kv = h // group
        q = q_ref[:, h * HEAD_DIM:(h + 1) * HEAD_DIM]
        k = k_ref[:, kv * HEAD_DIM:(kv + 1) * HEAD_DIM]
        v = v_ref[:, kv * HEAD_DIM:(kv + 1) * HEAD_DIM]
        o = _softmax_pv([_dot_nt(q, k)], [v])
        o_ref[:, h * HEAD_DIM:(h + 1) * HEAD_DIM] = o.astype(_BF16)


def _context_attention(qkv, n_kv, seq, n_rows_total):
    m = qkv.shape[0]
    kv_dim = n_kv * HEAD_DIM
    k_blk = D_MODEL // kv_dim
    return pl.pallas_call(
        functools.partial(_ctx_attn_kernel, group=N_HEADS // n_kv),
        grid=(m // seq,),
        in_specs=[
            pl.BlockSpec((seq, D_MODEL), lambda b: (b, 0)),
            pl.BlockSpec((seq, kv_dim), lambda b: (b, k_blk)),
            pl.BlockSpec((seq, kv_dim), lambda b: (b, k_blk + 1)),
        ],
        out_specs=pl.BlockSpec((seq, D_MODEL), lambda b: (b, 0)),
        out_shape=jax.ShapeDtypeStruct((n_rows_total, D_MODEL), _BF16),
        compiler_params=_params(("arbitrary",)),
        name="context_attention",
    )(qkv, qkv, qkv)


def _na_table_kernel(rpb_ref, o_ref):
    h = pl.program_id(0)
    c = lax.broadcasted_iota(jnp.int32, (GRID_W, GRID_W), 0)
    kc = lax.broadcasted_iota(jnp.int32, (GRID_W, GRID_W), 1)
    start = jnp.clip(c - WIN_C // 2, 0, GRID_W - WIN_C)
    col_ok = (kc >= start) & (kc < start + WIN_C)
    rel = kc - c + WIN_C - 1
    n_dc = 2 * WIN_C - 1
    tiles = []
    for d in range(-8, NA_TABLE - 8 + 1):
        if 0 <= d < 2 * WIN_R - 1:
            t = jnp.zeros((GRID_W, GRID_W), _F32)
            for jj in range(n_dc):
                t = jnp.where(rel == jj, rpb_ref[h, d * n_dc + jj], t)
            tiles.append(jnp.where(col_ok, t, NEG_INF))
        else:
            tiles.append(jnp.zeros((GRID_W, GRID_W), _F32))
    for i in range(NA_TABLE):
        o_ref[0, i] = jnp.concatenate([tiles[i], tiles[i + 1]], axis=1)


def _na_bias_table(rpb):
    h = rpb.shape[0]
    return pl.pallas_call(
        _na_table_kernel,
        grid=(h,),
        in_specs=[pl.BlockSpec(memory_space=pltpu.SMEM)],
        out_specs=pl.BlockSpec((1, NA_TABLE, GRID_W, 2 * GRID_W), lambda i: (i, 0, 0, 0)),
        out_shape=jax.ShapeDtypeStruct((h, NA_TABLE, GRID_W, 2 * GRID_W), _F32),
        compiler_params=_params(("arbitrary",)),
        name="na_bias_table",
    )(rpb.reshape(h, -1))


def _na_attn_kernel(alias_ref, q_ref, k_ref, v_ref, kc_ref, vc_ref, tab_ref, o_ref, *, rows):
    del alias_ref
    blk = pl.program_id(2)
    rs = blk * NA_Q_ROWS
    ks = jnp.clip(rs - WIN_R // 2, 0, rows - NA_K_ROWS)
    k0 = pl.multiple_of(ks * GRID_W, 4 * GRID_W)
    n_keys = NA_K_ROWS * GRID_W
    q = q_ref[...]
    k = k_ref[pl.ds(k0, n_keys), :]
    v = v_ref[pl.ds(k0, n_keys), :]
    s = _dot_nt(q, k)
    tile0 = ks - rs + 2 * WIN_R - 1
    bias = jnp.concatenate(
        [jnp.concatenate([tab_ref[0, tile0 - qi + 2 * p] for p in range(NA_K_ROWS // 2)], axis=1)
         for qi in range(NA_Q_ROWS)], axis=0)
    r = rs + lax.broadcasted_iota(jnp.int32, (NA_Q_ROWS * GRID_W, 1), 0) // GRID_W
    r0 = jnp.clip(r - WIN_R // 2, 0, rows - WIN_R)
    kr = ks + lax.broadcasted_iota(jnp.int32, (1, n_keys), 1) // GRID_W
    row_ok = (kr >= r0) & (kr < r0 + WIN_R)
    s = jnp.where(row_ok, s + bias, NEG_INF)
    s_ctx = _dot_nt(q, kc_ref[0])
    o_ref[...] = _softmax_pv([s, s_ctx], [v, vc_ref[0]]).astype(_BF16)


def _neighbourhood_attention(attn, qkv, k_ctx, v_ctx, table, row0):
    b, l, _ = k_ctx.shape
    t = qkv.shape[0] // b
    rows = t // GRID_W
    tq = NA_Q_ROWS * GRID_W
    nblk = t // tq
    out_blk0 = row0 // tq
    return pl.pallas_call(
        functools.partial(_na_attn_kernel, rows=rows),
        grid=(b, N_HEADS, nblk),
        in_specs=[
            pl.BlockSpec(memory_space=pl.ANY),
            pl.BlockSpec((tq, HEAD_DIM), lambda bi, h, i: (bi * nblk + i, h)),
            pl.BlockSpec((t, HEAD_DIM), lambda bi, h, i: (bi, N_HEADS + h)),
            pl.BlockSpec((t, HEAD_DIM), lambda bi, h, i: (bi, 2 * N_HEADS + h)),
            pl.BlockSpec((1, l, HEAD_DIM), lambda bi, h, i: (bi, 0, h)),
            pl.BlockSpec((1, l, HEAD_DIM), lambda bi, h, i: (bi, 0, h)),
            pl.BlockSpec((1, NA_TABLE, GRID_W, 2 * GRID_W), lambda bi, h, i: (h, 0, 0, 0)),
        ],
        out_specs=pl.BlockSpec((tq, HEAD_DIM), lambda bi, h, i: (out_blk0 + bi * nblk + i, h)),
        out_shape=jax.ShapeDtypeStruct(attn.shape, attn.dtype),
        input_output_aliases={0: 0},
        compiler_params=_params(("arbitrary", "arbitrary", "arbitrary")),
        name="neighbourhood_attention",
    )(attn, qkv, qkv, qkv, k_ctx, v_ctx, table)


def _gqa_attn_kernel(alias_ref, q_ref, k_ref, v_ref, kc_ref, vc_ref, o_ref, *, group, chunk):
    del alias_ref
    tq = q_ref.shape[0]
    q = jnp.concatenate([q_ref[:, g * HEAD_DIM:(g + 1) * HEAD_DIM] for g in range(group)], axis=0)
    t = k_ref.shape[0]
    m = l = acc = None
    for c in range(t // chunk + 1):
        if c < t // chunk:
            k = k_ref[c * chunk:(c + 1) * chunk, :]
            v = v_ref[c * chunk:(c + 1) * chunk, :]
        else:
            k, v = kc_ref[0], vc_ref[0]
        s = _dot_nt(q, k)
        m_c = jnp.max(s, axis=-1, keepdims=True)
        if m is None:
            m = m_c
            p = jnp.exp(s - m)
            l = jnp.sum(p, axis=-1, keepdims=True)
            acc = _dot(p.astype(_BF16), v)
        else:
            m_new = jnp.maximum(m, m_c)
            alpha = jnp.exp(m - m_new)
            p = jnp.exp(s - m_new)
            l = alpha * l + jnp.sum(p, axis=-1, keepdims=True)
            acc = alpha * acc + _dot(p.astype(_BF16), v)
            m = m_new
    o = acc / l
    for g in range(group):
        o_ref[:, g * HEAD_DIM:(g + 1) * HEAD_DIM] = o[g * tq:(g + 1) * tq].astype(_BF16)


def _gqa_attention(attn, qkv, k_ctx, v_ctx, row0):
    b, l, kv_dim = k_ctx.shape
    n_kv = kv_dim // HEAD_DIM
    group = N_HEADS // n_kv
    t = qkv.shape[0] // b
    tq = 128
    nblk = t // tq
    out_blk0 = row0 // tq
    return pl.pallas_call(
        functools.partial(_gqa_attn_kernel, group=group, chunk=1024),
        grid=(b, n_kv, nblk),
        in_specs=[
            pl.BlockSpec(memory_space=pl.ANY),
            pl.BlockSpec((tq, group * HEAD_DIM), lambda bi, h, i: (bi * nblk + i, h)),
            pl.BlockSpec((t, HEAD_DIM), lambda bi, h, i: (bi, N_HEADS + h)),
            pl.BlockSpec((t, HEAD_DIM), lambda bi, h, i: (bi, N_HEADS + n_kv + h)),
            pl.BlockSpec((1, l, HEAD_DIM), lambda bi, h, i: (bi, 0, h)),
            pl.BlockSpec((1, l, HEAD_DIM), lambda bi, h, i: (bi, 0, h)),
        ],
        out_specs=pl.BlockSpec((tq, group * HEAD_DIM), lambda bi, h, i: (out_blk0 + bi * nblk + i, h)),
        out_shape=jax.ShapeDtypeStruct(attn.shape, attn.dtype),
        input_output_aliases={0: 0},
        compiler_params=_params(("arbitrary", "arbitrary", "arbitrary")),
        name="gqa_attention",
    )(attn, qkv, qkv, qkv, k_ctx, v_ctx)


def _proj_router_kernel(a_ref, x_ref, mod_ref, g_ref, wo_ref, wr_ref, br_ref,
                        x1_ref, h_ref, ri_ref, rf_ref, cnt_ref, carry):
    i = pl.program_id(0)
    tm = a_ref.shape[0]

    @pl.when(i == 0)
    def _():
        carry[...] = jnp.zeros_like(carry)

    o = _dot(a_ref[...], wo_ref[...])
    x1 = x_ref[...] + _mod_slices(mod_ref, 2) * o
    x1_ref[...] = x1
    h = _norm_modulate(x1, g_ref[...], _mod_slices(mod_ref, 3), _mod_slices(mod_ref, 4)).astype(_BF16)
    h_ref[...] = h

    logits = _dot(h, wr_ref[...]) + br_ref[...]
    lane = lax.broadcasted_iota(jnp.int32, logits.shape, 1).astype(_F32)

    def masked_softmax(mask):
        z = jnp.where(mask, logits, -jnp.inf)
        e = jnp.exp(z - jnp.max(z, axis=-1, keepdims=True))
        return e / jnp.sum(e, axis=-1, keepdims=True)

    def top1(p, mask):
        best = jnp.max(jnp.where(mask, p, -1.0), axis=-1, keepdims=True)
        idx = jnp.min(jnp.where(mask & (p == best), lane, float(LANES)), axis=-1, keepdims=True)
        return best, idx

    g_mask = lane < N_GROUPS
    g_p, g_idx = top1(masked_softmax(g_mask), g_mask)
    lo = N_GROUPS + g_idx * EXPERTS_PER_GROUP
    e_mask = (lane >= lo) & (lane < lo + EXPERTS_PER_GROUP)
    e_prob = masked_softmax(e_mask)
    p1, i1 = top1(e_prob, e_mask)
    p2, i2 = top1(e_prob, e_mask & (lane != i1))
    e1, e2 = i1 - N_GROUPS, i2 - N_GROUPS
    gate1 = g_p * p1 / (p1 + p2)
    gate2 = g_p * p2 / (p1 + p2)

    hit1, hit2 = lane == e1, lane == e2
    onehot = jnp.where(hit1 | hit2, 1.0, 0.0)
    row = lax.broadcasted_iota(jnp.int32, (tm, tm), 0)
    col = lax.broadcasted_iota(jnp.int32, (tm, tm), 1)
    lower = jnp.where(row > col, 1.0, 0.0).astype(_BF16)
    before = _dot(lower, onehot.astype(_BF16)) + carry[...]
    rank1 = jnp.sum(jnp.where(hit1, before, 0.0), axis=-1, keepdims=True)
    rank2 = jnp.sum(jnp.where(hit2, before, 0.0), axis=-1, keepdims=True)
    carry[...] = carry[...] + jnp.sum(onehot, axis=0, keepdims=True)

    ri_ref[...] = jnp.where(lane == 0, e1, jnp.where(lane == 1, e2,
                            jnp.where(lane == 2, rank1, jnp.where(lane == 3, rank2, 0.0)))).astype(jnp.int32)
    rf_ref[...] = jnp.where(lane == 0, gate1, jnp.where(lane == 1, gate2, 0.0))
    cnt_ref[...] = jnp.broadcast_to(carry[...], cnt_ref.shape)


def _proj_router(attn, x, mod, g, w_o, w_r, b_r):
    m, d = x.shape
    tm = 256
    blocks_per_group = GROUP_ROWS // tm
    row = lambda i: (i, 0)
    const = lambda i: (0, 0)
    return pl.pallas_call(
        _proj_router_kernel,
        grid=(m // tm,),
        in_specs=[
            pl.BlockSpec((tm, d), row),
            pl.BlockSpec((tm, d), row),
            pl.BlockSpec((1, 1, 6 * d), lambda i: (i // blocks_per_group, 0, 0)),
            pl.BlockSpec((1, d), const),
            pl.BlockSpec((d, d), const),
            pl.BlockSpec((d, LANES), const),
            pl.BlockSpec((1, LANES), const),
        ],
        out_specs=[
            pl.BlockSpec((tm, d), row),
            pl.BlockSpec((tm, d), row),
            pl.BlockSpec((tm, LANES), row),
            pl.BlockSpec((tm, LANES), row),
            pl.BlockSpec((MOD_ROWS, LANES), const),
        ],
        out_shape=[
            jax.ShapeDtypeStruct((m, d), _F32),
            jax.ShapeDtypeStruct((m, d), _BF16),
            jax.ShapeDtypeStruct((m, LANES), jnp.int32),
            jax.ShapeDtypeStruct((m, LANES), _F32),
            jax.ShapeDtypeStruct((MOD_ROWS, LANES), _F32),
        ],
        scratch_shapes=[pltpu.VMEM((1, LANES), _F32)],
        compiler_params=_params(("arbitrary",)),
        name="proj_router",
    )(attn, x, mod, g, w_o, w_r, b_r)


def _expert_kernel(be_ref, used_ref, x_ref, wg_ref, wu_ref, wd_ref, o_ref):
    b = pl.program_id(0)

    @pl.when(b < used_ref[0])
    def _():
        x = x_ref[...]
        hidden = _silu(_dot(x, wg_ref[...])) * _dot(x, wu_ref[...])
        o_ref[...] = _dot(hidden.astype(_BF16), wd_ref[...])

    @pl.when(b >= used_ref[0])
    def _():
        o_ref[...] = jnp.zeros_like(o_ref)


def _experts(xs, block_e, n_used, w_gate, w_up, w_down):
    p, d = xs.shape
    de = w_gate.shape[-1]
    nb = p // MOE_ROWS
    return pl.pallas_call(
        _expert_kernel,
        grid_spec=pltpu.PrefetchScalarGridSpec(
            num_scalar_prefetch=2,
            grid=(nb,),
            in_specs=[
                pl.BlockSpec((MOE_ROWS, d), lambda b, be, nu: (b, 0)),
                pl.BlockSpec((None, d, de), lambda b, be, nu: (be[b], 0, 0)),
                pl.BlockSpec((None, d, de), lambda b, be, nu: (be[b], 0, 0)),
                pl.BlockSpec((None, de, d), lambda b, be, nu: (be[b], 0, 0)),
            ],
            out_specs=pl.BlockSpec((MOE_ROWS, d), lambda b, be, nu: (b, 0)),
        ),
        out_shape=jax.ShapeDtypeStruct((p, d), _F32),
        compiler_params=_params(("arbitrary",)),
        name="experts",
    )(block_e, n_used, xs, w_gate, w_up, w_down)


def _combine_kernel(x_ref, mod_ref, y1_ref, y2_ref, rf_ref, o_ref):
    gate = rf_ref[...]
    y = y1_ref[...] * gate[:, 0:1] + y2_ref[...] * gate[:, 1:2]
    o_ref[...] = x_ref[...] + _mod_slices(mod_ref, 5) * y


def _combine(x1, mod, y1, y2, rf):
    m, d = x1.shape
    tm = 512
    blocks_per_group = GROUP_ROWS // tm
    row = lambda i: (i, 0)
    return pl.pallas_call(
        _combine_kernel,
        grid=(m // tm,),
        in_specs=[
            pl.BlockSpec((tm, d), row),
            pl.BlockSpec((1, 1, 6 * d), lambda i: (i // blocks_per_group, 0, 0)),
            pl.BlockSpec((tm, d), row),
            pl.BlockSpec((tm, d), row),
            pl.BlockSpec((tm, LANES), row),
        ],
        out_specs=pl.BlockSpec((tm, d), row),
        out_shape=jax.ShapeDtypeStruct((m, d), _F32),
        compiler_params=_params(("arbitrary",)),
        name="combine",
    )(x1, mod, y1, y2, rf)


def _moe(x1, h2, ri, rf, counts, mod, w_gate, w_up, w_down):
    m = x1.shape[0]
    n_assign = 2 * m
    p_rows = (n_assign + N_EXPERTS * (MOE_ROWS - 1) + MOE_ROWS - 1) // MOE_ROWS * MOE_ROWS
    nb = p_rows // MOE_ROWS
    cnt = counts[0, :N_EXPERTS].astype(jnp.int32)
    padded = (cnt + MOE_ROWS - 1) // MOE_ROWS * MOE_ROWS
    ends = jnp.cumsum(padded)
    starts = ends - padded
    dest = starts[ri[:, 0:2]] + ri[:, 2:4]
    block_e = jnp.clip(jnp.searchsorted(ends, jnp.arange(nb, dtype=jnp.int32) * MOE_ROWS, side='right'),
                       0, N_EXPERTS - 1).astype(jnp.int32)
    n_used = (ends[-1:] // MOE_ROWS).astype(jnp.int32)
    tok = jnp.full((p_rows,), m, jnp.int32).at[dest.reshape(-1)].set(
        jnp.repeat(jnp.arange(m, dtype=jnp.int32), 2))
    h_pad = jnp.concatenate([h2, jnp.zeros((1, h2.shape[1]), h2.dtype)], axis=0)
    xs = h_pad[tok]
    yb = _experts(xs, block_e, n_used, w_gate, w_up, w_down)
    return _combine(x1, mod, yb[dest[:, 0]], yb[dest[:, 1]], rf)


def kernel(x_prompt, x_sample, c, cache_k_a, cache_v_a, cache_k_b, cache_v_b, c_ctx, w_mod, b_mod, norm1_g, norm2_g, w_qkv_a, w_o_a, q_norm_a, k_norm_a, rpb_a, w_qkv_b, w_o_b, q_norm_b, k_norm_b, w_router_group, b_router_group, w_router_expert, b_router_expert, w_gate, w_up, w_down):
    batch, seq, d = x_prompt.shape
    dec_batch, dec_seq, _ = x_sample.shape
    depth = w_mod.shape[0]
    mp = batch * seq
    ms = dec_batch * dec_seq
    assert mp == GROUP_ROWS and dec_seq == GROUP_ROWS and d == D_MODEL
    x = jnp.concatenate([x_prompt.reshape(mp, d), x_sample.reshape(ms, d)], axis=0)

    cond = jnp.concatenate([c_ctx[None], c, jnp.zeros((MOD_ROWS - 1 - dec_batch, d), _F32)], axis=0)
    mod_all = _modulation(cond, w_mod, b_mod)
    rope_tables = _rope_tables(dec_seq)

    w_router = jnp.concatenate([w_router_group, w_router_expert], axis=-1)
    n_route = w_router.shape[-1]
    w_router = jnp.pad(w_router, ((0, 0), (0, 0), (0, LANES - n_route))).astype(_BF16)
    b_router = jnp.pad(jnp.concatenate([b_router_group, b_router_expert], axis=-1),
                       ((0, 0), (0, LANES - n_route)))

    new_kv = [[], [], [], []]
    for i in range(depth):
        j = i // 2
        mod = mod_all[i].reshape(MOD_ROWS, 1, 6 * d)
        g1 = norm1_g[i].reshape(1, d)
        g2 = norm2_g[i].reshape(1, d)
        if i % 2 == 0:
            n_kv, w_qkv, w_o, qn, kn = N_HEADS, w_qkv_a[j], w_o_a[j], q_norm_a[j], k_norm_a[j]
            cache_k, cache_v, rope = cache_k_a[:, j], cache_v_a[:, j], None
        else:
            n_kv, w_qkv, w_o, qn, kn = N_KV_B, w_qkv_b[j], w_o_b[j], q_norm_b[j], k_norm_b[j]
            cache_k, cache_v, rope = cache_k_b[:, j], cache_v_b[:, j], rope_tables
        kv_dim = n_kv * HEAD_DIM
        w_qkv = w_qkv.astype(_BF16)
        qn, kn = qn.reshape(1, HEAD_DIM), kn.reshape(1, HEAD_DIM)
        qkv_p, kv_p = _qkv(x, mod, g1, w_qkv, qn, kn, n_kv, row0=0, m=mp, rope_tables=None, emit_kv=True)
        (qkv_s,) = _qkv(x, mod, g1, w_qkv, qn, kn, n_kv, row0=mp, m=ms, rope_tables=rope, emit_kv=False)
        new_kv[2 * (i % 2)].append(kv_p[:, :kv_dim].reshape(batch, seq, n_kv, HEAD_DIM))
        new_kv[2 * (i % 2) + 1].append(kv_p[:, kv_dim:].reshape(batch, seq, n_kv, HEAD_DIM))

        attn = _context_attention(qkv_p, n_kv, seq, mp + ms)
        k_ctx = cache_k.reshape(dec_batch, -1, kv_dim).astype(_BF16)
        v_ctx = cache_v.reshape(dec_batch, -1, kv_dim).astype(_BF16)
        if i % 2 == 0:
            attn = _neighbourhood_attention(attn, qkv_s, k_ctx, v_ctx, _na_bias_table(rpb_a[j]), mp)
        else:
            attn = _gqa_attention(attn, qkv_s, k_ctx, v_ctx, mp)

        x1, h2, ri, rf, counts = _proj_router(attn, x, mod, g2, w_o.astype(_BF16), w_router[i],
                                              b_router[i].reshape(1, LANES))
        x = _moe(x1, h2, ri, rf, counts, mod, w_gate[i].astype(_BF16), w_up[i].astype(_BF16),
                 w_down[i].astype(_BF16))

    return (x[:mp].reshape(batch, seq, d), x[mp:].reshape(dec_batch, dec_seq, d),
            jnp.stack(new_kv[0], axis=1), jnp.stack(new_kv[1], axis=1),
            jnp.stack(new_kv[2], axis=1), jnp.stack(new_kv[3], axis=1))
```

```python
import functools

import jax
import jax.numpy as jnp
from jax import lax
from jax.experimental import pallas as pl
from jax.experimental.pallas import tpu as pltpu

D_MODEL = 2048
HEAD_DIM = 128
N_HEADS = D_MODEL // HEAD_DIM
N_KV_B = 4
GRID_W = 64
WIN_R = 8
WIN_C = 16
ROPE_THETA = 10000.0
N_GROUPS = 4
EXPERTS_PER_GROUP = 8
N_EXPERTS = N_GROUPS * EXPERTS_PER_GROUP
D_EXPERT = 768
EPS = 1e-6
NEG_INF = -1e30

LANES = 128
MOD_ROWS = 8
GROUP_ROWS = 4096
MOE_ROWS = 256
NA_Q_ROWS = 8
NA_K_ROWS = 16
NA_TABLE = 30
VMEM_LIMIT = 56 * 1024 * 1024

_BF16 = jnp.bfloat16
_F32 = jnp.float32


def _params(sem, vmem=VMEM_LIMIT):
    return pltpu.CompilerParams(dimension_semantics=sem, vmem_limit_bytes=vmem)


def _dot(a, b):
    return jnp.dot(a, b, preferred_element_type=_F32)


def _dot_nt(a, b):
    return lax.dot_general(a, b, (((1,), (1,)), ((), ())), preferred_element_type=_F32)


def _silu(x):
    return x / (1.0 + jnp.exp(-x))


def _cast_kernel(w_ref, o_ref):
    o_ref[...] = w_ref[...].astype(o_ref.dtype)


def _to_bf16(w):
    cols = w.shape[-1]
    w2 = w.reshape(-1, cols)
    rows = w2.shape[0]
    rb = max(8, min(rows, (4 * 1024 * 1024) // (4 * cols)) // 8 * 8)
    while rows % rb:
        rb -= 8
    out = pl.pallas_call(
        _cast_kernel,
        grid=(rows // rb,),
        in_specs=[pl.BlockSpec((rb, cols), lambda i: (i, 0))],
        out_specs=pl.BlockSpec((rb, cols), lambda i: (i, 0)),
        out_shape=jax.ShapeDtypeStruct((rows, cols), _BF16),
        compiler_params=_params(("arbitrary",)),
        name="to_bf16",
    )(w2)
    return out.reshape(w.shape)


def _mod_kernel(cond_ref, w_ref, b_ref, o_ref):
    s = _silu(cond_ref[...]).astype(_BF16)
    o_ref[...] = _dot(s, w_ref[...].astype(_BF16)) + b_ref[...]


def _modulation(cond, w_mod, b_mod):
    depth, d, n = w_mod.shape
    tn = 1024
    return pl.pallas_call(
        _mod_kernel,
        grid=(depth, n // tn),
        in_specs=[
            pl.BlockSpec((MOD_ROWS, d), lambda l, j: (0, 0)),
            pl.BlockSpec((None, d, tn), lambda l, j: (l, 0, j)),
            pl.BlockSpec((None, 1, tn), lambda l, j: (l, 0, j)),
        ],
        out_specs=pl.BlockSpec((None, MOD_ROWS, tn), lambda l, j: (l, 0, j)),
        out_shape=jax.ShapeDtypeStruct((depth, MOD_ROWS, n), _F32),
        compiler_params=_params(("arbitrary", "arbitrary")),
        name="modulation",
    )(cond, w_mod, b_mod.reshape(depth, 1, n))


def _mod_slices(mod_ref, which):
    return mod_ref[0, :, which * D_MODEL:(which + 1) * D_MODEL]


def _norm_modulate(x, g, shift, scale):
    y = x * lax.rsqrt(jnp.mean(x * x, axis=-1, keepdims=True) + EPS)
    return (y * g) * (1.0 + scale) + shift


def _head_norm(a, g):
    return a * lax.rsqrt(jnp.mean(a * a, axis=-1, keepdims=True) + EPS) * g


def _rope(a, cos, sin):
    lane = lax.broadcasted_iota(jnp.int32, a.shape, 1)
    first = (lane % 64) < 32
    swapped = jnp.where(first, pltpu.roll(a, 96, axis=1), pltpu.roll(a, 32, axis=1))
    return a * cos + swapped * sin


def _qkv_kernel(*refs, n_q, n_k, heads_per_tile, rope, emit_kv):
    x_ref, mod_ref, g_ref, w_ref, qn_ref, kn_ref = refs[:6]
    refs = refs[6:]
    if rope:
        cos_ref, sin_ref = refs[:2]
        refs = refs[2:]
    o_ref = refs[0]
    kv_ref = refs[1] if emit_kv else None
    h_scr = refs[-1]
    j = pl.program_id(1)

    @pl.when(j == 0)
    def _():
        h = _norm_modulate(x_ref[...], g_ref[...], _mod_slices(mod_ref, 0), _mod_slices(mod_ref, 1))
        h_scr[...] = h.astype(_BF16)

    acc = _dot(h_scr[...], w_ref[...])

    def heads(gain_ref, scale):
        outs = []
        for hh in range(heads_per_tile):
            a = _head_norm(acc[:, hh * HEAD_DIM:(hh + 1) * HEAD_DIM], gain_ref[...])
            if rope:
                a = _rope(a, cos_ref[...], sin_ref[...])
            outs.append(a * scale if scale != 1.0 else a)
        return jnp.concatenate(outs, axis=1)

    @pl.when(j < n_q)
    def _():
        o_ref[...] = heads(qn_ref, HEAD_DIM ** -0.5).astype(_BF16)

    @pl.when((j >= n_q) & (j < n_q + n_k))
    def _():
        k = heads(kn_ref, 1.0)
        o_ref[...] = k.astype(_BF16)
        if emit_kv:
            kv_ref[...] = k

    @pl.when(j >= n_q + n_k)
    def _():
        o_ref[...] = acc.astype(_BF16)
        if emit_kv:
            kv_ref[...] = acc


def _qkv(x, mod, g, w, layer, qn, kn, n_kv, *, row0, m, rope_tables, emit_kv):
    d = x.shape[1]
    n = w.shape[-1]
    tm, tn = 1024, 512
    heads_per_tile = tn // HEAD_DIM
    n_q = D_MODEL // tn
    n_k = n_kv * HEAD_DIM // tn
    rope = rope_tables is not None
    blocks_per_group = GROUP_ROWS // tm
    blk0 = row0 // tm

    def group(i):
        return (blk0 + i) // blocks_per_group

    in_specs = [
        pl.BlockSpec((tm, d), lambda i, j: (blk0 + i, 0)),
        pl.BlockSpec((1, 1, 6 * d), lambda i, j: (group(i), 0, 0)),
        pl.BlockSpec((1, d), lambda i, j: (0, 0)),
        pl.BlockSpec((None, d, tn), lambda i, j: (layer, 0, j)),
        pl.BlockSpec((1, HEAD_DIM), lambda i, j: (0, 0)),
        pl.BlockSpec((1, HEAD_DIM), lambda i, j: (0, 0)),
    ]
    args = [x, mod, g, w, qn, kn]
    if rope:
        pos = lambda i, j: (i % blocks_per_group, 0)
        in_specs += [pl.BlockSpec((tm, HEAD_DIM), pos), pl.BlockSpec((tm, HEAD_DIM), pos)]
        args += list(rope_tables)
    out_specs = [pl.BlockSpec((tm, tn), lambda i, j: (i, j))]
    out_shape = [jax.ShapeDtypeStruct((m, n), _BF16)]
    if emit_kv:
        out_specs.append(pl.BlockSpec((tm, tn), lambda i, j: (i, jnp.maximum(j - n_q, 0))))
        out_shape.append(jax.ShapeDtypeStruct((m, n - D_MODEL), _F32))
    return pl.pallas_call(
        functools.partial(_qkv_kernel, n_q=n_q, n_k=n_k, heads_per_tile=heads_per_tile,
                          rope=rope, emit_kv=emit_kv),
        grid=(m // tm, n // tn),
        in_specs=in_specs,
        out_specs=out_specs,
        out_shape=out_shape,
        scratch_shapes=[pltpu.VMEM((tm, d), _BF16)],
        compiler_params=_params(("arbitrary", "arbitrary")),
        name="qkv_rope" if rope else "qkv",
    )(*args)


def _rope_tables(n_tokens):
    t = jnp.arange(n_tokens)
    quarter = HEAD_DIM // 4
    inv = ROPE_THETA ** (-jnp.arange(quarter, dtype=_F32) / quarter)
    ang_r = (t // GRID_W).astype(_F32)[:, None] * inv
    ang_c = (t % GRID_W).astype(_F32)[:, None] * inv
    cr, sr, cc, sc = jnp.cos(ang_r), jnp.sin(ang_r), jnp.cos(ang_c), jnp.sin(ang_c)
    return (jnp.concatenate([cr, cr, cc, cc], axis=1), jnp.concatenate([-sr, sr, -sc, sc], axis=1))


def _softmax_pv(scores, values):
    m = functools.reduce(jnp.maximum, [jnp.max(s, axis=-1, keepdims=True) for s in scores])
    ps = [jnp.exp(s - m) for s in scores]
    l = functools.reduce(jnp.add, [jnp.sum(p, axis=-1, keepdims=True) for p in ps])
    o = functools.reduce(jnp.add, [_dot(p.astype(_BF16), v) for p, v in zip(ps, values)])
    return o / l


def _ctx_attn_kernel(q_ref, k_ref, v_ref, o_ref, *, group):
    for h in range(N_HEADS):
        kv = h // group
        q = q_ref[:, h * HEAD_DIM:(h + 1) * HEAD_DIM]
        k = k_ref[:, kv * HEAD_DIM:(kv + 1) * HEAD_DIM]
        v = v_ref[:, kv * HEAD_DIM:(kv + 1) * HEAD_DIM]
        o = _softmax_pv([_dot_nt(q, k)], [v])
        o_ref[:, h * HEAD_DIM:(h + 1) * HEAD_DIM] = o.astype(_BF16)


def _context_attention(qkv, n_kv, seq, n_rows_total):
    m = qkv.shape[0]
    kv_dim = n_kv * HEAD_DIM
    k_blk = D_MODEL // kv_dim
    return pl.pallas_call(
        functools.partial(_ctx_attn_kernel, group=N_HEADS // n_kv),
        grid=(m // seq,),
        in_specs=[
            pl.BlockSpec((seq, D_MODEL), lambda b: (b, 0)),
            pl.BlockSpec((seq, kv_dim), lambda b: (b, k_blk)),
            pl.BlockSpec((seq, kv_dim), lambda b: (b, k_blk + 1)),
        ],
        out_specs=pl.BlockSpec((seq, D_MODEL), lambda b: (b, 0)),
        out_shape=jax.ShapeDtypeStruct((n_rows_total, D_MODEL), _BF16),
        compiler_params=_params(("arbitrary",)),
        name="context_attention",
    )(qkv, qkv, qkv)


def _na_table_kernel(rpb_ref, o_ref):
    h = pl.program_id(0)
    c = lax.broadcasted_iota(jnp.int32, (GRID_W, GRID_W), 0)
    kc = lax.broadcasted_iota(jnp.int32, (GRID_W, GRID_W), 1)
    start = jnp.clip(c - WIN_C // 2, 0, GRID_W - WIN_C)
    col_ok = (kc >= start) & (kc < start + WIN_C)
    rel = kc - c + WIN_C - 1
    n_dc = 2 * WIN_C - 1
    tiles = []
    for d in range(-8, NA_TABLE - 8 + 1):
        if 0 <= d < 2 * WIN_R - 1:
            t = jnp.zeros((GRID_W, GRID_W), _F32)
            for jj in range(n_dc):
                t = jnp.where(rel == jj, rpb_ref[h, d * n_dc + jj], t)
            tiles.append(jnp.where(col_ok, t, NEG_INF))
        else:
            tiles.append(jnp.zeros((GRID_W, GRID_W), _F32))
    for i in range(NA_TABLE):
        o_ref[0, i] = jnp.concatenate([tiles[i], tiles[i + 1]], axis=1)


def _na_bias_table(rpb):
    h = rpb.shape[0]
    return pl.pallas_call(
        _na_table_kernel,
        grid=(h,),
        in_specs=[pl.BlockSpec(memory_space=pltpu.SMEM)],
        out_specs=pl.BlockSpec((1, NA_TABLE, GRID_W, 2 * GRID_W), lambda i: (i, 0, 0, 0)),
        out_shape=jax.ShapeDtypeStruct((h, NA_TABLE, GRID_W, 2 * GRID_W), _F32),
        compiler_params=_params(("arbitrary",)),
        name="na_bias_table",
    )(rpb.reshape(h, -1))


def _na_attn_kernel(alias_ref, q_ref, k_ref, v_ref, kc_ref, vc_ref, tab_ref, o_ref, *, rows):
    del alias_ref
    blk = pl.program_id(2)
    rs = blk * NA_Q_ROWS
    ks = jnp.clip(rs - WIN_R // 2, 0, rows - NA_K_ROWS)
    k0 = pl.multiple_of(ks * GRID_W, 4 * GRID_W)
    n_keys = NA_K_ROWS * GRID_W
    q = q_ref[...]
    k = k_ref[pl.ds(k0, n_keys), :]
    v = v_ref[pl.ds(k0, n_keys), :]
    s = _dot_nt(q, k)
    tile0 = ks - rs + 2 * WIN_R - 1
    bias = jnp.concatenate(
        [jnp.concatenate([tab_ref[0, tile0 - qi + 2 * p] for p in range(NA_K_ROWS // 2)], axis=1)
         for qi in range(NA_Q_ROWS)], axis=0)
    r = rs + lax.broadcasted_iota(jnp.int32, (NA_Q_ROWS * GRID_W, 1), 0) // GRID_W
    r0 = jnp.clip(r - WIN_R // 2, 0, rows - WIN_R)
    kr = ks + lax.broadcasted_iota(jnp.int32, (1, n_keys), 1) // GRID_W
    row_ok = (kr >= r0) & (kr < r0 + WIN_R)
    s = jnp.where(row_ok, s + bias, NEG_INF)
    s_ctx = _dot_nt(q, kc_ref[0])
    o_ref[...] = _softmax_pv([s, s_ctx], [v, vc_ref[0]]).astype(_BF16)


def _neighbourhood_attention(attn, qkv, k_ctx, v_ctx, table, row0):
    b, l, _ = k_ctx.shape
    t = qkv.shape[0] // b
    rows = t // GRID_W
    tq = NA_Q_ROWS * GRID_W
    nblk = t // tq
    out_blk0 = row0 // tq
    return pl.pallas_call(
        functools.partial(_na_attn_kernel, rows=rows),
        grid=(b, N_HEADS, nblk),
        in_specs=[
            pl.BlockSpec(memory_space=pl.ANY),
            pl.BlockSpec((tq, HEAD_DIM), lambda bi, h, i: (bi * nblk + i, h)),
            pl.BlockSpec((t, HEAD_DIM), lambda bi, h, i: (bi, N_HEADS + h)),
            pl.BlockSpec((t, HEAD_DIM), lambda bi, h, i: (bi, 2 * N_HEADS + h)),
            pl.BlockSpec((1, l, HEAD_DIM), lambda bi, h, i: (bi, 0, h)),
            pl.BlockSpec((1, l, HEAD_DIM), lambda bi, h, i: (bi, 0, h)),
            pl.BlockSpec((1, NA_TABLE, GRID_W, 2 * GRID_W), lambda bi, h, i: (h, 0, 0, 0)),
        ],
        out_specs=pl.BlockSpec((tq, HEAD_DIM), lambda bi, h, i: (out_blk0 + bi * nblk + i, h)),
        out_shape=jax.ShapeDtypeStruct(attn.shape, attn.dtype),
        input_output_aliases={0: 0},
        compiler_params=_params(("arbitrary", "arbitrary", "arbitrary")),
        name="neighbourhood_attention",
    )(attn, qkv, qkv, qkv, k_ctx, v_ctx, table)


def _gqa_attn_kernel(alias_ref, q_ref, k_ref, v_ref, kc_ref, vc_ref, o_ref, *, group, chunk):
    del alias_ref
    tq = q_ref.shape[0]
    q = jnp.concatenate([q_ref[:, g * HEAD_DIM:(g + 1) * HEAD_DIM] for g in range(group)], axis=0)
    t = k_ref.shape[0]
    m = l = acc = None
    for c in range(t // chunk + 1):
        if c < t // chunk:
            k = k_ref[c * chunk:(c + 1) * chunk, :]
            v = v_ref[c * chunk:(c + 1) * chunk, :]
        else:
            k, v = kc_ref[0], vc_ref[0]
        s = _dot_nt(q, k)
        m_c = jnp.max(s, axis=-1, keepdims=True)
        if m is None:
            m = m_c
            p = jnp.exp(s - m)
            l = jnp.sum(p, axis=-1, keepdims=True)
            acc = _dot(p.astype(_BF16), v)
        else:
            m_new = jnp.maximum(m, m_c)
            alpha = jnp.exp(m - m_new)
            p = jnp.exp(s - m_new)
            l = alpha * l + jnp.sum(p, axis=-1, keepdims=True)
            acc = alpha * acc + _dot(p.astype(_BF16), v)
            m = m_new
    o = acc / l
    for g in range(group):
        o_ref[:, g * HEAD_DIM:(g + 1) * HEAD_DIM] = o[g * tq:(g + 1) * tq].astype(_BF16)


def _gqa_attention(attn, qkv, k_ctx, v_ctx, row0):
    b, l, kv_dim = k_ctx.shape
    n_kv = kv_dim // HEAD_DIM
    group = N_HEADS // n_kv
    t = qkv.shape[0] // b
    tq = 128
    nblk = t // tq
    out_blk0 = row0 // tq
    return pl.pallas_call(
        functools.partial(_gqa_attn_kernel, group=group, chunk=1024),
        grid=(b, n_kv, nblk),
        in_specs=[
            pl.BlockSpec(memory_space=pl.ANY),
            pl.BlockSpec((tq, group * HEAD_DIM), lambda bi, h, i: (bi * nblk + i, h)),
            pl.BlockSpec((t, HEAD_DIM), lambda bi, h, i: (bi, N_HEADS + h)),
            pl.BlockSpec((t, HEAD_DIM), lambda bi, h, i: (bi, N_HEADS + n_kv + h)),
            pl.BlockSpec((1, l, HEAD_DIM), lambda bi, h, i: (bi, 0, h)),
            pl.BlockSpec((1, l, HEAD_DIM), lambda bi, h, i: (bi, 0, h)),
        ],
        out_specs=pl.BlockSpec((tq, group * HEAD_DIM), lambda bi, h, i: (out_blk0 + bi * nblk + i, h)),
        out_shape=jax.ShapeDtypeStruct(attn.shape, attn.dtype),
        input_output_aliases={0: 0},
        compiler_params=_params(("arbitrary", "arbitrary", "arbitrary")),
        name="gqa_attention",
    )(attn, qkv, qkv, qkv, k_ctx, v_ctx)


def _proj_router_kernel(a_ref, x_ref, mod_ref, g_ref, wo_ref, wr_ref, br_ref,
                        x1_ref, h_ref, ri_ref, rf_ref, cnt_ref, carry):
    i = pl.program_id(0)
    tm = a_ref.shape[0]

    @pl.when(i == 0)
    def _():
        carry[...] = jnp.zeros_like(carry)

    o = _dot(a_ref[...], wo_ref[...])
    x1 = x_ref[...] + _mod_slices(mod_ref, 2) * o
    x1_ref[...] = x1
    h = _norm_modulate(x1, g_ref[...], _mod_slices(mod_ref, 3), _mod_slices(mod_ref, 4))
    h_ref[...] = h

    logits = _dot(h.astype(_BF16), wr_ref[...]) + br_ref[...]
    lane = lax.broadcasted_iota(jnp.int32, logits.shape, 1).astype(_F32)

    def masked_softmax(mask):
        z = jnp.where(mask, logits, -jnp.inf)
        e = jnp.exp(z - jnp.max(z, axis=-1, keepdims=True))
        return e / jnp.sum(e, axis=-1, keepdims=True)

    def top1(p, mask):
        best = jnp.max(jnp.where(mask, p, -1.0), axis=-1, keepdims=True)
        idx = jnp.min(jnp.where(mask & (p == best), lane, float(LANES)), axis=-1, keepdims=True)
        return best, idx

    g_mask = lane < N_GROUPS
    g_p, g_idx = top1(masked_softmax(g_mask), g_mask)
    lo = N_GROUPS + g_idx * EXPERTS_PER_GROUP
    e_mask = (lane >= lo) & (lane < lo + EXPERTS_PER_GROUP)
    e_prob = masked_softmax(e_mask)
    p1, i1 = top1(e_prob, e_mask)
    p2, i2 = top1(e_prob, e_mask & (lane != i1))
    e1, e2 = i1 - N_GROUPS, i2 - N_GROUPS
    gate1 = g_p * p1 / (p1 + p2)
    gate2 = g_p * p2 / (p1 + p2)

    hit1, hit2 = lane == e1, lane == e2
    onehot = jnp.where(hit1 | hit2, 1.0, 0.0)
    row = lax.broadcasted_iota(jnp.int32, (tm, tm), 0)
    col = lax.broadcasted_iota(jnp.int32, (tm, tm), 1)
    lower = jnp.where(row > col, 1.0, 0.0).astype(_BF16)
    before = _dot(lower, onehot.astype(_BF16)) + carry[...]
    rank1 = jnp.sum(jnp.where(hit1, before, 0.0), axis=-1, keepdims=True)
    rank2 = jnp.sum(jnp.where(hit2, before, 0.0), axis=-1, keepdims=True)
    carry[...] = carry[...] + jnp.sum(onehot, axis=0, keepdims=True)

    record = jnp.where(lane == 0, e1, jnp.where(lane == 1, e2,
                       jnp.where(lane == 2, rank1, jnp.where(lane == 3, rank2, 0.0))))
    ri_ref[...] = record.T[:MOD_ROWS].astype(jnp.int32)
    rf_ref[...] = jnp.where(lane == 0, gate1, jnp.where(lane == 1, gate2, 0.0))
    cnt_ref[...] = jnp.broadcast_to(carry[...], cnt_ref.shape)


def _proj_router(attn, x, mod, g, w_o, layer, w_r, b_r):
    m, d = x.shape
    tm = 256
    blocks_per_group = GROUP_ROWS // tm
    row = lambda i: (i, 0)
    const = lambda i: (0, 0)
    return pl.pallas_call(
        _proj_router_kernel,
        grid=(m // tm,),
        in_specs=[
            pl.BlockSpec((tm, d), row),
            pl.BlockSpec((tm, d), row),
            pl.BlockSpec((1, 1, 6 * d), lambda i: (i // blocks_per_group, 0, 0)),
            pl.BlockSpec((1, d), const),
            pl.BlockSpec((None, d, d), lambda i: (layer, 0, 0)),
            pl.BlockSpec((d, LANES), const),
            pl.BlockSpec((1, LANES), const),
        ],
        out_specs=[
            pl.BlockSpec((tm, d), row),
            pl.BlockSpec((tm, d), row),
            pl.BlockSpec((MOD_ROWS, tm), lambda i: (0, i)),
            pl.BlockSpec((tm, LANES), row),
            pl.BlockSpec((MOD_ROWS, LANES), const),
        ],
        out_shape=[
            jax.ShapeDtypeStruct((m, d), _F32),
            jax.ShapeDtypeStruct((m, d), _F32),
            jax.ShapeDtypeStruct((MOD_ROWS, m), jnp.int32),
            jax.ShapeDtypeStruct((m, LANES), _F32),
            jax.ShapeDtypeStruct((MOD_ROWS, LANES), _F32),
        ],
        scratch_shapes=[pltpu.VMEM((1, LANES), _F32)],
        compiler_params=_params(("arbitrary",)),
        name="proj_router",
    )(attn, x, mod, g, w_o, w_r, b_r)


def _row_dest(starts_ref, idx, slot, k, r):
    return starts_ref[idx[slot, k, r]] + idx[slot, 2 + k, r]


def _record_copy(ri_hbm, idx, isem, blk, slot, tm):
    return pltpu.make_async_copy(ri_hbm.at[:, pl.ds(blk * tm, tm)], idx.at[slot], isem.at[slot])


def _dispatch_kernel(starts_ref, ri_hbm, h_ref, xs_hbm, idx, isem, sem):
    i = pl.program_id(0)
    n = pl.num_programs(0)
    tm = h_ref.shape[0]

    @pl.when(i == 0)
    def _():
        _record_copy(ri_hbm, idx, isem, 0, 0, tm).start()

    @pl.when(i + 1 < n)
    def _():
        _record_copy(ri_hbm, idx, isem, i + 1, (i + 1) % 2, tm).start()

    slot = i % 2
    _record_copy(ri_hbm, idx, isem, i, slot, tm).wait()

    def body(r, carry):
        for k in range(2):
            d = _row_dest(starts_ref, idx, slot, k, r)
            pltpu.make_async_copy(h_ref.at[pl.ds(r, 1)], xs_hbm.at[pl.ds(d, 1)], sem).start()
        return carry

    lax.fori_loop(0, tm, body, 0, unroll=8)
    for _ in range(2):
        pltpu.make_async_copy(h_ref, xs_hbm.at[pl.ds(0, tm)], sem).wait()


def _dispatch(h2, ri, starts, p_rows):
    m, d = h2.shape
    tm = 256
    return pl.pallas_call(
        _dispatch_kernel,
        grid_spec=pltpu.PrefetchScalarGridSpec(
            num_scalar_prefetch=1,
            grid=(m // tm,),
            in_specs=[
                pl.BlockSpec(memory_space=pl.ANY),
                pl.BlockSpec((tm, d), lambda i, st: (i, 0)),
            ],
            out_specs=pl.BlockSpec(memory_space=pl.ANY),
            scratch_shapes=[
                pltpu.SMEM((2, MOD_ROWS, tm), jnp.int32),
                pltpu.SemaphoreType.DMA((2,)),
                pltpu.SemaphoreType.DMA,
            ],
        ),
        out_shape=jax.ShapeDtypeStruct((p_rows, d), h2.dtype),
        compiler_params=_params(("arbitrary",)),
        name="dispatch",
    )(starts, ri, h2)


def _expert_kernel(be_ref, valid_ref, x_ref, wg_ref, wu_ref, wd_ref, o_ref):
    b = pl.program_id(0)
    valid = valid_ref[b]

    @pl.when(valid > 0)
    def _():
        row = lax.broadcasted_iota(jnp.int32, x_ref.shape, 0)
        x = jnp.where(row < valid, x_ref[...], 0.0).astype(_BF16)
        hidden = _silu(_dot(x, wg_ref[...])) * _dot(x, wu_ref[...])
        o_ref[...] = _dot(hidden.astype(_BF16), wd_ref[...])

    @pl.when(valid <= 0)
    def _():
        o_ref[...] = jnp.zeros_like(o_ref)


def _experts(xs, block_e, block_valid, w_gate, w_up, w_down, layer):
    p, d = xs.shape
    de = w_gate.shape[-1]
    nb = p // MOE_ROWS
    return pl.pallas_call(
        _expert_kernel,
        grid_spec=pltpu.PrefetchScalarGridSpec(
            num_scalar_prefetch=2,
            grid=(nb,),
            in_specs=[
                pl.BlockSpec((MOE_ROWS, d), lambda b, be, nv: (b, 0)),
                pl.BlockSpec((None, None, d, de), lambda b, be, nv: (layer, be[b], 0, 0)),
                pl.BlockSpec((None, None, d, de), lambda b, be, nv: (layer, be[b], 0, 0)),
                pl.BlockSpec((None, None, de, d), lambda b, be, nv: (layer, be[b], 0, 0)),
            ],
            out_specs=pl.BlockSpec((MOE_ROWS, d), lambda b, be, nv: (b, 0)),
        ),
        out_shape=jax.ShapeDtypeStruct((p, d), _F32),
        compiler_params=_params(("arbitrary",)),
        name="experts",
    )(block_e, block_valid, xs, w_gate, w_up, w_down)


def _combine_kernel(starts_ref, ri_hbm, yb_hbm, x_ref, mod_ref, rf_ref, o_ref, idx, buf, isem, sem):
    i = pl.program_id(0)
    n = pl.num_programs(0)
    tm = x_ref.shape[0]

    def fetch_rows(slot, islot):
        def body(r, carry):
            for k in range(2):
                d = _row_dest(starts_ref, idx, islot, k, r)
                pltpu.make_async_copy(yb_hbm.at[pl.ds(d, 1)], buf.at[slot, k, pl.ds(r, 1)],
                                      sem.at[slot]).start()
            return carry
        lax.fori_loop(0, tm, body, 0, unroll=8)

    @pl.when(i == 0)
    def _():
        _record_copy(ri_hbm, idx, isem, 0, 0, tm).start()

        @pl.when(n > 1)
        def _():
            _record_copy(ri_hbm, idx, isem, 1, 1, tm).start()

        _record_copy(ri_hbm, idx, isem, 0, 0, tm).wait()
        fetch_rows(0, 0)

    @pl.when(i + 2 < n)
    def _():
        _record_copy(ri_hbm, idx, isem, i + 2, (i + 2) % 3, tm).start()

    @pl.when(i + 1 < n)
    def _():
        _record_copy(ri_hbm, idx, isem, i + 1, (i + 1) % 3, tm).wait()
        fetch_rows((i + 1) % 2, (i + 1) % 3)

    slot = i % 2
    for k in range(2):
        pltpu.make_async_copy(yb_hbm.at[pl.ds(0, tm)], buf.at[slot, k], sem.at[slot]).wait()
    gate = rf_ref[...]
    y = buf[slot, 0] * gate[:, 0:1] + buf[slot, 1] * gate[:, 1:2]
    o_ref[...] = x_ref[...] + _mod_slices(mod_ref, 5) * y


def _combine(x1, mod, yb, ri, rf, starts):
    m, d = x1.shape
    tm = 256
    blocks_per_group = GROUP_ROWS // tm
    row = lambda i, st: (i, 0)
    return pl.pallas_call(
        _combine_kernel,
        grid_spec=pltpu.PrefetchScalarGridSpec(
            num_scalar_prefetch=1,
            grid=(m // tm,),
            in_specs=[
                pl.BlockSpec(memory_space=pl.ANY),
                pl.BlockSpec(memory_space=pl.ANY),
                pl.BlockSpec((tm, d), row),
                pl.BlockSpec((1, 1, 6 * d), lambda i, st: (i // blocks_per_group, 0, 0)),
                pl.BlockSpec((tm, LANES), row),
            ],
            out_specs=pl.BlockSpec((tm, d), row),
            scratch_shapes=[
                pltpu.SMEM((3, MOD_ROWS, tm), jnp.int32),
                pltpu.VMEM((2, 2, tm, d), _F32),
                pltpu.SemaphoreType.DMA((3,)),
                pltpu.SemaphoreType.DMA((2,)),
            ],
        ),
        out_shape=jax.ShapeDtypeStruct((m, d), _F32),
        compiler_params=_params(("arbitrary",)),
        name="combine",
    )(starts, ri, yb, x1, mod, rf)


def _moe(x1, h2, ri, rf, counts, mod, w_gate, w_up, w_down, layer):
    m = x1.shape[0]
    p_rows = (2 * m + N_EXPERTS * (MOE_ROWS - 1) + MOE_ROWS - 1) // MOE_ROWS * MOE_ROWS
    nb = p_rows // MOE_ROWS
    cnt = counts[0, :N_EXPERTS].astype(jnp.int32)
    blocks_e = (cnt + MOE_ROWS - 1) // MOE_ROWS
    end_blk = jnp.cumsum(blocks_e)
    start_blk = end_blk - blocks_e
    starts = (start_blk * MOE_ROWS).astype(jnp.int32)
    blk = jnp.arange(nb, dtype=jnp.int32)
    block_e = jnp.minimum(jnp.sum(end_blk[None, :] <= blk[:, None], axis=1), N_EXPERTS - 1).astype(jnp.int32)
    mine = block_e[:, None] == jnp.arange(N_EXPERTS, dtype=jnp.int32)[None, :]
    cnt_b = jnp.sum(jnp.where(mine, cnt[None, :], 0), axis=1)
    start_b = jnp.sum(jnp.where(mine, start_blk[None, :], 0), axis=1)
    block_valid = jnp.clip(cnt_b - (blk - start_b) * MOE_ROWS, 0, MOE_ROWS).astype(jnp.int32)
    xs = _dispatch(h2, ri, starts, p_rows)
    yb = _experts(xs, block_e, block_valid, w_gate, w_up, w_down, layer)
    return _combine(x1, mod, yb, ri, rf, starts)


def kernel(x_prompt, x_sample, c, cache_k_a, cache_v_a, cache_k_b, cache_v_b, c_ctx, w_mod, b_mod, norm1_g, norm2_g, w_qkv_a, w_o_a, q_norm_a, k_norm_a, rpb_a, w_qkv_b, w_o_b, q_norm_b, k_norm_b, w_router_group, b_router_group, w_router_expert, b_router_expert, w_gate, w_up, w_down):
    batch, seq, d = x_prompt.shape
    dec_batch, dec_seq, _ = x_sample.shape
    depth = w_mod.shape[0]
    mp = batch * seq
    ms = dec_batch * dec_seq
    assert mp == GROUP_ROWS and dec_seq == GROUP_ROWS and d == D_MODEL
    x = jnp.concatenate([x_prompt.reshape(mp, d), x_sample.reshape(ms, d)], axis=0)

    cond = jnp.concatenate([c_ctx[None], c, jnp.zeros((MOD_ROWS - 1 - dec_batch, d), _F32)], axis=0)
    mod_all = _modulation(cond, w_mod, b_mod)
    rope_tables = _rope_tables(dec_seq)

    w_router = jnp.concatenate([w_router_group, w_router_expert], axis=-1)
    n_route = w_router.shape[-1]
    w_router = jnp.pad(w_router, ((0, 0), (0, 0), (0, LANES - n_route))).astype(_BF16)
    b_router = jnp.pad(jnp.concatenate([b_router_group, b_router_expert], axis=-1),
                       ((0, 0), (0, LANES - n_route)))

    w_qkv_a, w_o_a, w_qkv_b, w_o_b = (_to_bf16(w) for w in (w_qkv_a, w_o_a, w_qkv_b, w_o_b))
    w_gate, w_up, w_down = (_to_bf16(w) for w in (w_gate, w_up, w_down))

    new_kv = [[], [], [], []]
    for i in range(depth):
        j = i // 2
        mod = mod_all[i].reshape(MOD_ROWS, 1, 6 * d)
        g1 = norm1_g[i].reshape(1, d)
        g2 = norm2_g[i].reshape(1, d)
        if i % 2 == 0:
            n_kv, w_qkv, w_o, qn, kn = N_HEADS, w_qkv_a, w_o_a, q_norm_a[j], k_norm_a[j]
            cache_k, cache_v, rope = cache_k_a[:, j], cache_v_a[:, j], None
        else:
            n_kv, w_qkv, w_o, qn, kn = N_KV_B, w_qkv_b, w_o_b, q_norm_b[j], k_norm_b[j]
            cache_k, cache_v, rope = cache_k_b[:, j], cache_v_b[:, j], rope_tables
        kv_dim = n_kv * HEAD_DIM
        qn, kn = qn.reshape(1, HEAD_DIM), kn.reshape(1, HEAD_DIM)
        qkv_p, kv_p = _qkv(x, mod, g1, w_qkv, j, qn, kn, n_kv, row0=0, m=mp, rope_tables=None, emit_kv=True)
        (qkv_s,) = _qkv(x, mod, g1, w_qkv, j, qn, kn, n_kv, row0=mp, m=ms, rope_tables=rope, emit_kv=False)
        new_kv[2 * (i % 2)].append(kv_p[:, :kv_dim].reshape(batch, seq, n_kv, HEAD_DIM))
        new_kv[2 * (i % 2) + 1].append(kv_p[:, kv_dim:].reshape(batch, seq, n_kv, HEAD_DIM))

        attn = _context_attention(qkv_p, n_kv, seq, mp + ms)
        k_ctx = cache_k.reshape(dec_batch, -1, kv_dim).astype(_BF16)
        v_ctx = cache_v.reshape(dec_batch, -1, kv_dim).astype(_BF16)
        if i % 2 == 0:
            attn = _neighbourhood_attention(attn, qkv_s, k_ctx, v_ctx, _na_bias_table(rpb_a[j]), mp)
        else:
            attn = _gqa_attention(attn, qkv_s, k_ctx, v_ctx, mp)

        x1, h2, ri, rf, counts = _proj_router(attn, x, mod, g2, w_o, j, w_router[i],
                                              b_router[i].reshape(1, LANES))
        x = _moe(x1, h2, ri, rf, counts, mod, w_gate, w_up, w_down, i)

    return (x[:mp].reshape(batch, seq, d), x[mp:].reshape(dec_batch, dec_seq, d),
            jnp.stack(new_kv[0], axis=1), jnp.stack(new_kv[1], axis=1),
            jnp.stack(new_kv[2], axis=1), jnp.stack(new_kv[3], axis=1))
```

```python
import functools
import math

import numpy as np

import jax
import jax.numpy as jnp
from jax import lax
from jax.experimental import pallas as pl
from jax.experimental.pallas import tpu as pltpu

D_MODEL = 2048
HEAD_DIM = 128
N_HEADS = D_MODEL // HEAD_DIM
N_KV_B = 4
GRID_W = 64
WIN_R = 8
WIN_C = 16
ROPE_THETA = 10000.0
N_GROUPS = 4
EXPERTS_PER_GROUP = 8
N_EXPERTS = N_GROUPS * EXPERTS_PER_GROUP
D_EXPERT = 768
EPS = 1e-6
NEG_INF = -1e30

LANES = 128
MOD_ROWS = 8
GROUP_ROWS = 4096
MOE_ROWS = 256
NA_Q_ROWS = 8
NA_K_ROWS = 16
NA_PAIRS = 30
NA_TABLE = 3 * NA_PAIRS + 1
NA_HEADS = 2
VMEM_LIMIT = 56 * 1024 * 1024
LOG2_E = math.log2(math.e)
Q_SCALE = HEAD_DIM ** -0.5 * LOG2_E

_BF16 = jnp.bfloat16
_F32 = jnp.float32


def _params(sem, vmem=VMEM_LIMIT):
    return pltpu.CompilerParams(dimension_semantics=sem, vmem_limit_bytes=vmem)


def _dot(a, b):
    return jnp.dot(a, b, preferred_element_type=_F32)


def _dot_nt(a, b):
    return lax.dot_general(a, b, (((1,), (1,)), ((), ())), preferred_element_type=_F32)


def _silu(x):
    return x / (1.0 + jnp.exp(-x))


def _cast_kernel(w_ref, o_ref):
    o_ref[...] = w_ref[...].astype(o_ref.dtype)


def _to_bf16(w):
    cols = w.shape[-1]
    w2 = w.reshape(-1, cols)
    rows = w2.shape[0]
    rb = max(8, min(rows, (4 * 1024 * 1024) // (4 * cols)) // 8 * 8)
    while rows % rb:
        rb -= 8
    out = pl.pallas_call(
        _cast_kernel,
        grid=(rows // rb,),
        in_specs=[pl.BlockSpec((rb, cols), lambda i: (i, 0))],
        out_specs=pl.BlockSpec((rb, cols), lambda i: (i, 0)),
        out_shape=jax.ShapeDtypeStruct((rows, cols), _BF16),
        compiler_params=_params(("arbitrary",)),
        name="to_bf16",
    )(w2)
    return out.reshape(w.shape)


def _mod_kernel(cond_ref, w_ref, b_ref, o_ref):
    s = _silu(cond_ref[...]).astype(_BF16)
    o_ref[...] = _dot(s, w_ref[...].astype(_BF16)) + b_ref[...]


def _modulation(cond, w_mod, b_mod):
    depth, d, n = w_mod.shape
    tn = 1024
    return pl.pallas_call(
        _mod_kernel,
        grid=(depth, n // tn),
        in_specs=[
            pl.BlockSpec((MOD_ROWS, d), lambda l, j: (0, 0)),
            pl.BlockSpec((None, d, tn), lambda l, j: (l, 0, j)),
            pl.BlockSpec((None, 1, tn), lambda l, j: (l, 0, j)),
        ],
        out_specs=pl.BlockSpec((None, MOD_ROWS, tn), lambda l, j: (l, 0, j)),
        out_shape=jax.ShapeDtypeStruct((depth, MOD_ROWS, n), _F32),
        compiler_params=_params(("arbitrary", "arbitrary")),
        name="modulation",
    )(cond, w_mod, b_mod.reshape(depth, 1, n))


def _mod_slices(mod_ref, which):
    return mod_ref[0, :, which * D_MODEL:(which + 1) * D_MODEL]


def _norm_modulate(x, g, shift, scale):
    y = x * lax.rsqrt(jnp.mean(x * x, axis=-1, keepdims=True) + EPS)
    return (y * g) * (1.0 + scale) + shift


def _head_norm(a, g):
    return a * lax.rsqrt(jnp.mean(a * a, axis=-1, keepdims=True) + EPS) * g


def _rope(a, cos, sin):
    lane = lax.broadcasted_iota(jnp.int32, a.shape, 1)
    first = (lane % 64) < 32
    swapped = jnp.where(first, pltpu.roll(a, 96, axis=1), pltpu.roll(a, 32, axis=1))
    return a * cos + swapped * sin


def _qkv_kernel(*refs, n_q, n_k, heads_per_tile, rope, emit_kv):
    x_ref, mod_ref, g_ref, w_ref, qn_ref, kn_ref = refs[:6]
    refs = refs[6:]
    if rope:
        cos_ref, sin_ref = refs[:2]
        refs = refs[2:]
    o_ref = refs[0]
    kv_ref = refs[1] if emit_kv else None
    h_scr = refs[-1]
    j = pl.program_id(1)

    @pl.when(j == 0)
    def _():
        h = _norm_modulate(x_ref[...], g_ref[...], _mod_slices(mod_ref, 0), _mod_slices(mod_ref, 1))
        h_scr[...] = h.astype(_BF16)

    acc = _dot(h_scr[...], w_ref[...])

    def heads(gain_ref, scale):
        outs = []
        for hh in range(heads_per_tile):
            a = _head_norm(acc[:, hh * HEAD_DIM:(hh + 1) * HEAD_DIM], gain_ref[...])
            if rope:
                a = _rope(a, cos_ref[...], sin_ref[...])
            outs.append(a * scale if scale != 1.0 else a)
        return jnp.concatenate(outs, axis=1)

    @pl.when(j < n_q)
    def _():
        o_ref[...] = heads(qn_ref, Q_SCALE).astype(_BF16)

    @pl.when((j >= n_q) & (j < n_q + n_k))
    def _():
        k = heads(kn_ref, 1.0)
        o_ref[...] = k.astype(_BF16)
        if emit_kv:
            kv_ref[...] = k

    @pl.when(j >= n_q + n_k)
    def _():
        o_ref[...] = acc.astype(_BF16)
        if emit_kv:
            kv_ref[...] = acc


def _qkv(x, mod, g, w, layer, qn, kn, n_kv, *, row0, m, rope_tables, emit_kv):
    d = x.shape[1]
    n = w.shape[-1]
    tm, tn = 1024, 512
    heads_per_tile = tn // HEAD_DIM
    n_q = D_MODEL // tn
    n_k = n_kv * HEAD_DIM // tn
    rope = rope_tables is not None
    blocks_per_group = GROUP_ROWS // tm
    blk0 = row0 // tm

    def group(i):
        return (blk0 + i) // blocks_per_group

    in_specs = [
        pl.BlockSpec((tm, d), lambda i, j: (blk0 + i, 0)),
        pl.BlockSpec((1, 1, 6 * d), lambda i, j: (group(i), 0, 0)),
        pl.BlockSpec((1, d), lambda i, j: (0, 0)),
        pl.BlockSpec((None, d, tn), lambda i, j: (layer, 0, j)),
        pl.BlockSpec((1, HEAD_DIM), lambda i, j: (0, 0)),
        pl.BlockSpec((1, HEAD_DIM), lambda i, j: (0, 0)),
    ]
    args = [x, mod, g, w, qn, kn]
    if rope:
        pos = lambda i, j: (i % blocks_per_group, 0)
        in_specs += [pl.BlockSpec((tm, HEAD_DIM), pos), pl.BlockSpec((tm, HEAD_DIM), pos)]
        args += list(rope_tables)
    out_specs = [pl.BlockSpec((tm, tn), lambda i, j: (i, j))]
    out_shape = [jax.ShapeDtypeStruct((m, n), _BF16)]
    if emit_kv:
        out_specs.append(pl.BlockSpec((tm, tn), lambda i, j: (i, jnp.maximum(j - n_q, 0))))
        out_shape.append(jax.ShapeDtypeStruct((m, n - D_MODEL), _F32))
    return pl.pallas_call(
        functools.partial(_qkv_kernel, n_q=n_q, n_k=n_k, heads_per_tile=heads_per_tile,
                          rope=rope, emit_kv=emit_kv),
        grid=(m // tm, n // tn),
        in_specs=in_specs,
        out_specs=out_specs,
        out_shape=out_shape,
        scratch_shapes=[pltpu.VMEM((tm, d), _BF16)],
        compiler_params=_params(("arbitrary", "arbitrary")),
        name="qkv_rope" if rope else "qkv",
    )(*args)


def _rope_tables(n_tokens):
    t = jnp.arange(n_tokens)
    quarter = HEAD_DIM // 4
    inv = ROPE_THETA ** (-jnp.arange(quarter, dtype=_F32) / quarter)
    ang_r = (t // GRID_W).astype(_F32)[:, None] * inv
    ang_c = (t % GRID_W).astype(_F32)[:, None] * inv
    cr, sr, cc, sc = jnp.cos(ang_r), jnp.sin(ang_r), jnp.cos(ang_c), jnp.sin(ang_c)
    return (jnp.concatenate([cr, cr, cc, cc], axis=1), jnp.concatenate([-sr, sr, -sc, sc], axis=1))


def _softmax_pv(scores, values):
    m = functools.reduce(jnp.maximum, [jnp.max(s, axis=-1, keepdims=True) for s in scores])
    ps = [jnp.exp2(s - m) for s in scores]
    l = functools.reduce(jnp.add, [jnp.sum(p, axis=-1, keepdims=True) for p in ps])
    o = functools.reduce(jnp.add, [_dot(p.astype(_BF16), v) for p, v in zip(ps, values)])
    return o / l


def _ctx_attn_kernel(q_ref, k_ref, v_ref, o_ref, *, group):
    for h in range(N_HEADS):
        kv = h // group
        q = q_ref[:, h * HEAD_DIM:(h + 1) * HEAD_DIM]
        k = k_ref[:, kv * HEAD_DIM:(kv + 1) * HEAD_DIM]
        v = v_ref[:, kv * HEAD_DIM:(kv + 1) * HEAD_DIM]
        o = _softmax_pv([_dot_nt(q, k)], [v])
        o_ref[:, h * HEAD_DIM:(h + 1) * HEAD_DIM] = o.astype(_BF16)


def _context_attention(qkv, n_kv, seq, n_rows_total):
    m = qkv.shape[0]
    kv_dim = n_kv * HEAD_DIM
    k_blk = D_MODEL // kv_dim
    return pl.pallas_call(
        functools.partial(_ctx_attn_kernel, group=N_HEADS // n_kv),
        grid=(m // seq,),
        in_specs=[
            pl.BlockSpec((seq, D_MODEL), lambda b: (b, 0)),
            pl.BlockSpec((seq, kv_dim), lambda b: (b, k_blk)),
            pl.BlockSpec((seq, kv_dim), lambda b: (b, k_blk + 1)),
        ],
        out_specs=pl.BlockSpec((seq, D_MODEL), lambda b: (b, 0)),
        out_shape=jax.ShapeDtypeStruct((n_rows_total, D_MODEL), _BF16),
        compiler_params=_params(("arbitrary",)),
        name="context_attention",
    )(qkv, qkv, qkv)


def _na_table_kernel(rpb_ref, o_ref):
    h = pl.program_id(0)
    c = lax.broadcasted_iota(jnp.int32, (GRID_W, GRID_W), 0)
    kc = lax.broadcasted_iota(jnp.int32, (GRID_W, GRID_W), 1)
    start = jnp.clip(c - WIN_C // 2, 0, GRID_W - WIN_C)
    col_ok = (kc >= start) & (kc < start + WIN_C)
    rel = kc - c + WIN_C - 1
    n_dc = 2 * WIN_C - 1
    masked = jnp.full((GRID_W, GRID_W), NEG_INF, _F32)
    tiles = []
    for d in range(-8, NA_PAIRS - 8 + 1):
        if 0 <= d < 2 * WIN_R - 1:
            t = jnp.zeros((GRID_W, GRID_W), _F32)
            for jj in range(n_dc):
                t = jnp.where(rel == jj, rpb_ref[h, d * n_dc + jj] * LOG2_E, t)
            tiles.append(jnp.where(col_ok, t, NEG_INF))
        else:
            tiles.append(masked)
    for i in range(NA_PAIRS):
        o_ref[0, i] = jnp.concatenate([tiles[i], tiles[i + 1]], axis=1)
        o_ref[0, NA_PAIRS + i] = jnp.concatenate([tiles[i], masked], axis=1)
        o_ref[0, 2 * NA_PAIRS + i] = jnp.concatenate([masked, tiles[i + 1]], axis=1)
    o_ref[0, 3 * NA_PAIRS] = jnp.concatenate([masked, masked], axis=1)


def _na_bias_table(rpb):
    h = rpb.shape[0]
    return pl.pallas_call(
        _na_table_kernel,
        grid=(h,),
        in_specs=[pl.BlockSpec(memory_space=pltpu.SMEM)],
        out_specs=pl.BlockSpec((1, NA_TABLE, GRID_W, 2 * GRID_W), lambda i: (i, 0, 0, 0)),
        out_shape=jax.ShapeDtypeStruct((h, NA_TABLE, GRID_W, 2 * GRID_W), _F32),
        compiler_params=_params(("arbitrary",)),
        name="na_bias_table",
    )(rpb.reshape(h, -1))


def _na_tile_plan(rows):
    nblk = rows // NA_Q_ROWS
    plan = np.zeros((nblk, NA_Q_ROWS * (NA_K_ROWS // 2)), np.int32)
    for blk in range(nblk):
        rs = blk * NA_Q_ROWS
        ks = min(max(rs - WIN_R // 2, 0), rows - NA_K_ROWS)
        for qi in range(NA_Q_ROWS):
            r0 = min(max(rs + qi - WIN_R // 2, 0), rows - WIN_R)
            for p in range(NA_K_ROWS // 2):
                left = r0 <= ks + 2 * p < r0 + WIN_R
                right = r0 <= ks + 2 * p + 1 < r0 + WIN_R
                pair = ks - rs + 2 * WIN_R - 1 - qi + 2 * p
                if left and right:
                    entry = pair
                elif left:
                    entry = NA_PAIRS + pair
                elif right:
                    entry = 2 * NA_PAIRS + pair
                else:
                    entry = 3 * NA_PAIRS
                plan[blk, qi * (NA_K_ROWS // 2) + p] = entry
    return plan


def _na_attn_kernel(plan_ref, alias_ref, q_ref, k_ref, v_ref, kc_ref, vc_ref, tab_ref, o_ref, *, rows):
    del alias_ref
    blk = pl.program_id(2)
    rs = blk * NA_Q_ROWS
    ks = jnp.clip(rs - WIN_R // 2, 0, rows - NA_K_ROWS)
    k0 = pl.multiple_of(ks * GRID_W, 4 * GRID_W)
    n_keys = NA_K_ROWS * GRID_W
    pairs = NA_K_ROWS // 2
    for hh in range(NA_HEADS):
        cols = slice(hh * HEAD_DIM, (hh + 1) * HEAD_DIM)
        q = q_ref[:, cols]
        k = k_ref[pl.ds(k0, n_keys), cols]
        v = v_ref[pl.ds(k0, n_keys), cols]
        bias = jnp.concatenate(
            [jnp.concatenate([tab_ref[hh, plan_ref[blk, qi * pairs + p]] for p in range(pairs)], axis=1)
             for qi in range(NA_Q_ROWS)], axis=0)
        s = _dot_nt(q, k) + bias
        s_ctx = _dot_nt(q, kc_ref[0, :, cols])
        o_ref[:, cols] = _softmax_pv([s, s_ctx], [v, vc_ref[0, :, cols]]).astype(_BF16)


def _neighbourhood_attention(attn, qkv, k_ctx, v_ctx, table, row0):
    b, l, _ = k_ctx.shape
    t = qkv.shape[0] // b
    rows = t // GRID_W
    tq = NA_Q_ROWS * GRID_W
    nblk = t // tq
    w = NA_HEADS * HEAD_DIM
    hp = N_HEADS // NA_HEADS
    out_blk0 = row0 // tq
    return pl.pallas_call(
        functools.partial(_na_attn_kernel, rows=rows),
        grid_spec=pltpu.PrefetchScalarGridSpec(
            num_scalar_prefetch=1,
            grid=(b, hp, nblk),
            in_specs=[
                pl.BlockSpec(memory_space=pl.ANY),
                pl.BlockSpec((tq, w), lambda bi, h, i, plan: (bi * nblk + i, h)),
                pl.BlockSpec((t, w), lambda bi, h, i, plan: (bi, hp + h)),
                pl.BlockSpec((t, w), lambda bi, h, i, plan: (bi, 2 * hp + h)),
                pl.BlockSpec((1, l, w), lambda bi, h, i, plan: (bi, 0, h)),
                pl.BlockSpec((1, l, w), lambda bi, h, i, plan: (bi, 0, h)),
                pl.BlockSpec((NA_HEADS, NA_TABLE, GRID_W, 2 * GRID_W), lambda bi, h, i, plan: (h, 0, 0, 0)),
            ],
            out_specs=pl.BlockSpec((tq, w), lambda bi, h, i, plan: (out_blk0 + bi * nblk + i, h)),
        ),
        out_shape=jax.ShapeDtypeStruct(attn.shape, attn.dtype),
        input_output_aliases={1: 0},
        compiler_params=_params(("arbitrary", "arbitrary", "arbitrary")),
        name="neighbourhood_attention",
    )(jnp.asarray(_na_tile_plan(rows)), attn, qkv, qkv, qkv, k_ctx, v_ctx, table)


def _gqa_attn_kernel(alias_ref, q_ref, k_ref, v_ref, kc_ref, vc_ref, o_ref, *, group, chunk):
    del alias_ref
    tq = q_ref.shape[0]
    q = jnp.concatenate([q_ref[:, g * HEAD_DIM:(g + 1) * HEAD_DIM] for g in range(group)], axis=0)
    t = k_ref.shape[0]
    m = l = acc = None
    for c in range(t // chunk + 1):
        if c < t // chunk:
            k = k_ref[c * chunk:(c + 1) * chunk, :]
            v = v_ref[c * chunk:(c + 1) * chunk, :]
        else:
            k, v = kc_ref[0], vc_ref[0]
        s = _dot_nt(q, k)
        m_c = jnp.max(s, axis=-1, keepdims=True)
        if m is None:
            m = m_c
            p = jnp.exp2(s - m)
            l = jnp.sum(p, axis=-1, keepdims=True)
            acc = _dot(p.astype(_BF16), v)
        else:
            m_new = jnp.maximum(m, m_c)
            alpha = jnp.exp2(m - m_new)
            p = jnp.exp2(s - m_new)
            l = alpha * l + jnp.sum(p, axis=-1, keepdims=True)
            acc = alpha * acc + _dot(p.astype(_BF16), v)
            m = m_new
    o = acc / l
    for g in range(group):
        o_ref[:, g * HEAD_DIM:(g + 1) * HEAD_DIM] = o[g * tq:(g + 1) * tq].astype(_BF16)


def _gqa_attention(attn, qkv, k_ctx, v_ctx, row0):
    b, l, kv_dim = k_ctx.shape
    n_kv = kv_dim // HEAD_DIM
    group = N_HEADS // n_kv
    t = qkv.shape[0] // b
    tq = 256
    nblk = t // tq
    out_blk0 = row0 // tq
    return pl.pallas_call(
        functools.partial(_gqa_attn_kernel, group=group, chunk=1024),
        grid=(b, n_kv, nblk),
        in_specs=[
            pl.BlockSpec(memory_space=pl.ANY),
            pl.BlockSpec((tq, group * HEAD_DIM), lambda bi, h, i: (bi * nblk + i, h)),
            pl.BlockSpec((t, HEAD_DIM), lambda bi, h, i: (bi, N_HEADS + h)),
            pl.BlockSpec((t, HEAD_DIM), lambda bi, h, i: (bi, N_HEADS + n_kv + h)),
            pl.BlockSpec((1, l, HEAD_DIM), lambda bi, h, i: (bi, 0, h)),
            pl.BlockSpec((1, l, HEAD_DIM), lambda bi, h, i: (bi, 0, h)),
        ],
        out_specs=pl.BlockSpec((tq, group * HEAD_DIM), lambda bi, h, i: (out_blk0 + bi * nblk + i, h)),
        out_shape=jax.ShapeDtypeStruct(attn.shape, attn.dtype),
        input_output_aliases={0: 0},
        compiler_params=_params(("arbitrary", "arbitrary", "arbitrary")),
        name="gqa_attention",
    )(attn, qkv, qkv, qkv, k_ctx, v_ctx)


def _pack_bf16_halves(h):
    half = h.shape[1] // 2

    def rounded(v):
        b = lax.bitcast_convert_type(v, jnp.int32)
        return b + 0x7FFF + ((b >> 16) & 1)

    lo, hi = rounded(h[:, :half]), rounded(h[:, half:])
    return (hi & -65536) | ((lo >> 16) & 0xFFFF)


def _unpack_bf16_halves(p):
    lo = lax.bitcast_convert_type(p << 16, _F32)
    hi = lax.bitcast_convert_type(p & -65536, _F32)
    return jnp.concatenate([lo, hi], axis=1).astype(_BF16)


def _route(logits, lane):
    def masked_softmax(mask):
        z = jnp.where(mask, logits, -jnp.inf)
        e = jnp.exp(z - jnp.max(z, axis=-1, keepdims=True))
        return e / jnp.sum(e, axis=-1, keepdims=True)

    def top1(p, mask):
        best = jnp.max(jnp.where(mask, p, -1.0), axis=-1, keepdims=True)
        idx = jnp.min(jnp.where(mask & (p == best), lane, float(LANES)), axis=-1, keepdims=True)
        return best, idx

    g_mask = lane < N_GROUPS
    g_p, g_idx = top1(masked_softmax(g_mask), g_mask)
    lo = N_GROUPS + g_idx * EXPERTS_PER_GROUP
    e_mask = (lane >= lo) & (lane < lo + EXPERTS_PER_GROUP)
    e_prob = masked_softmax(e_mask)
    p1, i1 = top1(e_prob, e_mask)
    p2, i2 = top1(e_prob, e_mask & (lane != i1))
    return i1 - N_GROUPS, i2 - N_GROUPS, g_p * p1 / (p1 + p2), g_p * p2 / (p1 + p2)


def _proj_router_kernel(a_ref, x_ref, mod_ref, g_ref, wo_ref, wr_ref, br_ref,
                        x1_ref, h_ref, ri_ref, rf_ref, cnt_ref, carry):
    i = pl.program_id(0)
    tm = a_ref.shape[0]

    @pl.when(i == 0)
    def _():
        carry[...] = jnp.zeros_like(carry)

    o = _dot(a_ref[...], wo_ref[...])
    x1 = x_ref[...] + _mod_slices(mod_ref, 2) * o
    x1_ref[...] = x1
    h = _norm_modulate(x1, g_ref[...], _mod_slices(mod_ref, 3), _mod_slices(mod_ref, 4))
    h_ref[...] = _pack_bf16_halves(h)

    logits = _dot(h.astype(_BF16), wr_ref[...]) + br_ref[...]
    lane = lax.broadcasted_iota(jnp.int32, logits.shape, 1).astype(_F32)
    e1, e2, gate1, gate2 = _route(logits, lane)

    hit1, hit2 = lane == e1, lane == e2
    onehot = jnp.where(hit1 | hit2, 1.0, 0.0)
    row = lax.broadcasted_iota(jnp.int32, (tm, tm), 0)
    col = lax.broadcasted_iota(jnp.int32, (tm, tm), 1)
    lower = jnp.where(row > col, 1.0, 0.0).astype(_BF16)
    before = _dot(lower, onehot.astype(_BF16)) + carry[...]
    rank1 = jnp.sum(jnp.where(hit1, before, 0.0), axis=-1, keepdims=True)
    rank2 = jnp.sum(jnp.where(hit2, before, 0.0), axis=-1, keepdims=True)
    carry[...] = carry[...] + jnp.sum(onehot, axis=0, keepdims=True)

    record = jnp.where(lane == 0, e1, jnp.where(lane == 1, e2,
                       jnp.where(lane == 2, rank1, jnp.where(lane == 3, rank2, 0.0))))
    ri_ref[...] = record.T[:MOD_ROWS].astype(jnp.int32)
    rf_ref[...] = jnp.where(lane == 0, gate1, jnp.where(lane == 1, gate2, 0.0))
    cnt_ref[...] = jnp.broadcast_to(carry[...], cnt_ref.shape)


def _proj_router(attn, x, mod, g, w_o, layer, w_r, b_r):
    m, d = x.shape
    tm = 256
    blocks_per_group = GROUP_ROWS // tm
    row = lambda i: (i, 0)
    const = lambda i: (0, 0)
    return pl.pallas_call(
        _proj_router_kernel,
        grid=(m // tm,),
        in_specs=[
            pl.BlockSpec((tm, d), row),
            pl.BlockSpec((tm, d), row),
            pl.BlockSpec((1, 1, 6 * d), lambda i: (i // blocks_per_group, 0, 0)),
            pl.BlockSpec((1, d), const),
            pl.BlockSpec((None, d, d), lambda i: (layer, 0, 0)),
            pl.BlockSpec((d, LANES), const),
            pl.BlockSpec((1, LANES), const),
        ],
        out_specs=[
            pl.BlockSpec((tm, d), row),
            pl.BlockSpec((tm, d // 2), row),
            pl.BlockSpec((MOD_ROWS, tm), lambda i: (0, i)),
            pl.BlockSpec((tm, LANES), row),
            pl.BlockSpec((MOD_ROWS, LANES), const),
        ],
        out_shape=[
            jax.ShapeDtypeStruct((m, d), _F32),
            jax.ShapeDtypeStruct((m, d // 2), jnp.int32),
            jax.ShapeDtypeStruct((MOD_ROWS, m), jnp.int32),
            jax.ShapeDtypeStruct((m, LANES), _F32),
            jax.ShapeDtypeStruct((MOD_ROWS, LANES), _F32),
        ],
        scratch_shapes=[pltpu.VMEM((1, LANES), _F32)],
        compiler_params=_params(("arbitrary",)),
        name="proj_router",
    )(attn, x, mod, g, w_o, w_r, b_r)


def _row_dest(starts_ref, idx, slot, k, r):
    return starts_ref[idx[slot, k, r]] + idx[slot, 2 + k, r]


def _record_copy(ri_hbm, idx, isem, blk, slot, tm):
    return pltpu.make_async_copy(ri_hbm.at[:, pl.ds(blk * tm, tm)], idx.at[slot], isem.at[slot])


def _dispatch_kernel(starts_ref, ri_hbm, h_ref, xs_hbm, idx, isem, sem):
    i = pl.program_id(0)
    n = pl.num_programs(0)
    tm = h_ref.shape[0]

    @pl.when(i == 0)
    def _():
        _record_copy(ri_hbm, idx, isem, 0, 0, tm).start()

    @pl.when(i + 1 < n)
    def _():
        _record_copy(ri_hbm, idx, isem, i + 1, (i + 1) % 2, tm).start()

    slot = i % 2
    _record_copy(ri_hbm, idx, isem, i, slot, tm).wait()

    def body(r, carry):
        for k in range(2):
            d = _row_dest(starts_ref, idx, slot, k, r)
            pltpu.make_async_copy(h_ref.at[pl.ds(r, 1)], xs_hbm.at[pl.ds(d, 1)], sem).start()
        return carry

    lax.fori_loop(0, tm, body, 0, unroll=8)
    for _ in range(2):
        pltpu.make_async_copy(h_ref, xs_hbm.at[pl.ds(0, tm)], sem).wait()


def _dispatch(h2, ri, starts, p_rows):
    m, d = h2.shape
    tm = 256
    return pl.pallas_call(
        _dispatch_kernel,
        grid_spec=pltpu.PrefetchScalarGridSpec(
            num_scalar_prefetch=1,
            grid=(m // tm,),
            in_specs=[
                pl.BlockSpec(memory_space=pl.ANY),
                pl.BlockSpec((tm, d), lambda i, st: (i, 0)),
            ],
            out_specs=pl.BlockSpec(memory_space=pl.ANY),
            scratch_shapes=[
                pltpu.SMEM((2, MOD_ROWS, tm), jnp.int32),
                pltpu.SemaphoreType.DMA((2,)),
                pltpu.SemaphoreType.DMA,
            ],
        ),
        out_shape=jax.ShapeDtypeStruct((p_rows, d), h2.dtype),
        compiler_params=_params(("arbitrary",)),
        name="dispatch",
    )(starts, ri, h2)


def _expert_kernel(be_ref, valid_ref, x_ref, wg_ref, wu_ref, wd_ref, o_ref):
    b = pl.program_id(0)
    valid = valid_ref[b]

    @pl.when(valid > 0)
    def _():
        row = lax.broadcasted_iota(jnp.int32, x_ref.shape, 0)
        x = _unpack_bf16_halves(jnp.where(row < valid, x_ref[...], 0))
        hidden = _silu(_dot(x, wg_ref[...])) * _dot(x, wu_ref[...])
        o_ref[...] = _dot(hidden.astype(_BF16), wd_ref[...])

    @pl.when(valid <= 0)
    def _():
        o_ref[...] = jnp.zeros_like(o_ref)


def _experts(xs, block_e, block_valid, w_gate, w_up, w_down, layer):
    p = xs.shape[0]
    d, de = w_gate.shape[-2:]
    nb = p // MOE_ROWS
    return pl.pallas_call(
        _expert_kernel,
        grid_spec=pltpu.PrefetchScalarGridSpec(
            num_scalar_prefetch=2,
            grid=(nb,),
            in_specs=[
                pl.BlockSpec((MOE_ROWS, d // 2), lambda b, be, nv: (b, 0)),
                pl.BlockSpec((None, None, d, de), lambda b, be, nv: (layer, be[b], 0, 0)),
                pl.BlockSpec((None, None, d, de), lambda b, be, nv: (layer, be[b], 0, 0)),
                pl.BlockSpec((None, None, de, d), lambda b, be, nv: (layer, be[b], 0, 0)),
            ],
            out_specs=pl.BlockSpec((MOE_ROWS, d), lambda b, be, nv: (b, 0)),
        ),
        out_shape=jax.ShapeDtypeStruct((p, d), _F32),
        compiler_params=_params(("arbitrary",)),
        name="experts",
    )(block_e, block_valid, xs, w_gate, w_up, w_down)


def _combine_kernel(starts_ref, ri_hbm, yb_hbm, x_ref, mod_ref, rf_ref, o_ref, idx, buf, isem, sem, *, blk0):
    i = pl.program_id(0)
    n = pl.num_programs(0)
    tm = x_ref.shape[0]

    def records(blk, slot):
        return _record_copy(ri_hbm, idx, isem, blk0 + blk, slot, tm)

    def fetch_rows(slot, islot):
        def body(r, carry):
            for k in range(2):
                d = _row_dest(starts_ref, idx, islot, k, r)
                pltpu.make_async_copy(yb_hbm.at[pl.ds(d, 1)], buf.at[slot, k, pl.ds(r, 1)],
                                      sem.at[slot]).start()
            return carry
        lax.fori_loop(0, tm, body, 0, unroll=8)

    @pl.when(i == 0)
    def _():
        records(0, 0).start()

        @pl.when(n > 1)
        def _():
            records(1, 1).start()

        records(0, 0).wait()
        fetch_rows(0, 0)

    @pl.when(i + 2 < n)
    def _():
        records(i + 2, (i + 2) % 3).start()

    @pl.when(i + 1 < n)
    def _():
        records(i + 1, (i + 1) % 3).wait()
        fetch_rows((i + 1) % 2, (i + 1) % 3)

    slot = i % 2
    for k in range(2):
        pltpu.make_async_copy(yb_hbm.at[pl.ds(0, tm)], buf.at[slot, k], sem.at[slot]).wait()
    gate = rf_ref[...]
    y = buf[slot, 0] * gate[:, 0:1] + buf[slot, 1] * gate[:, 1:2]
    o_ref[...] = x_ref[...] + _mod_slices(mod_ref, 5) * y


def _combine(x1, mod, yb, ri, rf, starts, row0, m):
    d = x1.shape[1]
    tm = 256
    blocks_per_group = GROUP_ROWS // tm
    blk0 = row0 // tm
    row = lambda i, st: (blk0 + i, 0)
    return pl.pallas_call(
        functools.partial(_combine_kernel, blk0=blk0),
        grid_spec=pltpu.PrefetchScalarGridSpec(
            num_scalar_prefetch=1,
            grid=(m // tm,),
            in_specs=[
                pl.BlockSpec(memory_space=pl.ANY),
                pl.BlockSpec(memory_space=pl.ANY),
                pl.BlockSpec((tm, d), row),
                pl.BlockSpec((1, 1, 6 * d), lambda i, st: ((blk0 + i) // blocks_per_group, 0, 0)),
                pl.BlockSpec((tm, LANES), row),
            ],
            out_specs=pl.BlockSpec((tm, d), lambda i, st: (i, 0)),
            scratch_shapes=[
                pltpu.SMEM((3, MOD_ROWS, tm), jnp.int32),
                pltpu.VMEM((2, 2, tm, d), _F32),
                pltpu.SemaphoreType.DMA((3,)),
                pltpu.SemaphoreType.DMA((2,)),
            ],
        ),
        out_shape=jax.ShapeDtypeStruct((m, d), _F32),
        compiler_params=_params(("arbitrary",)),
        name="combine",
    )(starts, ri, yb, x1, mod, rf)


def _moe(x1, h2, ri, rf, counts, mod, w_gate, w_up, w_down, layer, splits):
    m = x1.shape[0]
    p_rows = (2 * m + N_EXPERTS * (MOE_ROWS - 1) + MOE_ROWS - 1) // MOE_ROWS * MOE_ROWS
    nb = p_rows // MOE_ROWS
    cnt = counts[0, :N_EXPERTS].astype(jnp.int32)
    blocks_e = (cnt + MOE_ROWS - 1) // MOE_ROWS
    end_blk = jnp.cumsum(blocks_e)
    start_blk = end_blk - blocks_e
    starts = (start_blk * MOE_ROWS).astype(jnp.int32)
    blk = jnp.arange(nb, dtype=jnp.int32)
    block_e = jnp.minimum(jnp.sum(end_blk[None, :] <= blk[:, None], axis=1), N_EXPERTS - 1).astype(jnp.int32)
    mine = block_e[:, None] == jnp.arange(N_EXPERTS, dtype=jnp.int32)[None, :]
    cnt_b = jnp.sum(jnp.where(mine, cnt[None, :], 0), axis=1)
    start_b = jnp.sum(jnp.where(mine, start_blk[None, :], 0), axis=1)
    block_valid = jnp.clip(cnt_b - (blk - start_b) * MOE_ROWS, 0, MOE_ROWS).astype(jnp.int32)
    xs = _dispatch(h2, ri, starts, p_rows)
    yb = _experts(xs, block_e, block_valid, w_gate, w_up, w_down, layer)
    return [_combine(x1, mod, yb, ri, rf, starts, row0, rows) for row0, rows in splits]


def kernel(x_prompt, x_sample, c, cache_k_a, cache_v_a, cache_k_b, cache_v_b, c_ctx, w_mod, b_mod, norm1_g, norm2_g, w_qkv_a, w_o_a, q_norm_a, k_norm_a, rpb_a, w_qkv_b, w_o_b, q_norm_b, k_norm_b, w_router_group, b_router_group, w_router_expert, b_router_expert, w_gate, w_up, w_down):
    batch, seq, d = x_prompt.shape
    dec_batch, dec_seq, _ = x_sample.shape
    depth = w_mod.shape[0]
    mp = batch * seq
    ms = dec_batch * dec_seq
    assert mp == GROUP_ROWS and dec_seq == GROUP_ROWS and d == D_MODEL
    x = jnp.concatenate([x_prompt.reshape(mp, d), x_sample.reshape(ms, d)], axis=0)

    cond = jnp.concatenate([c_ctx[None], c, jnp.zeros((MOD_ROWS - 1 - dec_batch, d), _F32)], axis=0)
    mod_all = _modulation(cond, w_mod, b_mod)
    rope_tables = _rope_tables(dec_seq)

    w_router = jnp.concatenate([w_router_group, w_router_expert], axis=-1)
    n_route = w_router.shape[-1]
    w_router = jnp.pad(w_router, ((0, 0), (0, 0), (0, LANES - n_route))).astype(_BF16)
    b_router = jnp.pad(jnp.concatenate([b_router_group, b_router_expert], axis=-1),
                       ((0, 0), (0, LANES - n_route)))

    w_qkv_a, w_o_a, w_qkv_b, w_o_b = (_to_bf16(w) for w in (w_qkv_a, w_o_a, w_qkv_b, w_o_b))
    w_gate, w_up, w_down = (_to_bf16(w) for w in (w_gate, w_up, w_down))

    new_kv = [[], [], [], []]
    for i in range(depth):
        j = i // 2
        mod = mod_all[i].reshape(MOD_ROWS, 1, 6 * d)
        g1 = norm1_g[i].reshape(1, d)
        g2 = norm2_g[i].reshape(1, d)
        if i % 2 == 0:
            n_kv, w_qkv, w_o, qn, kn = N_HEADS, w_qkv_a, w_o_a, q_norm_a[j], k_norm_a[j]
            cache_k, cache_v, rope = cache_k_a[:, j], cache_v_a[:, j], None
        else:
            n_kv, w_qkv, w_o, qn, kn = N_KV_B, w_qkv_b, w_o_b, q_norm_b[j], k_norm_b[j]
            cache_k, cache_v, rope = cache_k_b[:, j], cache_v_b[:, j], rope_tables
        kv_dim = n_kv * HEAD_DIM
        qn, kn = qn.reshape(1, HEAD_DIM), kn.reshape(1, HEAD_DIM)
        qkv_p, kv_p = _qkv(x, mod, g1, w_qkv, j, qn, kn, n_kv, row0=0, m=mp, rope_tables=None, emit_kv=True)
        (qkv_s,) = _qkv(x, mod, g1, w_qkv, j, qn, kn, n_kv, row0=mp, m=ms, rope_tables=rope, emit_kv=False)
        new_kv[2 * (i % 2)].append(kv_p[:, :kv_dim].reshape(batch, seq, n_kv, HEAD_DIM))
        new_kv[2 * (i % 2) + 1].append(kv_p[:, kv_dim:].reshape(batch, seq, n_kv, HEAD_DIM))

        attn = _context_attention(qkv_p, n_kv, seq, mp + ms)
        k_ctx = cache_k.reshape(dec_batch, -1, kv_dim).astype(_BF16)
        v_ctx = cache_v.reshape(dec_batch, -1, kv_dim).astype(_BF16)
        if i % 2 == 0:
            attn = _neighbourhood_attention(attn, qkv_s, k_ctx, v_ctx, _na_bias_table(rpb_a[j]), mp)
        else:
            attn = _gqa_attention(attn, qkv_s, k_ctx, v_ctx, mp)

        x1, h2, ri, rf, counts = _proj_router(attn, x, mod, g2, w_o, j, w_router[i],
                                              b_router[i].reshape(1, LANES))
        last = i == depth - 1
        outs = _moe(x1, h2, ri, rf, counts, mod, w_gate, w_up, w_down, i,
                    [(0, mp), (mp, ms)] if last else [(0, mp + ms)])
        x = outs[0]

    return (outs[0].reshape(batch, seq, d), outs[1].reshape(dec_batch, dec_seq, d),
            jnp.stack(new_kv[0], axis=1), jnp.stack(new_kv[1], axis=1),
            jnp.stack(new_kv[2], axis=1), jnp.stack(new_kv[3], axis=1))
```

```python
import functools
import math

import numpy as np

import jax
import jax.numpy as jnp
from jax import lax
from jax.experimental import pallas as pl
from jax.experimental.pallas import tpu as pltpu

D_MODEL = 2048
HEAD_DIM = 128
N_HEADS = D_MODEL // HEAD_DIM
N_KV_B = 4
GRID_W = 64
WIN_R = 8
WIN_C = 16
ROPE_THETA = 10000.0
N_GROUPS = 4
EXPERTS_PER_GROUP = 8
N_EXPERTS = N_GROUPS * EXPERTS_PER_GROUP
D_EXPERT = 768
EPS = 1e-6
NEG_INF = -1e30

LANES = 128
MOD_ROWS = 8
GROUP_ROWS = 4096
MOE_ROWS = 256
NA_Q_ROWS = 8
NA_K_ROWS = 16
NA_PAIRS = 30
NA_TABLE = 3 * NA_PAIRS + 1
NA_HEADS = 2
VMEM_LIMIT = 56 * 1024 * 1024
LOG2_E = math.log2(math.e)
Q_SCALE = HEAD_DIM ** -0.5 * LOG2_E

_BF16 = jnp.bfloat16
_F32 = jnp.float32


def _params(sem, vmem=VMEM_LIMIT):
    return pltpu.CompilerParams(dimension_semantics=sem, vmem_limit_bytes=vmem)


def _dot(a, b):
    return jnp.dot(a, b, preferred_element_type=_F32)


def _dot_nt(a, b):
    return lax.dot_general(a, b, (((1,), (1,)), ((), ())), preferred_element_type=_F32)


def _silu(x):
    return x / (1.0 + jnp.exp(-x))


def _cast_kernel(w_ref, o_ref):
    o_ref[...] = w_ref[...].astype(o_ref.dtype)


def _to_bf16(w):
    cols = w.shape[-1]
    w2 = w.reshape(-1, cols)
    rows = w2.shape[0]
    rb = max(8, min(rows, (4 * 1024 * 1024) // (4 * cols)) // 8 * 8)
    while rows % rb:
        rb -= 8
    out = pl.pallas_call(
        _cast_kernel,
        grid=(rows // rb,),
        in_specs=[pl.BlockSpec((rb, cols), lambda i: (i, 0))],
        out_specs=pl.BlockSpec((rb, cols), lambda i: (i, 0)),
        out_shape=jax.ShapeDtypeStruct((rows, cols), _BF16),
        compiler_params=_params(("arbitrary",)),
        name="to_bf16",
    )(w2)
    return out.reshape(w.shape)


def _mod_kernel(cond_ref, w_ref, b_ref, o_ref):
    s = _silu(cond_ref[...]).astype(_BF16)
    o_ref[...] = _dot(s, w_ref[...].astype(_BF16)) + b_ref[...]


def _modulation(cond, w_mod, b_mod):
    depth, d, n = w_mod.shape
    tn = 1024
    return pl.pallas_call(
        _mod_kernel,
        grid=(depth, n // tn),
        in_specs=[
            pl.BlockSpec((MOD_ROWS, d), lambda l, j: (0, 0)),
            pl.BlockSpec((None, d, tn), lambda l, j: (l, 0, j)),
            pl.BlockSpec((None, 1, tn), lambda l, j: (l, 0, j)),
        ],
        out_specs=pl.BlockSpec((None, MOD_ROWS, tn), lambda l, j: (l, 0, j)),
        out_shape=jax.ShapeDtypeStruct((depth, MOD_ROWS, n), _F32),
        compiler_params=_params(("arbitrary", "arbitrary")),
        name="modulation",
    )(cond, w_mod, b_mod.reshape(depth, 1, n))


def _mod_slices(mod_ref, which):
    return mod_ref[0, :, which * D_MODEL:(which + 1) * D_MODEL]


def _norm_modulate(x, g, shift, scale):
    y = x * lax.rsqrt(jnp.mean(x * x, axis=-1, keepdims=True) + EPS)
    return (y * g) * (1.0 + scale) + shift


def _head_norm(a, g):
    return a * lax.rsqrt(jnp.mean(a * a, axis=-1, keepdims=True) + EPS) * g


def _rope(a, cos, sin):
    lane = lax.broadcasted_iota(jnp.int32, a.shape, 1)
    first = (lane % 64) < 32
    swapped = jnp.where(first, pltpu.roll(a, 96, axis=1), pltpu.roll(a, 32, axis=1))
    return a * cos + swapped * sin


def _qkv_kernel(*refs, n_q, n_k, heads_per_tile, rope, emit_kv):
    x_ref, mod_ref, g_ref, w_ref, qn_ref, kn_ref = refs[:6]
    refs = refs[6:]
    if rope:
        cos_ref, sin_ref = refs[:2]
        refs = refs[2:]
    o_ref = refs[0]
    kv_ref = refs[1] if emit_kv else None
    h_scr = refs[-1]
    j = pl.program_id(1)

    @pl.when(j == 0)
    def _():
        h = _norm_modulate(x_ref[...], g_ref[...], _mod_slices(mod_ref, 0), _mod_slices(mod_ref, 1))
        h_scr[...] = h.astype(_BF16)

    acc = _dot(h_scr[...], w_ref[...])

    def heads(gain_ref, scale):
        outs = []
        for hh in range(heads_per_tile):
            a = _head_norm(acc[:, hh * HEAD_DIM:(hh + 1) * HEAD_DIM], gain_ref[...])
            if rope:
                a = _rope(a, cos_ref[...], sin_ref[...])
            outs.append(a * scale if scale != 1.0 else a)
        return jnp.concatenate(outs, axis=1)

    @pl.when(j < n_q)
    def _():
        o_ref[...] = heads(qn_ref, Q_SCALE).astype(_BF16)

    @pl.when((j >= n_q) & (j < n_q + n_k))
    def _():
        k = heads(kn_ref, 1.0)
        o_ref[...] = k.astype(_BF16)
        if emit_kv:
            kv_ref[...] = k

    @pl.when(j >= n_q + n_k)
    def _():
        o_ref[...] = acc.astype(_BF16)
        if emit_kv:
            kv_ref[...] = acc


def _qkv(x, mod, g, w, layer, qn, kn, n_kv, *, row0, m, rope_tables, emit_kv):
    d = x.shape[1]
    n = w.shape[-1]
    tm, tn = 1024, 512
    heads_per_tile = tn // HEAD_DIM
    n_q = D_MODEL // tn
    n_k = n_kv * HEAD_DIM // tn
    rope = rope_tables is not None
    blocks_per_group = GROUP_ROWS // tm
    blk0 = row0 // tm

    def group(i):
        return (blk0 + i) // blocks_per_group

    in_specs = [
        pl.BlockSpec((tm, d), lambda i, j: (blk0 + i, 0)),
        pl.BlockSpec((1, 1, 6 * d), lambda i, j: (group(i), 0, 0)),
        pl.BlockSpec((1, d), lambda i, j: (0, 0)),
        pl.BlockSpec((None, d, tn), lambda i, j: (layer, 0, j)),
        pl.BlockSpec((1, HEAD_DIM), lambda i, j: (0, 0)),
        pl.BlockSpec((1, HEAD_DIM), lambda i, j: (0, 0)),
    ]
    args = [x, mod, g, w, qn, kn]
    if rope:
        pos = lambda i, j: (i % blocks_per_group, 0)
        in_specs += [pl.BlockSpec((tm, HEAD_DIM), pos), pl.BlockSpec((tm, HEAD_DIM), pos)]
        args += list(rope_tables)
    out_specs = [pl.BlockSpec((tm, tn), lambda i, j: (i, j))]
    out_shape = [jax.ShapeDtypeStruct((m, n), _BF16)]
    if emit_kv:
        out_specs.append(pl.BlockSpec((tm, tn), lambda i, j: (i, jnp.maximum(j - n_q, 0))))
        out_shape.append(jax.ShapeDtypeStruct((m, n - D_MODEL), _F32))
    return pl.pallas_call(
        functools.partial(_qkv_kernel, n_q=n_q, n_k=n_k, heads_per_tile=heads_per_tile,
                          rope=rope, emit_kv=emit_kv),
        grid=(m // tm, n // tn),
        in_specs=in_specs,
        out_specs=out_specs,
        out_shape=out_shape,
        scratch_shapes=[pltpu.VMEM((tm, d), _BF16)],
        compiler_params=_params(("arbitrary", "arbitrary")),
        name="qkv_rope" if rope else "qkv",
    )(*args)


def _rope_tables(n_tokens):
    t = jnp.arange(n_tokens)
    quarter = HEAD_DIM // 4
    inv = ROPE_THETA ** (-jnp.arange(quarter, dtype=_F32) / quarter)
    ang_r = (t // GRID_W).astype(_F32)[:, None] * inv
    ang_c = (t % GRID_W).astype(_F32)[:, None] * inv
    cr, sr, cc, sc = jnp.cos(ang_r), jnp.sin(ang_r), jnp.cos(ang_c), jnp.sin(ang_c)
    return (jnp.concatenate([cr, cr, cc, cc], axis=1), jnp.concatenate([-sr, sr, -sc, sc], axis=1))


def _softmax_pv(scores, values):
    m = functools.reduce(jnp.maximum, [jnp.max(s, axis=-1, keepdims=True) for s in scores])
    ps = [jnp.exp2(s - m) for s in scores]
    l = functools.reduce(jnp.add, [jnp.sum(p, axis=-1, keepdims=True) for p in ps])
    o = functools.reduce(jnp.add, [_dot(p.astype(_BF16), v) for p, v in zip(ps, values)])
    return o / l


def _ctx_attn_kernel(q_ref, k_ref, v_ref, o_ref, *, group):
    for h in range(N_HEADS):
        kv = h // group
        q = q_ref[:, h * HEAD_DIM:(h + 1) * HEAD_DIM]
        k = k_ref[:, kv * HEAD_DIM:(kv + 1) * HEAD_DIM]
        v = v_ref[:, kv * HEAD_DIM:(kv + 1) * HEAD_DIM]
        o = _softmax_pv([_dot_nt(q, k)], [v])
        o_ref[:, h * HEAD_DIM:(h + 1) * HEAD_DIM] = o.astype(_BF16)


def _context_attention(qkv, n_kv, seq, n_rows_total):
    m = qkv.shape[0]
    kv_dim = n_kv * HEAD_DIM
    k_blk = D_MODEL // kv_dim
    return pl.pallas_call(
        functools.partial(_ctx_attn_kernel, group=N_HEADS // n_kv),
        grid=(m // seq,),
        in_specs=[
            pl.BlockSpec((seq, D_MODEL), lambda b: (b, 0)),
            pl.BlockSpec((seq, kv_dim), lambda b: (b, k_blk)),
            pl.BlockSpec((seq, kv_dim), lambda b: (b, k_blk + 1)),
        ],
        out_specs=pl.BlockSpec((seq, D_MODEL), lambda b: (b, 0)),
        out_shape=jax.ShapeDtypeStruct((n_rows_total, D_MODEL), _BF16),
        compiler_params=_params(("arbitrary",)),
        name="context_attention",
    )(qkv, qkv, qkv)


def _na_table_kernel(rpb_ref, o_ref):
    h = pl.program_id(0)
    c = lax.broadcasted_iota(jnp.int32, (GRID_W, GRID_W), 0)
    kc = lax.broadcasted_iota(jnp.int32, (GRID_W, GRID_W), 1)
    start = jnp.clip(c - WIN_C // 2, 0, GRID_W - WIN_C)
    col_ok = (kc >= start) & (kc < start + WIN_C)
    rel = kc - c + WIN_C - 1
    n_dc = 2 * WIN_C - 1
    masked = jnp.full((GRID_W, GRID_W), NEG_INF, _F32)
    tiles = []
    for d in range(-8, NA_PAIRS - 8 + 1):
        if 0 <= d < 2 * WIN_R - 1:
            t = jnp.zeros((GRID_W, GRID_W), _F32)
            for jj in range(n_dc):
                t = jnp.where(rel == jj, rpb_ref[h, d * n_dc + jj] * LOG2_E, t)
            tiles.append(jnp.where(col_ok, t, NEG_INF))
        else:
            tiles.append(masked)
    for i in range(NA_PAIRS):
        o_ref[0, i] = jnp.concatenate([tiles[i], tiles[i + 1]], axis=1)
        o_ref[0, NA_PAIRS + i] = jnp.concatenate([tiles[i], masked], axis=1)
        o_ref[0, 2 * NA_PAIRS + i] = jnp.concatenate([masked, tiles[i + 1]], axis=1)
    o_ref[0, 3 * NA_PAIRS] = jnp.concatenate([masked, masked], axis=1)


def _na_bias_table(rpb):
    h = rpb.shape[0]
    return pl.pallas_call(
        _na_table_kernel,
        grid=(h,),
        in_specs=[pl.BlockSpec(memory_space=pltpu.SMEM)],
        out_specs=pl.BlockSpec((1, NA_TABLE, GRID_W, 2 * GRID_W), lambda i: (i, 0, 0, 0)),
        out_shape=jax.ShapeDtypeStruct((h, NA_TABLE, GRID_W, 2 * GRID_W), _F32),
        compiler_params=_params(("arbitrary",)),
        name="na_bias_table",
    )(rpb.reshape(h, -1))


def _na_tile_plan(rows):
    nblk = rows // NA_Q_ROWS
    plan = np.zeros((nblk, NA_Q_ROWS * (NA_K_ROWS // 2)), np.int32)
    for blk in range(nblk):
        rs = blk * NA_Q_ROWS
        ks = min(max(rs - WIN_R // 2, 0), rows - NA_K_ROWS)
        for qi in range(NA_Q_ROWS):
            r0 = min(max(rs + qi - WIN_R // 2, 0), rows - WIN_R)
            for p in range(NA_K_ROWS // 2):
                left = r0 <= ks + 2 * p < r0 + WIN_R
                right = r0 <= ks + 2 * p + 1 < r0 + WIN_R
                pair = ks - rs + 2 * WIN_R - 1 - qi + 2 * p
                if left and right:
                    entry = pair
                elif left:
                    entry = NA_PAIRS + pair
                elif right:
                    entry = 2 * NA_PAIRS + pair
                else:
                    entry = 3 * NA_PAIRS
                plan[blk, qi * (NA_K_ROWS // 2) + p] = entry
    return plan


def _na_attn_kernel(plan_ref, alias_ref, q_ref, k_ref, v_ref, kc_ref, vc_ref, tab_ref, o_ref, *, rows):
    del alias_ref
    blk = pl.program_id(2)
    rs = blk * NA_Q_ROWS
    ks = jnp.clip(rs - WIN_R // 2, 0, rows - NA_K_ROWS)
    k0 = pl.multiple_of(ks * GRID_W, 4 * GRID_W)
    n_keys = NA_K_ROWS * GRID_W
    pairs = NA_K_ROWS // 2
    for hh in range(NA_HEADS):
        cols = slice(hh * HEAD_DIM, (hh + 1) * HEAD_DIM)
        q = q_ref[:, cols]
        k = k_ref[pl.ds(k0, n_keys), cols]
        v = v_ref[pl.ds(k0, n_keys), cols]
        bias = jnp.concatenate(
            [jnp.concatenate([tab_ref[hh, plan_ref[blk, qi * pairs + p]] for p in range(pairs)], axis=1)
             for qi in range(NA_Q_ROWS)], axis=0)
        s = _dot_nt(q, k) + bias
        s_ctx = _dot_nt(q, kc_ref[0, :, cols])
        o_ref[:, cols] = _softmax_pv([s, s_ctx], [v, vc_ref[0, :, cols]]).astype(_BF16)


def _neighbourhood_attention(attn, qkv, k_ctx, v_ctx, table, row0):
    b, l, _ = k_ctx.shape
    t = qkv.shape[0] // b
    rows = t // GRID_W
    tq = NA_Q_ROWS * GRID_W
    nblk = t // tq
    w = NA_HEADS * HEAD_DIM
    hp = N_HEADS // NA_HEADS
    out_blk0 = row0 // tq
    return pl.pallas_call(
        functools.partial(_na_attn_kernel, rows=rows),
        grid_spec=pltpu.PrefetchScalarGridSpec(
            num_scalar_prefetch=1,
            grid=(b, hp, nblk),
            in_specs=[
                pl.BlockSpec(memory_space=pl.ANY),
                pl.BlockSpec((tq, w), lambda bi, h, i, plan: (bi * nblk + i, h)),
                pl.BlockSpec((t, w), lambda bi, h, i, plan: (bi, hp + h)),
                pl.BlockSpec((t, w), lambda bi, h, i, plan: (bi, 2 * hp + h)),
                pl.BlockSpec((1, l, w), lambda bi, h, i, plan: (bi, 0, h)),
                pl.BlockSpec((1, l, w), lambda bi, h, i, plan: (bi, 0, h)),
                pl.BlockSpec((NA_HEADS, NA_TABLE, GRID_W, 2 * GRID_W), lambda bi, h, i, plan: (h, 0, 0, 0)),
            ],
            out_specs=pl.BlockSpec((tq, w), lambda bi, h, i, plan: (out_blk0 + bi * nblk + i, h)),
        ),
        out_shape=jax.ShapeDtypeStruct(attn.shape, attn.dtype),
        input_output_aliases={1: 0},
        compiler_params=_params(("arbitrary", "arbitrary", "arbitrary")),
        name="neighbourhood_attention",
    )(jnp.asarray(_na_tile_plan(rows)), attn, qkv, qkv, qkv, k_ctx, v_ctx, table)


def _gqa_attn_kernel(alias_ref, q_ref, k_ref, v_ref, kc_ref, vc_ref, o_ref, *, group, chunk):
    del alias_ref
    tq = q_ref.shape[0]
    q = jnp.concatenate([q_ref[:, g * HEAD_DIM:(g + 1) * HEAD_DIM] for g in range(group)], axis=0)
    t = k_ref.shape[0]
    m = l = acc = None
    for c in range(t // chunk + 1):
        if c < t // chunk:
            k = k_ref[c * chunk:(c + 1) * chunk, :]
            v = v_ref[c * chunk:(c + 1) * chunk, :]
        else:
            k, v = kc_ref[0], vc_ref[0]
        s = _dot_nt(q, k)
        m_c = jnp.max(s, axis=-1, keepdims=True)
        if m is None:
            m = m_c
            p = jnp.exp2(s - m)
            l = jnp.sum(p, axis=-1, keepdims=True)
            acc = _dot(p.astype(_BF16), v)
        else:
            m_new = jnp.maximum(m, m_c)
            alpha = jnp.exp2(m - m_new)
            p = jnp.exp2(s - m_new)
            l = alpha * l + jnp.sum(p, axis=-1, keepdims=True)
            acc = alpha * acc + _dot(p.astype(_BF16), v)
            m = m_new
    o = acc / l
    for g in range(group):
        o_ref[:, g * HEAD_DIM:(g + 1) * HEAD_DIM] = o[g * tq:(g + 1) * tq].astype(_BF16)


def _gqa_attention(attn, qkv, k_ctx, v_ctx, row0):
    b, l, kv_dim = k_ctx.shape
    n_kv = kv_dim // HEAD_DIM
    group = N_HEADS // n_kv
    t = qkv.shape[0] // b
    tq = 256
    nblk = t // tq
    out_blk0 = row0 // tq
    return pl.pallas_call(
        functools.partial(_gqa_attn_kernel, group=group, chunk=1024),
        grid=(b, n_kv, nblk),
        in_specs=[
            pl.BlockSpec(memory_space=pl.ANY),
            pl.BlockSpec((tq, group * HEAD_DIM), lambda bi, h, i: (bi * nblk + i, h)),
            pl.BlockSpec((t, HEAD_DIM), lambda bi, h, i: (bi, N_HEADS + h)),
            pl.BlockSpec((t, HEAD_DIM), lambda bi, h, i: (bi, N_HEADS + n_kv + h)),
            pl.BlockSpec((1, l, HEAD_DIM), lambda bi, h, i: (bi, 0, h)),
            pl.BlockSpec((1, l, HEAD_DIM), lambda bi, h, i: (bi, 0, h)),
        ],
        out_specs=pl.BlockSpec((tq, group * HEAD_DIM), lambda bi, h, i: (out_blk0 + bi * nblk + i, h)),
        out_shape=jax.ShapeDtypeStruct(attn.shape, attn.dtype),
        input_output_aliases={0: 0},
        compiler_params=_params(("arbitrary", "arbitrary", "arbitrary")),
        name="gqa_attention",
    )(attn, qkv, qkv, qkv, k_ctx, v_ctx)


def _pack_bf16_halves(h):
    half = h.shape[1] // 2

    def rounded(v):
        b = lax.bitcast_convert_type(v, jnp.int32)
        return b + 0x7FFF + ((b >> 16) & 1)

    lo, hi = rounded(h[:, :half]), rounded(h[:, half:])
    return (hi & -65536) | ((lo >> 16) & 0xFFFF)


def _unpack_bf16_halves(p):
    lo = lax.bitcast_convert_type(p << 16, _F32)
    hi = lax.bitcast_convert_type(p & -65536, _F32)
    return jnp.concatenate([lo, hi], axis=1).astype(_BF16)


def _route(logits, lane):
    def masked_softmax(mask):
        z = jnp.where(mask, logits, -jnp.inf)
        e = jnp.exp(z - jnp.max(z, axis=-1, keepdims=True))
        return e / jnp.sum(e, axis=-1, keepdims=True)

    def top1(p, mask):
        best = jnp.max(jnp.where(mask, p, -1.0), axis=-1, keepdims=True)
        idx = jnp.min(jnp.where(mask & (p == best), lane, float(LANES)), axis=-1, keepdims=True)
        return best, idx

    g_mask = lane < N_GROUPS
    g_p, g_idx = top1(masked_softmax(g_mask), g_mask)
    lo = N_GROUPS + g_idx * EXPERTS_PER_GROUP
    e_mask = (lane >= lo) & (lane < lo + EXPERTS_PER_GROUP)
    e_prob = masked_softmax(e_mask)
    p1, i1 = top1(e_prob, e_mask)
    p2, i2 = top1(e_prob, e_mask & (lane != i1))
    return i1 - N_GROUPS, i2 - N_GROUPS, g_p * p1 / (p1 + p2), g_p * p2 / (p1 + p2)


def _proj_router_kernel(a_ref, x_ref, mod_ref, g_ref, wo_ref, wr_ref, br_ref,
                        x1_ref, h_ref, ri_ref, rf_ref, cnt_ref, carry):
    i = pl.program_id(0)
    tm = a_ref.shape[0]

    @pl.when(i == 0)
    def _():
        carry[...] = jnp.zeros_like(carry)

    o = _dot(a_ref[...], wo_ref[...])
    x1 = x_ref[...] + _mod_slices(mod_ref, 2) * o
    x1_ref[...] = x1
    h = _norm_modulate(x1, g_ref[...], _mod_slices(mod_ref, 3), _mod_slices(mod_ref, 4))
    h_ref[...] = _pack_bf16_halves(h)

    logits = _dot(h.astype(_BF16), wr_ref[...]) + br_ref[...]
    lane = lax.broadcasted_iota(jnp.int32, logits.shape, 1).astype(_F32)
    e1, e2, gate1, gate2 = _route(logits, lane)

    hit1, hit2 = lane == e1, lane == e2
    onehot = jnp.where(hit1 | hit2, 1.0, 0.0)
    row = lax.broadcasted_iota(jnp.int32, (tm, tm), 0)
    col = lax.broadcasted_iota(jnp.int32, (tm, tm), 1)
    lower = jnp.where(row > col, 1.0, 0.0).astype(_BF16)
    before = _dot(lower, onehot.astype(_BF16)) + carry[...]
    rank1 = jnp.sum(jnp.where(hit1, before, 0.0), axis=-1, keepdims=True)
    rank2 = jnp.sum(jnp.where(hit2, before, 0.0), axis=-1, keepdims=True)
    carry[...] = carry[...] + jnp.sum(onehot, axis=0, keepdims=True)

    record = jnp.where(lane == 0, e1, jnp.where(lane == 1, e2,
                       jnp.where(lane == 2, rank1, jnp.where(lane == 3, rank2, 0.0))))
    ri_ref[...] = record.T[:MOD_ROWS].astype(jnp.int32)
    rf_ref[...] = jnp.where(lane == 0, gate1, jnp.where(lane == 1, gate2, 0.0))
    cnt_ref[...] = jnp.broadcast_to(carry[...], cnt_ref.shape)


def _proj_router(attn, x, mod, g, w_o, layer, w_r, b_r):
    m, d = x.shape
    tm = 256
    blocks_per_group = GROUP_ROWS // tm
    row = lambda i: (i, 0)
    const = lambda i: (0, 0)
    return pl.pallas_call(
        _proj_router_kernel,
        grid=(m // tm,),
        in_specs=[
            pl.BlockSpec((tm, d), row),
            pl.BlockSpec((tm, d), row),
            pl.BlockSpec((1, 1, 6 * d), lambda i: (i // blocks_per_group, 0, 0)),
            pl.BlockSpec((1, d), const),
            pl.BlockSpec((None, d, d), lambda i: (layer, 0, 0)),
            pl.BlockSpec((d, LANES), const),
            pl.BlockSpec((1, LANES), const),
        ],
        out_specs=[
            pl.BlockSpec((tm, d), row),
            pl.BlockSpec((tm, d // 2), row),
            pl.BlockSpec((MOD_ROWS, tm), lambda i: (0, i)),
            pl.BlockSpec((tm, LANES), row),
            pl.BlockSpec((MOD_ROWS, LANES), const),
        ],
        out_shape=[
            jax.ShapeDtypeStruct((m, d), _F32),
            jax.ShapeDtypeStruct((m, d // 2), jnp.int32),
            jax.ShapeDtypeStruct((MOD_ROWS, m), jnp.int32),
            jax.ShapeDtypeStruct((m, LANES), _F32),
            jax.ShapeDtypeStruct((MOD_ROWS, LANES), _F32),
        ],
        scratch_shapes=[pltpu.VMEM((1, LANES), _F32)],
        compiler_params=_params(("arbitrary",)),
        name="proj_router",
    )(attn, x, mod, g, w_o, w_r, b_r)


def _record_copy(dest_hbm, idx, isem, blk, slot, tm):
    return pltpu.make_async_copy(dest_hbm.at[:, pl.ds(blk * tm, tm)], idx.at[slot], isem.at[slot])


def _start_row_copies(idx, slot, tm, make_copy):
    for chunk in range(tm // LANES):
        def body(j, carry, chunk=chunk):
            base = chunk * LANES + pl.multiple_of(j * 8, 8)
            for u in range(8):
                for k in range(2):
                    make_copy(base + u, k, idx[slot, k, base + u]).start(priority=k)
            return carry
        lax.fori_loop(0, LANES // 8, body, 0)


def _dispatch_kernel(dest_hbm, h_ref, xs_hbm, idx, isem, sem):
    i = pl.program_id(0)
    n = pl.num_programs(0)
    tm = h_ref.shape[0]

    @pl.when(i == 0)
    def _():
        _record_copy(dest_hbm, idx, isem, 0, 0, tm).start()

    @pl.when(i + 1 < n)
    def _():
        _record_copy(dest_hbm, idx, isem, i + 1, (i + 1) % 2, tm).start()

    _record_copy(dest_hbm, idx, isem, i, i % 2, tm).wait()
    for slot in range(2):
        @pl.when(i % 2 == slot)
        def _(slot=slot):
            _start_row_copies(idx, slot, tm, lambda r, k, d: pltpu.make_async_copy(
                h_ref.at[pl.ds(r, 1)], xs_hbm.at[pl.ds(d, 1)], sem))

    for _ in range(2):
        pltpu.make_async_copy(h_ref, xs_hbm.at[pl.ds(0, tm)], sem).wait()


def _dispatch(h2, dest, p_rows):
    m, d = h2.shape
    tm = 256
    return pl.pallas_call(
        _dispatch_kernel,
        grid=(m // tm,),
        in_specs=[
            pl.BlockSpec(memory_space=pl.ANY),
            pl.BlockSpec((tm, d), lambda i: (i, 0)),
        ],
        out_specs=pl.BlockSpec(memory_space=pl.ANY),
        out_shape=jax.ShapeDtypeStruct((p_rows, d), h2.dtype),
        scratch_shapes=[
            pltpu.SMEM((2, MOD_ROWS, tm), jnp.int32),
            pltpu.SemaphoreType.DMA((2,)),
            pltpu.SemaphoreType.DMA,
        ],
        compiler_params=_params(("arbitrary",)),
        name="dispatch",
    )(dest, h2)


def _expert_kernel(be_ref, valid_ref, next_ref, x_ref, wg_hbm, wu_hbm, wd_hbm, o_ref,
                   stage_g, stage_u, stage_d, wg, wu, wd, sem, *, layer):
    b = pl.program_id(0)
    valid = valid_ref[b]
    expert = be_ref[b]

    def fetch(e):
        return [pltpu.make_async_copy(src.at[layer, e], dst, sem)
                for src, dst in ((wg_hbm, stage_g), (wu_hbm, stage_u), (wd_hbm, stage_d))]

    @pl.when(b == 0)
    def _():
        for copy in fetch(expert):
            copy.start()

    first = (b == 0) | (expert != be_ref[jnp.maximum(b - 1, 0)])

    @pl.when(first & (valid > 0))
    def _():
        for copy in fetch(expert):
            copy.wait()
        rows = 256
        for stage, dst in ((stage_g, wg), (stage_u, wu), (stage_d, wd)):
            for r in range(0, stage.shape[0], rows):
                dst[r:r + rows, :] = stage[r:r + rows, :].astype(_BF16)

        @pl.when(next_ref[b] >= 0)
        def _():
            for copy in fetch(next_ref[b]):
                copy.start()

    @pl.when(valid > 0)
    def _():
        row = lax.broadcasted_iota(jnp.int32, x_ref.shape, 0)
        x = _unpack_bf16_halves(jnp.where(row < valid, x_ref[...], 0))
        hidden = _silu(_dot(x, wg[...])) * _dot(x, wu[...])
        o_ref[...] = _dot(hidden.astype(_BF16), wd[...])

    @pl.when(valid <= 0)
    def _():
        o_ref[...] = jnp.zeros_like(o_ref)


def _experts(xs, block_e, block_valid, block_next, w_gate, w_up, w_down, layer):
    p = xs.shape[0]
    d, de = w_gate.shape[-2:]
    nb = p // MOE_ROWS
    return pl.pallas_call(
        functools.partial(_expert_kernel, layer=layer),
        grid_spec=pltpu.PrefetchScalarGridSpec(
            num_scalar_prefetch=3,
            grid=(nb,),
            in_specs=[
                pl.BlockSpec((MOE_ROWS, d // 2), lambda b, be, nv, nx: (b, 0)),
                pl.BlockSpec(memory_space=pl.ANY),
                pl.BlockSpec(memory_space=pl.ANY),
                pl.BlockSpec(memory_space=pl.ANY),
            ],
            out_specs=pl.BlockSpec((MOE_ROWS, d), lambda b, be, nv, nx: (b, 0)),
            scratch_shapes=[
                pltpu.VMEM((d, de), _F32), pltpu.VMEM((d, de), _F32), pltpu.VMEM((de, d), _F32),
                pltpu.VMEM((d, de), _BF16), pltpu.VMEM((d, de), _BF16), pltpu.VMEM((de, d), _BF16),
                pltpu.SemaphoreType.DMA,
            ],
        ),
        out_shape=jax.ShapeDtypeStruct((p, d), _F32),
        compiler_params=_params(("arbitrary",)),
        name="experts",
    )(block_e, block_valid, block_next, xs, w_gate, w_up, w_down)


def _combine_kernel(dest_hbm, yb_hbm, x_ref, mod_ref, rf_ref, o_ref, idx, buf, isem, sem, *, blk0):
    i = pl.program_id(0)
    n = pl.num_programs(0)
    tm = x_ref.shape[0]

    def records(blk, slot):
        return _record_copy(dest_hbm, idx, isem, blk0 + blk, slot, tm)

    def fetch_rows(slot):
        _start_row_copies(idx, slot, tm, lambda r, k, d: pltpu.make_async_copy(
            yb_hbm.at[pl.ds(d, 1)], buf.at[slot, k, pl.ds(r, 1)], sem.at[slot]))

    @pl.when(i == 0)
    def _():
        records(0, 0).start()

        @pl.when(n > 1)
        def _():
            records(1, 1).start()

        records(0, 0).wait()
        fetch_rows(0)

    @pl.when(i + 2 < n)
    def _():
        records(i + 2, i % 2).start()

    for slot in range(2):
        @pl.when((i + 1 < n) & ((i + 1) % 2 == slot))
        def _(slot=slot):
            records(i + 1, slot).wait()
            fetch_rows(slot)

    slot = i % 2
    for k in range(2):
        pltpu.make_async_copy(yb_hbm.at[pl.ds(0, tm)], buf.at[slot, k], sem.at[slot]).wait()
    gate = rf_ref[...]
    y = buf[slot, 0] * gate[:, 0:1] + buf[slot, 1] * gate[:, 1:2]
    o_ref[...] = x_ref[...] + _mod_slices(mod_ref, 5) * y


def _combine(x1, mod, yb, dest, rf, row0, m):
    d = x1.shape[1]
    tm = 256
    blocks_per_group = GROUP_ROWS // tm
    blk0 = row0 // tm
    row = lambda i: (blk0 + i, 0)
    return pl.pallas_call(
        functools.partial(_combine_kernel, blk0=blk0),
        grid=(m // tm,),
        in_specs=[
            pl.BlockSpec(memory_space=pl.ANY),
            pl.BlockSpec(memory_space=pl.ANY),
            pl.BlockSpec((tm, d), row),
            pl.BlockSpec((1, 1, 6 * d), lambda i: ((blk0 + i) // blocks_per_group, 0, 0)),
            pl.BlockSpec((tm, LANES), row),
        ],
        out_specs=pl.BlockSpec((tm, d), lambda i: (i, 0)),
        out_shape=jax.ShapeDtypeStruct((m, d), _F32),
        scratch_shapes=[
            pltpu.SMEM((2, MOD_ROWS, tm), jnp.int32),
            pltpu.VMEM((2, 2, tm, d), _F32),
            pltpu.SemaphoreType.DMA((2,)),
            pltpu.SemaphoreType.DMA((2,)),
        ],
        compiler_params=_params(("arbitrary",)),
        name="combine",
    )(dest, yb, x1, mod, rf)


def _moe(x1, h2, ri, rf, counts, mod, w_gate, w_up, w_down, layer, splits):
    m = x1.shape[0]
    p_rows = (2 * m + N_EXPERTS * (MOE_ROWS - 1) + MOE_ROWS - 1) // MOE_ROWS * MOE_ROWS
    nb = p_rows // MOE_ROWS
    cnt = counts[0, :N_EXPERTS].astype(jnp.int32)
    blocks_e = (cnt + MOE_ROWS - 1) // MOE_ROWS
    end_blk = jnp.cumsum(blocks_e)
    start_blk = end_blk - blocks_e
    starts = (start_blk * MOE_ROWS).astype(jnp.int32)
    blk = jnp.arange(nb, dtype=jnp.int32)
    block_e = jnp.minimum(jnp.sum(end_blk[None, :] <= blk[:, None], axis=1), N_EXPERTS - 1).astype(jnp.int32)
    mine = block_e[:, None] == jnp.arange(N_EXPERTS, dtype=jnp.int32)[None, :]
    cnt_b = jnp.sum(jnp.where(mine, cnt[None, :], 0), axis=1)
    start_b = jnp.sum(jnp.where(mine, start_blk[None, :], 0), axis=1)
    block_valid = jnp.clip(cnt_b - (blk - start_b) * MOE_ROWS, 0, MOE_ROWS).astype(jnp.int32)
    ids = jnp.arange(N_EXPERTS, dtype=jnp.int32)
    later = (ids[None, :] > ids[:, None]) & (cnt[None, :] > 0)
    next_e = jnp.min(jnp.where(later, ids[None, :], N_EXPERTS), axis=1)
    next_e = jnp.where(next_e == N_EXPERTS, -1, next_e)
    block_next = jnp.sum(jnp.where(mine, next_e[None, :], 0), axis=1).astype(jnp.int32)
    experts = jnp.arange(N_EXPERTS, dtype=jnp.int32)[None, :, None]
    dest = jnp.sum(jnp.where(ri[0:2, None, :] == experts, starts[None, :, None], 0), axis=1) + ri[2:4]
    dest = jnp.concatenate([dest, jnp.zeros((MOD_ROWS - 2, m), jnp.int32)], axis=0)
    xs = _dispatch(h2, dest, p_rows)
    yb = _experts(xs, block_e, block_valid, block_next, w_gate, w_up, w_down, layer)
    return [_combine(x1, mod, yb, dest, rf, row0, rows) for row0, rows in splits]


def kernel(x_prompt, x_sample, c, cache_k_a, cache_v_a, cache_k_b, cache_v_b, c_ctx, w_mod, b_mod, norm1_g, norm2_g, w_qkv_a, w_o_a, q_norm_a, k_norm_a, rpb_a, w_qkv_b, w_o_b, q_norm_b, k_norm_b, w_router_group, b_router_group, w_router_expert, b_router_expert, w_gate, w_up, w_down):
    batch, seq, d = x_prompt.shape
    dec_batch, dec_seq, _ = x_sample.shape
    depth = w_mod.shape[0]
    mp = batch * seq
    ms = dec_batch * dec_seq
    assert mp == GROUP_ROWS and dec_seq == GROUP_ROWS and d == D_MODEL
    x = jnp.concatenate([x_prompt.reshape(mp, d), x_sample.reshape(ms, d)], axis=0)

    cond = jnp.concatenate([c_ctx[None], c, jnp.zeros((MOD_ROWS - 1 - dec_batch, d), _F32)], axis=0)
    mod_all = _modulation(cond, w_mod, b_mod)
    rope_tables = _rope_tables(dec_seq)

    w_router = jnp.concatenate([w_router_group, w_router_expert], axis=-1)
    n_route = w_router.shape[-1]
    w_router = jnp.pad(w_router, ((0, 0), (0, 0), (0, LANES - n_route))).astype(_BF16)
    b_router = jnp.pad(jnp.concatenate([b_router_group, b_router_expert], axis=-1),
                       ((0, 0), (0, LANES - n_route)))

    w_qkv_a, w_o_a, w_qkv_b, w_o_b = (_to_bf16(w) for w in (w_qkv_a, w_o_a, w_qkv_b, w_o_b))

    new_kv = [[], [], [], []]
    for i in range(depth):
        j = i // 2
        mod = mod_all[i].reshape(MOD_ROWS, 1, 6 * d)
        g1 = norm1_g[i].reshape(1, d)
        g2 = norm2_g[i].reshape(1, d)
        if i % 2 == 0:
            n_kv, w_qkv, w_o, qn, kn = N_HEADS, w_qkv_a, w_o_a, q_norm_a[j], k_norm_a[j]
            cache_k, cache_v, rope = cache_k_a[:, j], cache_v_a[:, j], None
        else:
            n_kv, w_qkv, w_o, qn, kn = N_KV_B, w_qkv_b, w_o_b, q_norm_b[j], k_norm_b[j]
            cache_k, cache_v, rope = cache_k_b[:, j], cache_v_b[:, j], rope_tables
        kv_dim = n_kv * HEAD_DIM
        qn, kn = qn.reshape(1, HEAD_DIM), kn.reshape(1, HEAD_DIM)
        qkv_p, kv_p = _qkv(x, mod, g1, w_qkv, j, qn, kn, n_kv, row0=0, m=mp, rope_tables=None, emit_kv=True)
        (qkv_s,) = _qkv(x, mod, g1, w_qkv, j, qn, kn, n_kv, row0=mp, m=ms, rope_tables=rope, emit_kv=False)
        new_kv[2 * (i % 2)].append(kv_p[:, :kv_dim].reshape(batch, seq, n_kv, HEAD_DIM))
        new_kv[2 * (i % 2) + 1].append(kv_p[:, kv_dim:].reshape(batch, seq, n_kv, HEAD_DIM))

        attn = _context_attention(qkv_p, n_kv, seq, mp + ms)
        k_ctx = cache_k.reshape(dec_batch, -1, kv_dim).astype(_BF16)
        v_ctx = cache_v.reshape(dec_batch, -1, kv_dim).astype(_BF16)
        if i % 2 == 0:
            attn = _neighbourhood_attention(attn, qkv_s, k_ctx, v_ctx, _na_bias_table(rpb_a[j]), mp)
        else:
            attn = _gqa_attention(attn, qkv_s, k_ctx, v_ctx, mp)

        x1, h2, ri, rf, counts = _proj_router(attn, x, mod, g2, w_o, j, w_router[i],
                                              b_router[i].reshape(1, LANES))
        last = i == depth - 1
        outs = _moe(x1, h2, ri, rf, counts, mod, w_gate, w_up, w_down, i,
                    [(0, mp), (mp, ms)] if last else [(0, mp + ms)])
        x = outs[0]

    return (outs[0].reshape(batch, seq, d), outs[1].reshape(dec_batch, dec_seq, d),
            jnp.stack(new_kv[0], axis=1), jnp.stack(new_kv[1], axis=1),
            jnp.stack(new_kv[2], axis=1), jnp.stack(new_kv[3], axis=1))
```

```python
import functools
import math

import numpy as np

import jax
import jax.numpy as jnp
from jax import lax
from jax.experimental import pallas as pl
from jax.experimental.pallas import tpu as pltpu

D_MODEL = 2048
HEAD_DIM = 128
N_HEADS = D_MODEL // HEAD_DIM
N_KV_B = 4
GRID_W = 64
WIN_R = 8
WIN_C = 16
ROPE_THETA = 10000.0
N_GROUPS = 4
EXPERTS_PER_GROUP = 8
N_EXPERTS = N_GROUPS * EXPERTS_PER_GROUP
D_EXPERT = 768
EPS = 1e-6
NEG_INF = -1e30

LANES = 128
MOD_ROWS = 8
GROUP_ROWS = 4096
MOE_ROWS = 512
NA_Q_ROWS = 8
NA_K_ROWS = 16
NA_PAIRS = 30
NA_TABLE = 3 * NA_PAIRS + 1
NA_HEADS = 2
VMEM_LIMIT = 56 * 1024 * 1024
LOG2_E = math.log2(math.e)
Q_SCALE = HEAD_DIM ** -0.5 * LOG2_E

_BF16 = jnp.bfloat16
_F32 = jnp.float32


def _params(sem, vmem=VMEM_LIMIT):
    return pltpu.CompilerParams(dimension_semantics=sem, vmem_limit_bytes=vmem)


def _dot(a, b):
    return jnp.dot(a, b, preferred_element_type=_F32)


def _dot_nt(a, b):
    return lax.dot_general(a, b, (((1,), (1,)), ((), ())), preferred_element_type=_F32)


def _silu(x):
    return x / (1.0 + jnp.exp(-x))


def _cast_kernel(w_ref, o_ref):
    o_ref[...] = w_ref[...].astype(o_ref.dtype)


def _to_bf16(w):
    cols = w.shape[-1]
    w2 = w.reshape(-1, cols)
    rows = w2.shape[0]
    rb = max(8, min(rows, (4 * 1024 * 1024) // (4 * cols)) // 8 * 8)
    while rows % rb:
        rb -= 8
    out = pl.pallas_call(
        _cast_kernel,
        grid=(rows // rb,),
        in_specs=[pl.BlockSpec((rb, cols), lambda i: (i, 0))],
        out_specs=pl.BlockSpec((rb, cols), lambda i: (i, 0)),
        out_shape=jax.ShapeDtypeStruct((rows, cols), _BF16),
        compiler_params=_params(("arbitrary",)),
        name="to_bf16",
    )(w2)
    return out.reshape(w.shape)


def _mod_kernel(cond_ref, w_ref, b_ref, o_ref):
    s = _silu(cond_ref[...]).astype(_BF16)
    o_ref[...] = _dot(s, w_ref[...].astype(_BF16)) + b_ref[...]


def _modulation(cond, w_mod, b_mod):
    depth, d, n = w_mod.shape
    tn = 1024
    return pl.pallas_call(
        _mod_kernel,
        grid=(depth, n // tn),
        in_specs=[
            pl.BlockSpec((MOD_ROWS, d), lambda l, j: (0, 0)),
            pl.BlockSpec((None, d, tn), lambda l, j: (l, 0, j)),
            pl.BlockSpec((None, 1, tn), lambda l, j: (l, 0, j)),
        ],
        out_specs=pl.BlockSpec((None, MOD_ROWS, tn), lambda l, j: (l, 0, j)),
        out_shape=jax.ShapeDtypeStruct((depth, MOD_ROWS, n), _F32),
        compiler_params=_params(("arbitrary", "arbitrary")),
        name="modulation",
    )(cond, w_mod, b_mod.reshape(depth, 1, n))


def _mod_slices(mod_ref, which):
    return mod_ref[0, :, which * D_MODEL:(which + 1) * D_MODEL]


def _norm_modulate(x, g, shift, scale):
    y = x * lax.rsqrt(jnp.mean(x * x, axis=-1, keepdims=True) + EPS)
    return (y * g) * (1.0 + scale) + shift


def _head_norm(a, g):
    return a * lax.rsqrt(jnp.mean(a * a, axis=-1, keepdims=True) + EPS) * g


def _rope(a, cos, sin):
    lane = lax.broadcasted_iota(jnp.int32, a.shape, 1)
    first = (lane % 64) < 32
    swapped = jnp.where(first, pltpu.roll(a, 96, axis=1), pltpu.roll(a, 32, axis=1))
    return a * cos + swapped * sin


def _qkv_kernel(*refs, n_q, n_k, heads_per_tile, rope, emit_kv):
    x_ref, mod_ref, g_ref, w_ref, qn_ref, kn_ref = refs[:6]
    refs = refs[6:]
    if rope:
        cos_ref, sin_ref = refs[:2]
        refs = refs[2:]
    o_ref = refs[0]
    k32_ref, v32_ref = refs[1:3] if emit_kv else (None, None)
    h_scr = refs[-1]
    j = pl.program_id(1)

    @pl.when(j == 0)
    def _():
        h = _norm_modulate(x_ref[...], g_ref[...], _mod_slices(mod_ref, 0), _mod_slices(mod_ref, 1))
        h_scr[...] = h.astype(_BF16)

    acc = _dot(h_scr[...], w_ref[...])

    def heads(gain_ref, scale):
        outs = []
        for hh in range(heads_per_tile):
            a = _head_norm(acc[:, hh * HEAD_DIM:(hh + 1) * HEAD_DIM], gain_ref[...])
            if rope:
                a = _rope(a, cos_ref[...], sin_ref[...])
            outs.append(a * scale if scale != 1.0 else a)
        return jnp.concatenate(outs, axis=1)

    @pl.when(j < n_q)
    def _():
        o_ref[...] = heads(qn_ref, Q_SCALE).astype(_BF16)

    @pl.when((j >= n_q) & (j < n_q + n_k))
    def _():
        k = heads(kn_ref, 1.0)
        o_ref[...] = k.astype(_BF16)
        if emit_kv:
            k32_ref[...] = k

    @pl.when(j >= n_q + n_k)
    def _():
        o_ref[...] = acc.astype(_BF16)
        if emit_kv:
            v32_ref[...] = acc


def _qkv(x, mod, g, w, layer, qn, kn, n_kv, *, row0, m, rope_tables, emit_kv):
    d = x.shape[1]
    n = w.shape[-1]
    tm, tn = 1024, 512
    heads_per_tile = tn // HEAD_DIM
    n_q = D_MODEL // tn
    n_k = n_kv * HEAD_DIM // tn
    rope = rope_tables is not None
    blocks_per_group = GROUP_ROWS // tm
    blk0 = row0 // tm

    def group(i):
        return (blk0 + i) // blocks_per_group

    in_specs = [
        pl.BlockSpec((tm, d), lambda i, j: (blk0 + i, 0)),
        pl.BlockSpec((1, 1, 6 * d), lambda i, j: (group(i), 0, 0)),
        pl.BlockSpec((1, d), lambda i, j: (0, 0)),
        pl.BlockSpec((None, d, tn), lambda i, j: (layer, 0, j)),
        pl.BlockSpec((1, HEAD_DIM), lambda i, j: (0, 0)),
        pl.BlockSpec((1, HEAD_DIM), lambda i, j: (0, 0)),
    ]
    args = [x, mod, g, w, qn, kn]
    if rope:
        pos = lambda i, j: (i % blocks_per_group, 0)
        in_specs += [pl.BlockSpec((tm, HEAD_DIM), pos), pl.BlockSpec((tm, HEAD_DIM), pos)]
        args += list(rope_tables)
    out_specs = [pl.BlockSpec((tm, tn), lambda i, j: (i, j))]
    out_shape = [jax.ShapeDtypeStruct((m, n), _BF16)]
    if emit_kv:
        out_specs += [pl.BlockSpec((tm, tn), lambda i, j: (i, jnp.clip(j - n_q, 0, n_k - 1))),
                      pl.BlockSpec((tm, tn), lambda i, j: (i, jnp.clip(j - n_q - n_k, 0, n_k - 1)))]
        out_shape += [jax.ShapeDtypeStruct((m, n_k * tn), _F32)] * 2
    return pl.pallas_call(
        functools.partial(_qkv_kernel, n_q=n_q, n_k=n_k, heads_per_tile=heads_per_tile,
                          rope=rope, emit_kv=emit_kv),
        grid=(m // tm, n // tn),
        in_specs=in_specs,
        out_specs=out_specs,
        out_shape=out_shape,
        scratch_shapes=[pltpu.VMEM((tm, d), _BF16)],
        compiler_params=_params(("arbitrary", "arbitrary")),
        name="qkv_rope" if rope else "qkv",
    )(*args)


def _rope_tables(n_tokens):
    t = jnp.arange(n_tokens)
    quarter = HEAD_DIM // 4
    inv = ROPE_THETA ** (-jnp.arange(quarter, dtype=_F32) / quarter)
    ang_r = (t // GRID_W).astype(_F32)[:, None] * inv
    ang_c = (t % GRID_W).astype(_F32)[:, None] * inv
    cr, sr, cc, sc = jnp.cos(ang_r), jnp.sin(ang_r), jnp.cos(ang_c), jnp.sin(ang_c)
    return (jnp.concatenate([cr, cr, cc, cc], axis=1), jnp.concatenate([-sr, sr, -sc, sc], axis=1))


def _softmax_pv(scores, values):
    m = functools.reduce(jnp.maximum, [jnp.max(s, axis=-1, keepdims=True) for s in scores])
    ps = [jnp.exp2(s - m) for s in scores]
    l = functools.reduce(jnp.add, [jnp.sum(p, axis=-1, keepdims=True) for p in ps])
    o = functools.reduce(jnp.add, [_dot(p.astype(_BF16), v) for p, v in zip(ps, values)])
    return o / l


def _ctx_attn_kernel(q_ref, k_ref, v_ref, o_ref, *, group):
    for h in range(N_HEADS):
        kv = h // group
        q = q_ref[:, h * HEAD_DIM:(h + 1) * HEAD_DIM]
        k = k_ref[:, kv * HEAD_DIM:(kv + 1) * HEAD_DIM]
        v = v_ref[:, kv * HEAD_DIM:(kv + 1) * HEAD_DIM]
        o = _softmax_pv([_dot_nt(q, k)], [v])
        o_ref[:, h * HEAD_DIM:(h + 1) * HEAD_DIM] = o.astype(_BF16)


def _context_attention(qkv, n_kv, seq, n_rows_total):
    m = qkv.shape[0]
    kv_dim = n_kv * HEAD_DIM
    k_blk = D_MODEL // kv_dim
    return pl.pallas_call(
        functools.partial(_ctx_attn_kernel, group=N_HEADS // n_kv),
        grid=(m // seq,),
        in_specs=[
            pl.BlockSpec((seq, D_MODEL), lambda b: (b, 0)),
            pl.BlockSpec((seq, kv_dim), lambda b: (b, k_blk)),
            pl.BlockSpec((seq, kv_dim), lambda b: (b, k_blk + 1)),
        ],
        out_specs=pl.BlockSpec((seq, D_MODEL), lambda b: (b, 0)),
        out_shape=jax.ShapeDtypeStruct((n_rows_total, D_MODEL), _BF16),
        compiler_params=_params(("arbitrary",)),
        name="context_attention",
    )(qkv, qkv, qkv)


def _na_table_kernel(rpb_ref, o_ref):
    h = pl.program_id(0)
    c = lax.broadcasted_iota(jnp.int32, (GRID_W, GRID_W), 0)
    kc = lax.broadcasted_iota(jnp.int32, (GRID_W, GRID_W), 1)
    start = jnp.clip(c - WIN_C // 2, 0, GRID_W - WIN_C)
    col_ok = (kc >= start) & (kc < start + WIN_C)
    rel = kc - c + WIN_C - 1
    n_dc = 2 * WIN_C - 1
    masked = jnp.full((GRID_W, GRID_W), NEG_INF, _F32)
    tiles = []
    for d in range(-8, NA_PAIRS - 8 + 1):
        if 0 <= d < 2 * WIN_R - 1:
            t = jnp.zeros((GRID_W, GRID_W), _F32)
            for jj in range(n_dc):
                t = jnp.where(rel == jj, rpb_ref[h, d * n_dc + jj] * LOG2_E, t)
            tiles.append(jnp.where(col_ok, t, NEG_INF))
        else:
            tiles.append(masked)
    for i in range(NA_PAIRS):
        o_ref[0, i] = jnp.concatenate([tiles[i], tiles[i + 1]], axis=1)
        o_ref[0, NA_PAIRS + i] = jnp.concatenate([tiles[i], masked], axis=1)
        o_ref[0, 2 * NA_PAIRS + i] = jnp.concatenate([masked, tiles[i + 1]], axis=1)
    o_ref[0, 3 * NA_PAIRS] = jnp.concatenate([masked, masked], axis=1)


def _na_bias_table(rpb):
    h = rpb.shape[0]
    return pl.pallas_call(
        _na_table_kernel,
        grid=(h,),
        in_specs=[pl.BlockSpec(memory_space=pltpu.SMEM)],
        out_specs=pl.BlockSpec((1, NA_TABLE, GRID_W, 2 * GRID_W), lambda i: (i, 0, 0, 0)),
        out_shape=jax.ShapeDtypeStruct((h, NA_TABLE, GRID_W, 2 * GRID_W), _F32),
        compiler_params=_params(("arbitrary",)),
        name="na_bias_table",
    )(rpb.reshape(h, -1))


def _na_tile_plan(rows):
    nblk = rows // NA_Q_ROWS
    plan = np.zeros((nblk, NA_Q_ROWS * (NA_K_ROWS // 2)), np.int32)
    for blk in range(nblk):
        rs = blk * NA_Q_ROWS
        ks = min(max(rs - WIN_R // 2, 0), rows - NA_K_ROWS)
        for qi in range(NA_Q_ROWS):
            r0 = min(max(rs + qi - WIN_R // 2, 0), rows - WIN_R)
            for p in range(NA_K_ROWS // 2):
                left = r0 <= ks + 2 * p < r0 + WIN_R
                right = r0 <= ks + 2 * p + 1 < r0 + WIN_R
                pair = ks - rs + 2 * WIN_R - 1 - qi + 2 * p
                if left and right:
                    entry = pair
                elif left:
                    entry = NA_PAIRS + pair
                elif right:
                    entry = 2 * NA_PAIRS + pair
                else:
                    entry = 3 * NA_PAIRS
                plan[blk, qi * (NA_K_ROWS // 2) + p] = entry
    return plan


def _na_attn_kernel(plan_ref, alias_ref, q_ref, k_ref, v_ref, kc_ref, vc_ref, tab_ref, o_ref, *, rows):
    del alias_ref
    blk = pl.program_id(2)
    rs = blk * NA_Q_ROWS
    ks = jnp.clip(rs - WIN_R // 2, 0, rows - NA_K_ROWS)
    k0 = pl.multiple_of(ks * GRID_W, 4 * GRID_W)
    n_keys = NA_K_ROWS * GRID_W
    pairs = NA_K_ROWS // 2
    for hh in range(NA_HEADS):
        cols = slice(hh * HEAD_DIM, (hh + 1) * HEAD_DIM)
        q = q_ref[:, cols]
        k = k_ref[pl.ds(k0, n_keys), cols]
        v = v_ref[pl.ds(k0, n_keys), cols]
        bias = jnp.concatenate(
            [jnp.concatenate([tab_ref[hh, plan_ref[blk, qi * pairs + p]] for p in range(pairs)], axis=1)
             for qi in range(NA_Q_ROWS)], axis=0)
        s = _dot_nt(q, k) + bias
        s_ctx = _dot_nt(q, kc_ref[0, :, cols])
        o_ref[:, cols] = _softmax_pv([s, s_ctx], [v, vc_ref[0, :, cols]]).astype(_BF16)


def _neighbourhood_attention(attn, qkv, k_ctx, v_ctx, table, row0):
    b, l, _ = k_ctx.shape
    t = qkv.shape[0] // b
    rows = t // GRID_W
    tq = NA_Q_ROWS * GRID_W
    nblk = t // tq
    w = NA_HEADS * HEAD_DIM
    hp = N_HEADS // NA_HEADS
    out_blk0 = row0 // tq
    return pl.pallas_call(
        functools.partial(_na_attn_kernel, rows=rows),
        grid_spec=pltpu.PrefetchScalarGridSpec(
            num_scalar_prefetch=1,
            grid=(b, hp, nblk),
            in_specs=[
                pl.BlockSpec(memory_space=pl.ANY),
                pl.BlockSpec((tq, w), lambda bi, h, i, plan: (bi * nblk + i, h)),
                pl.BlockSpec((t, w), lambda bi, h, i, plan: (bi, hp + h)),
                pl.BlockSpec((t, w), lambda bi, h, i, plan: (bi, 2 * hp + h)),
                pl.BlockSpec((1, l, w), lambda bi, h, i, plan: (bi, 0, h)),
                pl.BlockSpec((1, l, w), lambda bi, h, i, plan: (bi, 0, h)),
                pl.BlockSpec((NA_HEADS, NA_TABLE, GRID_W, 2 * GRID_W), lambda bi, h, i, plan: (h, 0, 0, 0)),
            ],
            out_specs=pl.BlockSpec((tq, w), lambda bi, h, i, plan: (out_blk0 + bi * nblk + i, h)),
        ),
        out_shape=jax.ShapeDtypeStruct(attn.shape, attn.dtype),
        input_output_aliases={1: 0},
        compiler_params=_params(("arbitrary", "arbitrary", "arbitrary")),
        name="neighbourhood_attention",
    )(jnp.asarray(_na_tile_plan(rows)), attn, qkv, qkv, qkv, k_ctx, v_ctx, table)


def _gqa_attn_kernel(alias_ref, q_ref, k_ref, v_ref, kc_ref, vc_ref, o_ref, *, group, chunk):
    del alias_ref
    tq = q_ref.shape[0]
    q = jnp.concatenate([q_ref[:, g * HEAD_DIM:(g + 1) * HEAD_DIM] for g in range(group)], axis=0)
    t = k_ref.shape[0]
    m = l = acc = None
    for c in range(t // chunk + 1):
        if c < t // chunk:
            k = k_ref[c * chunk:(c + 1) * chunk, :]
            v = v_ref[c * chunk:(c + 1) * chunk, :]
        else:
            k, v = kc_ref[0], vc_ref[0]
        s = _dot_nt(q, k)
        m_c = jnp.max(s, axis=-1, keepdims=True)
        if m is None:
            m = m_c
            p = jnp.exp2(s - m)
            l = jnp.sum(p, axis=-1, keepdims=True)
            acc = _dot(p.astype(_BF16), v)
        else:
            m_new = jnp.maximum(m, m_c)
            alpha = jnp.exp2(m - m_new)
            p = jnp.exp2(s - m_new)
            l = alpha * l + jnp.sum(p, axis=-1, keepdims=True)
            acc = alpha * acc + _dot(p.astype(_BF16), v)
            m = m_new
    o = acc / l
    for g in range(group):
        o_ref[:, g * HEAD_DIM:(g + 1) * HEAD_DIM] = o[g * tq:(g + 1) * tq].astype(_BF16)


def _gqa_attention(attn, qkv, k_ctx, v_ctx, row0):
    b, l, kv_dim = k_ctx.shape
    n_kv = kv_dim // HEAD_DIM
    group = N_HEADS // n_kv
    t = qkv.shape[0] // b
    tq = 256
    nblk = t // tq
    out_blk0 = row0 // tq
    return pl.pallas_call(
        functools.partial(_gqa_attn_kernel, group=group, chunk=1024),
        grid=(b, n_kv, nblk),
        in_specs=[
            pl.BlockSpec(memory_space=pl.ANY),
            pl.BlockSpec((tq, group * HEAD_DIM), lambda bi, h, i: (bi * nblk + i, h)),
            pl.BlockSpec((t, HEAD_DIM), lambda bi, h, i: (bi, N_HEADS + h)),
            pl.BlockSpec((t, HEAD_DIM), lambda bi, h, i: (bi, N_HEADS + n_kv + h)),
            pl.BlockSpec((1, l, HEAD_DIM), lambda bi, h, i: (bi, 0, h)),
            pl.BlockSpec((1, l, HEAD_DIM), lambda bi, h, i: (bi, 0, h)),
        ],
        out_specs=pl.BlockSpec((tq, group * HEAD_DIM), lambda bi, h, i: (out_blk0 + bi * nblk + i, h)),
        out_shape=jax.ShapeDtypeStruct(attn.shape, attn.dtype),
        input_output_aliases={0: 0},
        compiler_params=_params(("arbitrary", "arbitrary", "arbitrary")),
        name="gqa_attention",
    )(attn, qkv, qkv, qkv, k_ctx, v_ctx)


def _pack_bf16_halves(h):
    half = h.shape[1] // 2

    def rounded(v):
        b = lax.bitcast_convert_type(v, jnp.int32)
        return b + 0x7FFF + ((b >> 16) & 1)

    lo, hi = rounded(h[:, :half]), rounded(h[:, half:])
    return (hi & -65536) | ((lo >> 16) & 0xFFFF)


def _unpack_bf16_halves(p):
    lo = lax.bitcast_convert_type(p << 16, _F32)
    hi = lax.bitcast_convert_type(p & -65536, _F32)
    return jnp.concatenate([lo, hi], axis=1).astype(_BF16)


def _route(logits, lane):
    def masked_softmax(mask):
        z = jnp.where(mask, logits, -jnp.inf)
        e = jnp.exp(z - jnp.max(z, axis=-1, keepdims=True))
        return e / jnp.sum(e, axis=-1, keepdims=True)

    def top1(p, mask):
        best = jnp.max(jnp.where(mask, p, -1.0), axis=-1, keepdims=True)
        idx = jnp.min(jnp.where(mask & (p == best), lane, float(LANES)), axis=-1, keepdims=True)
        return best, idx

    g_mask = lane < N_GROUPS
    g_p, g_idx = top1(masked_softmax(g_mask), g_mask)
    lo = N_GROUPS + g_idx * EXPERTS_PER_GROUP
    e_mask = (lane >= lo) & (lane < lo + EXPERTS_PER_GROUP)
    e_prob = masked_softmax(e_mask)
    p1, i1 = top1(e_prob, e_mask)
    p2, i2 = top1(e_prob, e_mask & (lane != i1))
    return i1 - N_GROUPS, i2 - N_GROUPS, g_p * p1 / (p1 + p2), g_p * p2 / (p1 + p2)


def _proj_router_kernel(a_ref, x_ref, mod_ref, g_ref, wo_ref, wr_ref, br_ref,
                        x1_ref, h_ref, ri_ref, rf_ref, cnt_ref, carry):
    i = pl.program_id(0)
    tm = a_ref.shape[0]

    @pl.when(i == 0)
    def _():
        carry[...] = jnp.zeros_like(carry)

    o = _dot(a_ref[...], wo_ref[...])
    x1 = x_ref[...] + _mod_slices(mod_ref, 2) * o
    x1_ref[...] = x1
    h = _norm_modulate(x1, g_ref[...], _mod_slices(mod_ref, 3), _mod_slices(mod_ref, 4))
    h_ref[...] = _pack_bf16_halves(h)

    logits = _dot(h.astype(_BF16), wr_ref[...]) + br_ref[...]
    lane = lax.broadcasted_iota(jnp.int32, logits.shape, 1).astype(_F32)
    e1, e2, gate1, gate2 = _route(logits, lane)

    hit1, hit2 = lane == e1, lane == e2
    onehot = jnp.where(hit1 | hit2, 1.0, 0.0)
    row = lax.broadcasted_iota(jnp.int32, (tm, tm), 0)
    col = lax.broadcasted_iota(jnp.int32, (tm, tm), 1)
    lower = jnp.where(row > col, 1.0, 0.0).astype(_BF16)
    before = _dot(lower, onehot.astype(_BF16)) + carry[...]
    rank1 = jnp.sum(jnp.where(hit1, before, 0.0), axis=-1, keepdims=True)
    rank2 = jnp.sum(jnp.where(hit2, before, 0.0), axis=-1, keepdims=True)
    carry[...] = carry[...] + jnp.sum(onehot, axis=0, keepdims=True)

    record = jnp.where(lane == 0, e1, jnp.where(lane == 1, e2,
                       jnp.where(lane == 2, rank1, jnp.where(lane == 3, rank2, 0.0))))
    ri_ref[...] = record.T[:MOD_ROWS].astype(jnp.int32)
    rf_ref[...] = jnp.where(lane == 0, gate1, jnp.where(lane == 1, gate2, 0.0))
    cnt_ref[...] = jnp.broadcast_to(carry[...], cnt_ref.shape)


def _proj_router(attn, x, mod, g, w_o, layer, w_r, b_r):
    m, d = x.shape
    tm = 512
    blocks_per_group = GROUP_ROWS // tm
    row = lambda i: (i, 0)
    const = lambda i: (0, 0)
    return pl.pallas_call(
        _proj_router_kernel,
        grid=(m // tm,),
        in_specs=[
            pl.BlockSpec((tm, d), row),
            pl.BlockSpec((tm, d), row),
            pl.BlockSpec((1, 1, 6 * d), lambda i: (i // blocks_per_group, 0, 0)),
            pl.BlockSpec((1, d), const),
            pl.BlockSpec((None, d, d), lambda i: (layer, 0, 0)),
            pl.BlockSpec((d, LANES), const),
            pl.BlockSpec((1, LANES), const),
        ],
        out_specs=[
            pl.BlockSpec((tm, d), row),
            pl.BlockSpec((tm, d // 2), row),
            pl.BlockSpec((MOD_ROWS, tm), lambda i: (0, i)),
            pl.BlockSpec((tm, LANES), row),
            pl.BlockSpec((MOD_ROWS, LANES), const),
        ],
        out_shape=[
            jax.ShapeDtypeStruct((m, d), _F32),
            jax.ShapeDtypeStruct((m, d // 2), jnp.int32),
            jax.ShapeDtypeStruct((MOD_ROWS, m), jnp.int32),
            jax.ShapeDtypeStruct((m, LANES), _F32),
            jax.ShapeDtypeStruct((MOD_ROWS, LANES), _F32),
        ],
        scratch_shapes=[pltpu.VMEM((1, LANES), _F32)],
        compiler_params=_params(("arbitrary",)),
        name="proj_router",
    )(attn, x, mod, g, w_o, w_r, b_r)


def _record_copy(dest_hbm, idx, isem, blk, slot, tm):
    return pltpu.make_async_copy(dest_hbm.at[:, pl.ds(blk * tm, tm)], idx.at[slot], isem.at[slot])


def _start_row_copies(idx, slot, tm, make_copy):
    for chunk in range(tm // LANES):
        def body(j, carry, chunk=chunk):
            base = chunk * LANES + pl.multiple_of(j * 8, 8)
            for u in range(8):
                for k in range(2):
                    make_copy(base + u, k, idx[slot, k, base + u]).start(priority=k)
            return carry
        lax.fori_loop(0, LANES // 8, body, 0)


def _dispatch_kernel(dest_hbm, h_ref, xs_hbm, idx, isem, sem):
    i = pl.program_id(0)
    n = pl.num_programs(0)
    tm = h_ref.shape[0]

    @pl.when(i == 0)
    def _():
        _record_copy(dest_hbm, idx, isem, 0, 0, tm).start()

    @pl.when(i + 1 < n)
    def _():
        _record_copy(dest_hbm, idx, isem, i + 1, (i + 1) % 2, tm).start()

    _record_copy(dest_hbm, idx, isem, i, i % 2, tm).wait()
    for slot in range(2):
        @pl.when(i % 2 == slot)
        def _(slot=slot):
            _start_row_copies(idx, slot, tm, lambda r, k, d: pltpu.make_async_copy(
                h_ref.at[pl.ds(r, 1)], xs_hbm.at[pl.ds(d, 1)], sem))

    for _ in range(2):
        pltpu.make_async_copy(h_ref, xs_hbm.at[pl.ds(0, tm)], sem).wait()


def _dispatch(h2, dest, p_rows):
    m, d = h2.shape
    tm = 256
    return pl.pallas_call(
        _dispatch_kernel,
        grid=(m // tm,),
        in_specs=[
            pl.BlockSpec(memory_space=pl.ANY),
            pl.BlockSpec((tm, d), lambda i: (i, 0)),
        ],
        out_specs=pl.BlockSpec(memory_space=pl.ANY),
        out_shape=jax.ShapeDtypeStruct((p_rows, d), h2.dtype),
        scratch_shapes=[
            pltpu.SMEM((2, MOD_ROWS, tm), jnp.int32),
            pltpu.SemaphoreType.DMA((2,)),
            pltpu.SemaphoreType.DMA,
        ],
        compiler_params=_params(("arbitrary",)),
        name="dispatch",
    )(dest, h2)


def _expert_kernel(be_ref, valid_ref, next_ref, x_ref, wg_hbm, wu_hbm, wd_hbm, o_ref,
                   stage_g, stage_u, stage_d, wg, wu, wd, sem, *, layer):
    b = pl.program_id(0)
    valid = valid_ref[b]
    expert = be_ref[b]

    def fetch(e):
        return [pltpu.make_async_copy(src.at[layer, e], dst, sem)
                for src, dst in ((wg_hbm, stage_g), (wu_hbm, stage_u), (wd_hbm, stage_d))]

    @pl.when(b == 0)
    def _():
        for copy in fetch(expert):
            copy.start()

    first = (b == 0) | (expert != be_ref[jnp.maximum(b - 1, 0)])

    @pl.when(first & (valid > 0))
    def _():
        for copy in fetch(expert):
            copy.wait()
        rows = 256
        for stage, dst in ((stage_g, wg), (stage_u, wu), (stage_d, wd)):
            for r in range(0, stage.shape[0], rows):
                dst[r:r + rows, :] = stage[r:r + rows, :].astype(_BF16)

        @pl.when(next_ref[b] >= 0)
        def _():
            for copy in fetch(next_ref[b]):
                copy.start()

    @pl.when(valid > 0)
    def _():
        row = lax.broadcasted_iota(jnp.int32, x_ref.shape, 0)
        x = _unpack_bf16_halves(jnp.where(row < valid, x_ref[...], 0))
        hidden = _silu(_dot(x, wg[...])) * _dot(x, wu[...])
        o_ref[...] = _dot(hidden.astype(_BF16), wd[...])

    @pl.when(valid <= 0)
    def _():
        o_ref[...] = jnp.zeros_like(o_ref)


def _experts(xs, block_e, block_valid, block_next, w_gate, w_up, w_down, layer):
    p = xs.shape[0]
    d, de = w_gate.shape[-2:]
    nb = p // MOE_ROWS
    return pl.pallas_call(
        functools.partial(_expert_kernel, layer=layer),
        grid_spec=pltpu.PrefetchScalarGridSpec(
            num_scalar_prefetch=3,
            grid=(nb,),
            in_specs=[
                pl.BlockSpec((MOE_ROWS, d // 2), lambda b, be, nv, nx: (b, 0)),
                pl.BlockSpec(memory_space=pl.ANY),
                pl.BlockSpec(memory_space=pl.ANY),
                pl.BlockSpec(memory_space=pl.ANY),
            ],
            out_specs=pl.BlockSpec((MOE_ROWS, d), lambda b, be, nv, nx: (b, 0)),
            scratch_shapes=[
                pltpu.VMEM((d, de), _F32), pltpu.VMEM((d, de), _F32), pltpu.VMEM((de, d), _F32),
                pltpu.VMEM((d, de), _BF16), pltpu.VMEM((d, de), _BF16), pltpu.VMEM((de, d), _BF16),
                pltpu.SemaphoreType.DMA,
            ],
        ),
        out_shape=jax.ShapeDtypeStruct((p, d), _F32),
        compiler_params=_params(("arbitrary",)),
        name="experts",
    )(block_e, block_valid, block_next, xs, w_gate, w_up, w_down)


def _combine_kernel(dest_hbm, yb_hbm, x_ref, mod_ref, rf_ref, o_ref, idx, buf, isem, sem, *, blk0):
    i = pl.program_id(0)
    n = pl.num_programs(0)
    tm = x_ref.shape[0]

    def records(blk, slot):
        return _record_copy(dest_hbm, idx, isem, blk0 + blk, slot, tm)

    def fetch_rows(slot):
        _start_row_copies(idx, slot, tm, lambda r, k, d: pltpu.make_async_copy(
            yb_hbm.at[pl.ds(d, 1)], buf.at[slot, k, pl.ds(r, 1)], sem.at[slot]))

    @pl.when(i == 0)
    def _():
        records(0, 0).start()

        @pl.when(n > 1)
        def _():
            records(1, 1).start()

        records(0, 0).wait()
        fetch_rows(0)

    @pl.when(i + 2 < n)
    def _():
        records(i + 2, i % 2).start()

    for slot in range(2):
        @pl.when((i + 1 < n) & ((i + 1) % 2 == slot))
        def _(slot=slot):
            records(i + 1, slot).wait()
            fetch_rows(slot)

    slot = i % 2
    for k in range(2):
        pltpu.make_async_copy(yb_hbm.at[pl.ds(0, tm)], buf.at[slot, k], sem.at[slot]).wait()
    gate = rf_ref[...]
    y = buf[slot, 0] * gate[:, 0:1] + buf[slot, 1] * gate[:, 1:2]
    o_ref[...] = x_ref[...] + _mod_slices(mod_ref, 5) * y


def _combine(x1, mod, yb, dest, rf, row0, m):
    d = x1.shape[1]
    tm = 256
    blocks_per_group = GROUP_ROWS // tm
    blk0 = row0 // tm
    row = lambda i: (blk0 + i, 0)
    return pl.pallas_call(
        functools.partial(_combine_kernel, blk0=blk0),
        grid=(m // tm,),
        in_specs=[
            pl.BlockSpec(memory_space=pl.ANY),
            pl.BlockSpec(memory_space=pl.ANY),
            pl.BlockSpec((tm, d), row),
            pl.BlockSpec((1, 1, 6 * d), lambda i: ((blk0 + i) // blocks_per_group, 0, 0)),
            pl.BlockSpec((tm, LANES), row),
        ],
        out_specs=pl.BlockSpec((tm, d), lambda i: (i, 0)),
        out_shape=jax.ShapeDtypeStruct((m, d), _F32),
        scratch_shapes=[
            pltpu.SMEM((2, MOD_ROWS, tm), jnp.int32),
            pltpu.VMEM((2, 2, tm, d), _F32),
            pltpu.SemaphoreType.DMA((2,)),
            pltpu.SemaphoreType.DMA((2,)),
        ],
        compiler_params=_params(("arbitrary",)),
        name="combine",
    )(dest, yb, x1, mod, rf)


def _moe(x1, h2, ri, rf, counts, mod, w_gate, w_up, w_down, layer, splits):
    m = x1.shape[0]
    p_rows = (2 * m + N_EXPERTS * (MOE_ROWS - 1) + MOE_ROWS - 1) // MOE_ROWS * MOE_ROWS
    nb = p_rows // MOE_ROWS
    cnt = counts[0, :N_EXPERTS].astype(jnp.int32)
    blocks_e = (cnt + MOE_ROWS - 1) // MOE_ROWS
    end_blk = jnp.cumsum(blocks_e)
    start_blk = end_blk - blocks_e
    starts = (start_blk * MOE_ROWS).astype(jnp.int32)
    blk = jnp.arange(nb, dtype=jnp.int32)
    block_e = jnp.minimum(jnp.sum(end_blk[None, :] <= blk[:, None], axis=1), N_EXPERTS - 1).astype(jnp.int32)
    mine = block_e[:, None] == jnp.arange(N_EXPERTS, dtype=jnp.int32)[None, :]
    cnt_b = jnp.sum(jnp.where(mine, cnt[None, :], 0), axis=1)
    start_b = jnp.sum(jnp.where(mine, start_blk[None, :], 0), axis=1)
    block_valid = jnp.clip(cnt_b - (blk - start_b) * MOE_ROWS, 0, MOE_ROWS).astype(jnp.int32)
    ids = jnp.arange(N_EXPERTS, dtype=jnp.int32)
    later = (ids[None, :] > ids[:, None]) & (cnt[None, :] > 0)
    next_e = jnp.min(jnp.where(later, ids[None, :], N_EXPERTS), axis=1)
    next_e = jnp.where(next_e == N_EXPERTS, -1, next_e)
    block_next = jnp.sum(jnp.where(mine, next_e[None, :], 0), axis=1).astype(jnp.int32)
    experts = jnp.arange(N_EXPERTS, dtype=jnp.int32)[None, :, None]
    dest = jnp.sum(jnp.where(ri[0:2, None, :] == experts, starts[None, :, None], 0), axis=1) + ri[2:4]
    dest = jnp.concatenate([dest, jnp.zeros((MOD_ROWS - 2, m), jnp.int32)], axis=0)
    xs = _dispatch(h2, dest, p_rows)
    yb = _experts(xs, block_e, block_valid, block_next, w_gate, w_up, w_down, layer)
    return [_combine(x1, mod, yb, dest, rf, row0, rows) for row0, rows in splits]


def kernel(x_prompt, x_sample, c, cache_k_a, cache_v_a, cache_k_b, cache_v_b, c_ctx, w_mod, b_mod, norm1_g, norm2_g, w_qkv_a, w_o_a, q_norm_a, k_norm_a, rpb_a, w_qkv_b, w_o_b, q_norm_b, k_norm_b, w_router_group, b_router_group, w_router_expert, b_router_expert, w_gate, w_up, w_down):
    batch, seq, d = x_prompt.shape
    dec_batch, dec_seq, _ = x_sample.shape
    depth = w_mod.shape[0]
    mp = batch * seq
    ms = dec_batch * dec_seq
    assert mp == GROUP_ROWS and dec_seq == GROUP_ROWS and d == D_MODEL
    x = jnp.concatenate([x_prompt.reshape(mp, d), x_sample.reshape(ms, d)], axis=0)

    cond = jnp.concatenate([c_ctx[None], c, jnp.zeros((MOD_ROWS - 1 - dec_batch, d), _F32)], axis=0)
    mod_all = _modulation(cond, w_mod, b_mod)
    rope_tables = _rope_tables(dec_seq)

    w_router = jnp.concatenate([w_router_group, w_router_expert], axis=-1)
    n_route = w_router.shape[-1]
    w_router = jnp.pad(w_router, ((0, 0), (0, 0), (0, LANES - n_route))).astype(_BF16)
    b_router = jnp.pad(jnp.concatenate([b_router_group, b_router_expert], axis=-1),
                       ((0, 0), (0, LANES - n_route)))

    w_qkv_a, w_o_a, w_qkv_b, w_o_b = (_to_bf16(w) for w in (w_qkv_a, w_o_a, w_qkv_b, w_o_b))

    new_kv = [[], [], [], []]
    for i in range(depth):
        j = i // 2
        mod = mod_all[i].reshape(MOD_ROWS, 1, 6 * d)
        g1 = norm1_g[i].reshape(1, d)
        g2 = norm2_g[i].reshape(1, d)
        if i % 2 == 0:
            n_kv, w_qkv, w_o, qn, kn = N_HEADS, w_qkv_a, w_o_a, q_norm_a[j], k_norm_a[j]
            cache_k, cache_v, rope = cache_k_a[:, j], cache_v_a[:, j], None
        else:
            n_kv, w_qkv, w_o, qn, kn = N_KV_B, w_qkv_b, w_o_b, q_norm_b[j], k_norm_b[j]
            cache_k, cache_v, rope = cache_k_b[:, j], cache_v_b[:, j], rope_tables
        kv_dim = n_kv * HEAD_DIM
        qn, kn = qn.reshape(1, HEAD_DIM), kn.reshape(1, HEAD_DIM)
        qkv_p, k_p, v_p = _qkv(x, mod, g1, w_qkv, j, qn, kn, n_kv, row0=0, m=mp, rope_tables=None, emit_kv=True)
        (qkv_s,) = _qkv(x, mod, g1, w_qkv, j, qn, kn, n_kv, row0=mp, m=ms, rope_tables=rope, emit_kv=False)
        new_kv[2 * (i % 2)].append(k_p.reshape(batch, seq, n_kv, HEAD_DIM))
        new_kv[2 * (i % 2) + 1].append(v_p.reshape(batch, seq, n_kv, HEAD_DIM))

        attn = _context_attention(qkv_p, n_kv, seq, mp + ms)
        k_ctx = cache_k.reshape(dec_batch, -1, kv_dim).astype(_BF16)
        v_ctx = cache_v.reshape(dec_batch, -1, kv_dim).astype(_BF16)
        if i % 2 == 0:
            attn = _neighbourhood_attention(attn, qkv_s, k_ctx, v_ctx, _na_bias_table(rpb_a[j]), mp)
        else:
            attn = _gqa_attention(attn, qkv_s, k_ctx, v_ctx, mp)

        x1, h2, ri, rf, counts = _proj_router(attn, x, mod, g2, w_o, j, w_router[i],
                                              b_router[i].reshape(1, LANES))
        last = i == depth - 1
        outs = _moe(x1, h2, ri, rf, counts, mod, w_gate, w_up, w_down, i,
                    [(0, mp), (mp, ms)] if last else [(0, mp + ms)])
        x = outs[0]

    return (outs[0].reshape(batch, seq, d), outs[1].reshape(dec_batch, dec_seq, d),
            jnp.stack(new_kv[0], axis=1), jnp.stack(new_kv[1], axis=1),
            jnp.stack(new_kv[2], axis=1), jnp.stack(new_kv[3], axis=1))
```

```python
import functools
import math

import numpy as np

import jax
import jax.numpy as jnp
from jax import lax
from jax.experimental import pallas as pl
from jax.experimental.pallas import tpu as pltpu

D_MODEL = 2048
HEAD_DIM = 128
N_HEADS = D_MODEL // HEAD_DIM
N_KV_B = 4
GRID_W = 64
WIN_R = 8
WIN_C = 16
ROPE_THETA = 10000.0
N_GROUPS = 4
EXPERTS_PER_GROUP = 8
N_EXPERTS = N_GROUPS * EXPERTS_PER_GROUP
D_EXPERT = 768
EPS = 1e-6
NEG_INF = -1e30

LANES = 128
MOD_ROWS = 8
GROUP_ROWS = 4096
MOE_ROWS = 256
NA_Q_ROWS = 8
NA_K_ROWS = 16
NA_PAIRS = 30
NA_TABLE = 3 * NA_PAIRS + 1
NA_HEADS = 2
ROW_CHUNK = 64
VMEM_LIMIT = 56 * 1024 * 1024
LOG2_E = math.log2(math.e)
Q_SCALE = HEAD_DIM ** -0.5 * LOG2_E

_BF16 = jnp.bfloat16
_F32 = jnp.float32


def _params(sem, vmem=VMEM_LIMIT):
    return pltpu.CompilerParams(dimension_semantics=sem, vmem_limit_bytes=vmem)


def _dot(a, b):
    return jnp.dot(a, b, preferred_element_type=_F32)


def _dot_nt(a, b):
    return lax.dot_general(a, b, (((1,), (1,)), ((), ())), preferred_element_type=_F32)


def _silu(x):
    return x / (1.0 + jnp.exp(-x))


def _cast_kernel(w_ref, o_ref):
    o_ref[...] = w_ref[...].astype(o_ref.dtype)


def _to_bf16(w):
    cols = w.shape[-1]
    w2 = w.reshape(-1, cols)
    rows = w2.shape[0]
    rb = max(8, min(rows, (4 * 1024 * 1024) // (4 * cols)) // 8 * 8)
    while rows % rb:
        rb -= 8
    out = pl.pallas_call(
        _cast_kernel,
        grid=(rows // rb,),
        in_specs=[pl.BlockSpec((rb, cols), lambda i: (i, 0))],
        out_specs=pl.BlockSpec((rb, cols), lambda i: (i, 0)),
        out_shape=jax.ShapeDtypeStruct((rows, cols), _BF16),
        compiler_params=_params(("arbitrary",)),
        name="to_bf16",
    )(w2)
    return out.reshape(w.shape)


def _mod_kernel(cond_ref, w_ref, b_ref, o_ref):
    s = _silu(cond_ref[...]).astype(_BF16)
    o_ref[...] = _dot(s, w_ref[...].astype(_BF16)) + b_ref[...]


def _modulation(cond, w_mod, b_mod):
    depth, d, n = w_mod.shape
    tn = 1024
    return pl.pallas_call(
        _mod_kernel,
        grid=(depth, n // tn),
        in_specs=[
            pl.BlockSpec((MOD_ROWS, d), lambda l, j: (0, 0)),
            pl.BlockSpec((None, d, tn), lambda l, j: (l, 0, j)),
            pl.BlockSpec((None, 1, tn), lambda l, j: (l, 0, j)),
        ],
        out_specs=pl.BlockSpec((None, MOD_ROWS, tn), lambda l, j: (l, 0, j)),
        out_shape=jax.ShapeDtypeStruct((depth, MOD_ROWS, n), _F32),
        compiler_params=_params(("arbitrary", "arbitrary")),
        name="modulation",
    )(cond, w_mod, b_mod.reshape(depth, 1, n))


def _mod_slices(mod_ref, which):
    return mod_ref[0, :, which * D_MODEL:(which + 1) * D_MODEL]


def _norm_modulate(x, g, shift, scale):
    y = x * lax.rsqrt(jnp.mean(x * x, axis=-1, keepdims=True) + EPS)
    return (y * g) * (1.0 + scale) + shift


def _head_norm(a, g):
    return a * lax.rsqrt(jnp.mean(a * a, axis=-1, keepdims=True) + EPS) * g


def _rope_partner_matrix():
    j = np.arange(HEAD_DIM)
    partner = np.where((j % 64) < 32, j + 32, j - 32)
    p = np.zeros((HEAD_DIM, HEAD_DIM), np.float32)
    p[partner, j] = 1.0
    return jnp.asarray(p, _BF16)


def _rope(a, cos, sin, partner_ref):
    return a * cos + _dot(a.astype(_BF16), partner_ref[...]) * sin


def _qkv_kernel(*refs, n_q, n_k, heads_per_tile, rope, emit_kv):
    x_ref, mod_ref, g_ref, w_ref, qn_ref, kn_ref = refs[:6]
    refs = refs[6:]
    if rope:
        cos_ref, sin_ref, partner_ref = refs[:3]
        refs = refs[3:]
    o_ref = refs[0]
    k32_ref, v32_ref = refs[1:3] if emit_kv else (None, None)
    h_scr, acc_scr = refs[-2:]
    j = pl.program_id(1)
    tm = x_ref.shape[0]

    @pl.when(j == 0)
    def _():
        def body(c, carry):
            rows = pl.ds(pl.multiple_of(c * ROW_CHUNK, ROW_CHUNK), ROW_CHUNK)
            h = _norm_modulate(x_ref[rows, :], g_ref[...], _mod_slices(mod_ref, 0), _mod_slices(mod_ref, 1))
            h_scr[rows, :] = h.astype(_BF16)
            return carry
        lax.fori_loop(0, tm // ROW_CHUNK, body, 0, unroll=2)

    acc_scr[...] = _dot(h_scr[...], w_ref[...])

    def finish(gain_ref, scale, f32_ref, plain=False):
        def body(c, carry):
            rows = pl.ds(pl.multiple_of(c * ROW_CHUNK, ROW_CHUNK), ROW_CHUNK)
            if plain:
                y = acc_scr[rows, :]
            else:
                gain = gain_ref[...] * scale
                outs = []
                for hh in range(heads_per_tile):
                    a = _head_norm(acc_scr[rows, hh * HEAD_DIM:(hh + 1) * HEAD_DIM], gain)
                    if rope:
                        a = _rope(a, cos_ref[rows, :], sin_ref[rows, :], partner_ref)
                    outs.append(a)
                y = jnp.concatenate(outs, axis=1)
            o_ref[rows, :] = y.astype(_BF16)
            if f32_ref is not None:
                f32_ref[rows, :] = y
            return carry
        lax.fori_loop(0, tm // ROW_CHUNK, body, 0, unroll=4)

    @pl.when(j < n_q)
    def _():
        finish(qn_ref, Q_SCALE, None)

    @pl.when((j >= n_q) & (j < n_q + n_k))
    def _():
        finish(kn_ref, 1.0, k32_ref)

    @pl.when(j >= n_q + n_k)
    def _():
        finish(None, 1.0, v32_ref, plain=True)


def _qkv(x, mod, g, w, layer, qn, kn, n_kv, *, row0, m, rope_tables, emit_kv):
    d = x.shape[1]
    n = w.shape[-1]
    tm, tn = 1024, 512
    heads_per_tile = tn // HEAD_DIM
    n_q = D_MODEL // tn
    n_k = n_kv * HEAD_DIM // tn
    rope = rope_tables is not None
    blocks_per_group = GROUP_ROWS // tm
    blk0 = row0 // tm

    def group(i):
        return (blk0 + i) // blocks_per_group

    in_specs = [
        pl.BlockSpec((tm, d), lambda i, j: (blk0 + i, 0)),
        pl.BlockSpec((1, 1, 6 * d), lambda i, j: (group(i), 0, 0)),
        pl.BlockSpec((1, d), lambda i, j: (0, 0)),
        pl.BlockSpec((None, d, tn), lambda i, j: (layer, 0, j)),
        pl.BlockSpec((1, HEAD_DIM), lambda i, j: (0, 0)),
        pl.BlockSpec((1, HEAD_DIM), lambda i, j: (0, 0)),
    ]
    args = [x, mod, g, w, qn, kn]
    if rope:
        pos = lambda i, j: (i % blocks_per_group, 0)
        in_specs += [pl.BlockSpec((tm, HEAD_DIM), pos), pl.BlockSpec((tm, HEAD_DIM), pos),
                     pl.BlockSpec((HEAD_DIM, HEAD_DIM), lambda i, j: (0, 0))]
        args += list(rope_tables) + [_rope_partner_matrix()]
    out_specs = [pl.BlockSpec((tm, tn), lambda i, j: (i, j))]
    out_shape = [jax.ShapeDtypeStruct((m, n), _BF16)]
    if emit_kv:
        out_specs += [pl.BlockSpec((tm, tn), lambda i, j: (i, jnp.clip(j - n_q, 0, n_k - 1))),
                      pl.BlockSpec((tm, tn), lambda i, j: (i, jnp.clip(j - n_q - n_k, 0, n_k - 1)))]
        out_shape += [jax.ShapeDtypeStruct((m, n_k * tn), _F32)] * 2
    return pl.pallas_call(
        functools.partial(_qkv_kernel, n_q=n_q, n_k=n_k, heads_per_tile=heads_per_tile,
                          rope=rope, emit_kv=emit_kv),
        grid=(m // tm, n // tn),
        in_specs=in_specs,
        out_specs=out_specs,
        out_shape=out_shape,
        scratch_shapes=[pltpu.VMEM((tm, d), _BF16), pltpu.VMEM((tm, tn), _F32)],
        compiler_params=_params(("arbitrary", "arbitrary")),
        name="qkv_rope" if rope else "qkv",
    )(*args)


def _rope_tables(n_tokens):
    t = jnp.arange(n_tokens)
    quarter = HEAD_DIM // 4
    inv = ROPE_THETA ** (-jnp.arange(quarter, dtype=_F32) / quarter)
    ang_r = (t // GRID_W).astype(_F32)[:, None] * inv
    ang_c = (t % GRID_W).astype(_F32)[:, None] * inv
    cr, sr, cc, sc = jnp.cos(ang_r), jnp.sin(ang_r), jnp.cos(ang_c), jnp.sin(ang_c)
    return (jnp.concatenate([cr, cr, cc, cc], axis=1), jnp.concatenate([-sr, sr, -sc, sc], axis=1))


def _softmax_pv(scores, values):
    m = functools.reduce(jnp.maximum, [jnp.max(s, axis=-1, keepdims=True) for s in scores])
    ps = [jnp.exp2(s - m) for s in scores]
    l = functools.reduce(jnp.add, [jnp.sum(p, axis=-1, keepdims=True) for p in ps])
    o = functools.reduce(jnp.add, [_dot(p.astype(_BF16), v) for p, v in zip(ps, values)])
    return o / l


def _ctx_attn_kernel(q_ref, k_ref, v_ref, o_ref, *, group):
    for h in range(N_HEADS):
        kv = h // group
        q = q_ref[:, h * HEAD_DIM:(h + 1) * HEAD_DIM]
        k = k_ref[:, kv * HEAD_DIM:(kv + 1) * HEAD_DIM]
        v = v_ref[:, kv * HEAD_DIM:(kv + 1) * HEAD_DIM]
        o = _softmax_pv([_dot_nt(q, k)], [v])
        o_ref[:, h * HEAD_DIM:(h + 1) * HEAD_DIM] = o.astype(_BF16)


def _context_attention(qkv, n_kv, seq, n_rows_total):
    m = qkv.shape[0]
    kv_dim = n_kv * HEAD_DIM
    k_blk = D_MODEL // kv_dim
    return pl.pallas_call(
        functools.partial(_ctx_attn_kernel, group=N_HEADS // n_kv),
        grid=(m // seq,),
        in_specs=[
            pl.BlockSpec((seq, D_MODEL), lambda b: (b, 0)),
            pl.BlockSpec((seq, kv_dim), lambda b: (b, k_blk)),
            pl.BlockSpec((seq, kv_dim), lambda b: (b, k_blk + 1)),
        ],
        out_specs=pl.BlockSpec((seq, D_MODEL), lambda b: (b, 0)),
        out_shape=jax.ShapeDtypeStruct((n_rows_total, D_MODEL), _BF16),
        compiler_params=_params(("arbitrary",)),
        name="context_attention",
    )(qkv, qkv, qkv)


def _na_table_kernel(rpb_ref, o_ref):
    h = pl.program_id(0)
    c = lax.broadcasted_iota(jnp.int32, (GRID_W, GRID_W), 0)
    kc = lax.broadcasted_iota(jnp.int32, (GRID_W, GRID_W), 1)
    start = jnp.clip(c - WIN_C // 2, 0, GRID_W - WIN_C)
    col_ok = (kc >= start) & (kc < start + WIN_C)
    rel = kc - c + WIN_C - 1
    n_dc = 2 * WIN_C - 1
    masked = jnp.full((GRID_W, GRID_W), NEG_INF, _F32)
    tiles = []
    for d in range(-8, NA_PAIRS - 8 + 1):
        if 0 <= d < 2 * WIN_R - 1:
            t = jnp.zeros((GRID_W, GRID_W), _F32)
            for jj in range(n_dc):
                t = jnp.where(rel == jj, rpb_ref[h, d * n_dc + jj] * LOG2_E, t)
            tiles.append(jnp.where(col_ok, t, NEG_INF))
        else:
            tiles.append(masked)
    for i in range(NA_PAIRS):
        o_ref[0, i] = jnp.concatenate([tiles[i], tiles[i + 1]], axis=1)
        o_ref[0, NA_PAIRS + i] = jnp.concatenate([tiles[i], masked], axis=1)
        o_ref[0, 2 * NA_PAIRS + i] = jnp.concatenate([masked, tiles[i + 1]], axis=1)
    o_ref[0, 3 * NA_PAIRS] = jnp.concatenate([masked, masked], axis=1)


def _na_bias_table(rpb):
    h = rpb.shape[0]
    return pl.pallas_call(
        _na_table_kernel,
        grid=(h,),
        in_specs=[pl.BlockSpec(memory_space=pltpu.SMEM)],
        out_specs=pl.BlockSpec((1, NA_TABLE, GRID_W, 2 * GRID_W), lambda i: (i, 0, 0, 0)),
        out_shape=jax.ShapeDtypeStruct((h, NA_TABLE, GRID_W, 2 * GRID_W), _F32),
        compiler_params=_params(("arbitrary",)),
        name="na_bias_table",
    )(rpb.reshape(h, -1))


def _na_tile_plan(rows):
    nblk = rows // NA_Q_ROWS
    plan = np.zeros((nblk, NA_Q_ROWS * (NA_K_ROWS // 2)), np.int32)
    for blk in range(nblk):
        rs = blk * NA_Q_ROWS
        ks = min(max(rs - WIN_R // 2, 0), rows - NA_K_ROWS)
        for qi in range(NA_Q_ROWS):
            r0 = min(max(rs + qi - WIN_R // 2, 0), rows - WIN_R)
            for p in range(NA_K_ROWS // 2):
                left = r0 <= ks + 2 * p < r0 + WIN_R
                right = r0 <= ks + 2 * p + 1 < r0 + WIN_R
                pair = ks - rs + 2 * WIN_R - 1 - qi + 2 * p
                if left and right:
                    entry = pair
                elif left:
                    entry = NA_PAIRS + pair
                elif right:
                    entry = 2 * NA_PAIRS + pair
                else:
                    entry = 3 * NA_PAIRS
                plan[blk, qi * (NA_K_ROWS // 2) + p] = entry
    return plan


def _na_attn_kernel(plan_ref, alias_ref, q_ref, k_ref, v_ref, kc_ref, vc_ref, tab_ref, o_ref, *, rows):
    del alias_ref
    blk = pl.program_id(2)
    rs = blk * NA_Q_ROWS
    ks = jnp.clip(rs - WIN_R // 2, 0, rows - NA_K_ROWS)
    k0 = pl.multiple_of(ks * GRID_W, 4 * GRID_W)
    n_keys = NA_K_ROWS * GRID_W
    pairs = NA_K_ROWS // 2
    for hh in range(NA_HEADS):
        cols = slice(hh * HEAD_DIM, (hh + 1) * HEAD_DIM)
        q = q_ref[:, cols]
        k = k_ref[pl.ds(k0, n_keys), cols]
        v = v_ref[pl.ds(k0, n_keys), cols]
        bias = jnp.concatenate(
            [jnp.concatenate([tab_ref[hh, plan_ref[blk, qi * pairs + p]] for p in range(pairs)], axis=1)
             for qi in range(NA_Q_ROWS)], axis=0)
        s = _dot_nt(q, k) + bias
        s_ctx = _dot_nt(q, kc_ref[0, :, cols])
        o_ref[:, cols] = _softmax_pv([s, s_ctx], [v, vc_ref[0, :, cols]]).astype(_BF16)


def _neighbourhood_attention(attn, qkv, k_ctx, v_ctx, table, row0):
    b, l, _ = k_ctx.shape
    t = qkv.shape[0] // b
    rows = t // GRID_W
    tq = NA_Q_ROWS * GRID_W
    nblk = t // tq
    w = NA_HEADS * HEAD_DIM
    hp = N_HEADS // NA_HEADS
    out_blk0 = row0 // tq
    return pl.pallas_call(
        functools.partial(_na_attn_kernel, rows=rows),
        grid_spec=pltpu.PrefetchScalarGridSpec(
            num_scalar_prefetch=1,
            grid=(b, hp, nblk),
            in_specs=[
                pl.BlockSpec(memory_space=pl.ANY),
                pl.BlockSpec((tq, w), lambda bi, h, i, plan: (bi * nblk + i, h)),
                pl.BlockSpec((t, w), lambda bi, h, i, plan: (bi, hp + h)),
                pl.BlockSpec((t, w), lambda bi, h, i, plan: (bi, 2 * hp + h)),
                pl.BlockSpec((1, l, w), lambda bi, h, i, plan: (bi, 0, h)),
                pl.BlockSpec((1, l, w), lambda bi, h, i, plan: (bi, 0, h)),
                pl.BlockSpec((NA_HEADS, NA_TABLE, GRID_W, 2 * GRID_W), lambda bi, h, i, plan: (h, 0, 0, 0)),
            ],
            out_specs=pl.BlockSpec((tq, w), lambda bi, h, i, plan: (out_blk0 + bi * nblk + i, h)),
        ),
        out_shape=jax.ShapeDtypeStruct(attn.shape, attn.dtype),
        input_output_aliases={1: 0},
        compiler_params=_params(("arbitrary", "arbitrary", "arbitrary")),
        name="neighbourhood_attention",
    )(jnp.asarray(_na_tile_plan(rows)), attn, qkv, qkv, qkv, k_ctx, v_ctx, table)


def _gqa_attn_kernel(alias_ref, q_ref, k_ref, v_ref, kc_ref, vc_ref, o_ref, *, group, chunk):
    del alias_ref
    tq = q_ref.shape[0]
    q = jnp.concatenate([q_ref[:, g * HEAD_DIM:(g + 1) * HEAD_DIM] for g in range(group)], axis=0)
    t = k_ref.shape[0]
    m = l = acc = None
    for c in range(t // chunk + 1):
        if c < t // chunk:
            k = k_ref[c * chunk:(c + 1) * chunk, :]
            v = v_ref[c * chunk:(c + 1) * chunk, :]
        else:
            k, v = kc_ref[0], vc_ref[0]
        s = _dot_nt(q, k)
        m_c = jnp.max(s, axis=-1, keepdims=True)
        if m is None:
            m = m_c
            p = jnp.exp2(s - m)
            l = jnp.sum(p, axis=-1, keepdims=True)
            acc = _dot(p.astype(_BF16), v)
        else:
            m_new = jnp.maximum(m, m_c)
            alpha = jnp.exp2(m - m_new)
            p = jnp.exp2(s - m_new)
            l = alpha * l + jnp.sum(p, axis=-1, keepdims=True)
            acc = alpha * acc + _dot(p.astype(_BF16), v)
            m = m_new
    o = acc / l
    for g in range(group):
        o_ref[:, g * HEAD_DIM:(g + 1) * HEAD_DIM] = o[g * tq:(g + 1) * tq].astype(_BF16)


def _gqa_attention(attn, qkv, k_ctx, v_ctx, row0):
    b, l, kv_dim = k_ctx.shape
    n_kv = kv_dim // HEAD_DIM
    group = N_HEADS // n_kv
    t = qkv.shape[0] // b
    tq = 256
    nblk = t // tq
    out_blk0 = row0 // tq
    return pl.pallas_call(
        functools.partial(_gqa_attn_kernel, group=group, chunk=1024),
        grid=(b, n_kv, nblk),
        in_specs=[
            pl.BlockSpec(memory_space=pl.ANY),
            pl.BlockSpec((tq, group * HEAD_DIM), lambda bi, h, i: (bi * nblk + i, h)),
            pl.BlockSpec((t, HEAD_DIM), lambda bi, h, i: (bi, N_HEADS + h)),
            pl.BlockSpec((t, HEAD_DIM), lambda bi, h, i: (bi, N_HEADS + n_kv + h)),
            pl.BlockSpec((1, l, HEAD_DIM), lambda bi, h, i: (bi, 0, h)),
            pl.BlockSpec((1, l, HEAD_DIM), lambda bi, h, i: (bi, 0, h)),
        ],
        out_specs=pl.BlockSpec((tq, group * HEAD_DIM), lambda bi, h, i: (out_blk0 + bi * nblk + i, h)),
        out_shape=jax.ShapeDtypeStruct(attn.shape, attn.dtype),
        input_output_aliases={0: 0},
        compiler_params=_params(("arbitrary", "arbitrary", "arbitrary")),
        name="gqa_attention",
    )(attn, qkv, qkv, qkv, k_ctx, v_ctx)


def _route(logits, lane):
    def masked_softmax(mask):
        z = jnp.where(mask, logits, -jnp.inf)
        e = jnp.exp(z - jnp.max(z, axis=-1, keepdims=True))
        return e / jnp.sum(e, axis=-1, keepdims=True)

    def top1(p, mask):
        best = jnp.max(jnp.where(mask, p, -1.0), axis=-1, keepdims=True)
        idx = jnp.min(jnp.where(mask & (p == best), lane, float(LANES)), axis=-1, keepdims=True)
        return best, idx

    g_mask = lane < N_GROUPS
    g_p, g_idx = top1(masked_softmax(g_mask), g_mask)
    lo = N_GROUPS + g_idx * EXPERTS_PER_GROUP
    e_mask = (lane >= lo) & (lane < lo + EXPERTS_PER_GROUP)
    e_prob = masked_softmax(e_mask)
    p1, i1 = top1(e_prob, e_mask)
    p2, i2 = top1(e_prob, e_mask & (lane != i1))
    return i1 - N_GROUPS, i2 - N_GROUPS, g_p * p1 / (p1 + p2), g_p * p2 / (p1 + p2)


def _proj_router_kernel(a_ref, x_ref, mod_ref, g_ref, wo_ref, wr_ref, br_ref,
                        x1_ref, h_ref, ri_ref, rf_ref, cnt_ref, carry, o_scr, hb_scr):
    i = pl.program_id(0)
    tm = a_ref.shape[0]

    @pl.when(i == 0)
    def _():
        carry[...] = jnp.zeros_like(carry)

    o_scr[...] = _dot(a_ref[...], wo_ref[...])

    chunk = 16

    def body(c, loop_carry):
        rows = pl.ds(pl.multiple_of(c * chunk, chunk), chunk)
        x1 = x_ref[rows, :] + _mod_slices(mod_ref, 2) * o_scr[rows, :]
        x1_ref[rows, :] = x1
        h = _norm_modulate(x1, g_ref[...], _mod_slices(mod_ref, 3), _mod_slices(mod_ref, 4))
        h_ref[rows, :] = h
        hb_scr[rows, :] = h.astype(_BF16)
        return loop_carry

    lax.fori_loop(0, tm // chunk, body, 0, unroll=4)

    logits = _dot(hb_scr[...], wr_ref[...]) + br_ref[...]
    lane = lax.broadcasted_iota(jnp.int32, logits.shape, 1).astype(_F32)
    e1, e2, gate1, gate2 = _route(logits, lane)

    hit1, hit2 = lane == e1, lane == e2
    onehot = jnp.where(hit1 | hit2, 1.0, 0.0)
    row = lax.broadcasted_iota(jnp.int32, (tm, tm), 0)
    col = lax.broadcasted_iota(jnp.int32, (tm, tm), 1)
    lower = jnp.where(row > col, 1.0, 0.0).astype(_BF16)
    before = _dot(lower, onehot.astype(_BF16)) + carry[...]
    rank1 = jnp.sum(jnp.where(hit1, before, 0.0), axis=-1, keepdims=True)
    rank2 = jnp.sum(jnp.where(hit2, before, 0.0), axis=-1, keepdims=True)
    carry[...] = carry[...] + jnp.sum(onehot, axis=0, keepdims=True)

    record = jnp.where(lane == 0, e1, jnp.where(lane == 1, e2,
                       jnp.where(lane == 2, rank1, jnp.where(lane == 3, rank2, 0.0))))
    ri_ref[...] = record.T[:MOD_ROWS].astype(jnp.int32)
    rf_ref[...] = jnp.where(lane == 0, gate1, jnp.where(lane == 1, gate2, 0.0))
    cnt_ref[...] = jnp.broadcast_to(carry[...], cnt_ref.shape)


def _proj_router(attn, x, mod, g, w_o, layer, w_r, b_r):
    m, d = x.shape
    tm = 512
    blocks_per_group = GROUP_ROWS // tm
    row = lambda i: (i, 0)
    const = lambda i: (0, 0)
    return pl.pallas_call(
        _proj_router_kernel,
        grid=(m // tm,),
        in_specs=[
            pl.BlockSpec((tm, d), row),
            pl.BlockSpec((tm, d), row),
            pl.BlockSpec((1, 1, 6 * d), lambda i: (i // blocks_per_group, 0, 0)),
            pl.BlockSpec((1, d), const),
            pl.BlockSpec((None, d, d), lambda i: (layer, 0, 0)),
            pl.BlockSpec((d, LANES), const),
            pl.BlockSpec((1, LANES), const),
        ],
        out_specs=[
            pl.BlockSpec((tm, d), row),
            pl.BlockSpec((tm, d), row),
            pl.BlockSpec((MOD_ROWS, tm), lambda i: (0, i)),
            pl.BlockSpec((tm, LANES), row),
            pl.BlockSpec((MOD_ROWS, LANES), const),
        ],
        out_shape=[
            jax.ShapeDtypeStruct((m, d), _F32),
            jax.ShapeDtypeStruct((m, d), _F32),
            jax.ShapeDtypeStruct((MOD_ROWS, m), jnp.int32),
            jax.ShapeDtypeStruct((m, LANES), _F32),
            jax.ShapeDtypeStruct((MOD_ROWS, LANES), _F32),
        ],
        scratch_shapes=[pltpu.VMEM((1, LANES), _F32), pltpu.VMEM((tm, d), _F32), pltpu.VMEM((tm, d), _BF16)],
        compiler_params=_params(("arbitrary",)),
        name="proj_router",
    )(attn, x, mod, g, w_o, w_r, b_r)


def _record_copy(dest_hbm, idx, isem, blk, slot, tm):
    return pltpu.make_async_copy(dest_hbm.at[:, pl.ds(blk * tm, tm)], idx.at[slot], isem.at[slot])


def _start_row_copies(idx, slot, tm, make_copy):
    for chunk in range(tm // LANES):
        def body(j, carry, chunk=chunk):
            base = chunk * LANES + pl.multiple_of(j * 8, 8)
            for u in range(8):
                for k in range(2):
                    make_copy(base + u, k, idx[slot, k, base + u]).start(priority=k)
            return carry
        lax.fori_loop(0, LANES // 8, body, 0)


def _dispatch_kernel(dest_hbm, h_ref, xs_hbm, idx, isem, sem):
    i = pl.program_id(0)
    n = pl.num_programs(0)
    tm = h_ref.shape[0]

    @pl.when(i == 0)
    def _():
        _record_copy(dest_hbm, idx, isem, 0, 0, tm).start()

    @pl.when(i + 1 < n)
    def _():
        _record_copy(dest_hbm, idx, isem, i + 1, (i + 1) % 2, tm).start()

    _record_copy(dest_hbm, idx, isem, i, i % 2, tm).wait()
    for slot in range(2):
        @pl.when(i % 2 == slot)
        def _(slot=slot):
            _start_row_copies(idx, slot, tm, lambda r, k, d: pltpu.make_async_copy(
                h_ref.at[pl.ds(r, 1)], xs_hbm.at[pl.ds(d, 1)], sem))

    for _ in range(2):
        pltpu.make_async_copy(h_ref, xs_hbm.at[pl.ds(0, tm)], sem).wait()


def _dispatch(h2, dest, p_rows):
    m, d = h2.shape
    tm = 256
    return pl.pallas_call(
        _dispatch_kernel,
        grid=(m // tm,),
        in_specs=[
            pl.BlockSpec(memory_space=pl.ANY),
            pl.BlockSpec((tm, d), lambda i: (i, 0)),
        ],
        out_specs=pl.BlockSpec(memory_space=pl.ANY),
        out_shape=jax.ShapeDtypeStruct((p_rows, d), h2.dtype),
        scratch_shapes=[
            pltpu.SMEM((2, MOD_ROWS, tm), jnp.int32),
            pltpu.SemaphoreType.DMA((2,)),
            pltpu.SemaphoreType.DMA,
        ],
        compiler_params=_params(("arbitrary",)),
        name="dispatch",
    )(dest, h2)


def _expert_kernel(be_ref, valid_ref, next_ref, x_ref, wg_hbm, wu_hbm, wd_hbm, o_ref,
                   stage_g, stage_u, stage_d, wg, wu, wd, g_scr, u_scr, hid_scr, sem, *, layer):
    b = pl.program_id(0)
    valid = valid_ref[b]
    expert = be_ref[b]

    def fetch(e):
        return [pltpu.make_async_copy(src.at[layer, e], dst, sem)
                for src, dst in ((wg_hbm, stage_g), (wu_hbm, stage_u), (wd_hbm, stage_d))]

    @pl.when(b == 0)
    def _():
        for copy in fetch(expert):
            copy.start()

    first = (b == 0) | (expert != be_ref[jnp.maximum(b - 1, 0)])

    @pl.when(first & (valid > 0))
    def _():
        for copy in fetch(expert):
            copy.wait()
        rows = 256
        for stage, dst in ((stage_g, wg), (stage_u, wu), (stage_d, wd)):
            for r in range(0, stage.shape[0], rows):
                dst[r:r + rows, :] = stage[r:r + rows, :].astype(_BF16)

        @pl.when(next_ref[b] >= 0)
        def _():
            for copy in fetch(next_ref[b]):
                copy.start()

    @pl.when(valid > 0)
    def _():
        row = lax.broadcasted_iota(jnp.int32, x_ref.shape, 0)
        x = jnp.where(row < valid, x_ref[...], 0.0).astype(_BF16)
        g_scr[...] = _dot(x, wg[...])
        u_scr[...] = _dot(x, wu[...])

        def body(c, carry):
            rows = pl.ds(pl.multiple_of(c * ROW_CHUNK, ROW_CHUNK), ROW_CHUNK)
            hid_scr[rows, :] = (_silu(g_scr[rows, :]) * u_scr[rows, :]).astype(_BF16)
            return carry

        lax.fori_loop(0, x_ref.shape[0] // ROW_CHUNK, body, 0, unroll=2)
        o_ref[...] = _dot(hid_scr[...], wd[...])

    @pl.when(valid <= 0)
    def _():
        o_ref[...] = jnp.zeros_like(o_ref)


def _experts(xs, block_e, block_valid, block_next, w_gate, w_up, w_down, layer):
    p = xs.shape[0]
    d, de = w_gate.shape[-2:]
    nb = p // MOE_ROWS
    return pl.pallas_call(
        functools.partial(_expert_kernel, layer=layer),
        grid_spec=pltpu.PrefetchScalarGridSpec(
            num_scalar_prefetch=3,
            grid=(nb,),
            in_specs=[
                pl.BlockSpec((MOE_ROWS, d), lambda b, be, nv, nx: (b, 0)),
                pl.BlockSpec(memory_space=pl.ANY),
                pl.BlockSpec(memory_space=pl.ANY),
                pl.BlockSpec(memory_space=pl.ANY),
            ],
            out_specs=pl.BlockSpec((MOE_ROWS, d), lambda b, be, nv, nx: (b, 0)),
            scratch_shapes=[
                pltpu.VMEM((d, de), _F32), pltpu.VMEM((d, de), _F32), pltpu.VMEM((de, d), _F32),
                pltpu.VMEM((d, de), _BF16), pltpu.VMEM((d, de), _BF16), pltpu.VMEM((de, d), _BF16),
                pltpu.VMEM((MOE_ROWS, de), _F32), pltpu.VMEM((MOE_ROWS, de), _F32),
                pltpu.VMEM((MOE_ROWS, de), _BF16),
                pltpu.SemaphoreType.DMA,
            ],
        ),
        out_shape=jax.ShapeDtypeStruct((p, d), _F32),
        compiler_params=_params(("arbitrary",)),
        name="experts",
    )(block_e, block_valid, block_next, xs, w_gate, w_up, w_down)


def _combine_kernel(dest_hbm, yb_hbm, x_ref, mod_ref, rf_ref, o_ref, idx, buf, isem, sem, *, blk0):
    i = pl.program_id(0)
    n = pl.num_programs(0)
    tm = x_ref.shape[0]

    def records(blk, slot):
        return _record_copy(dest_hbm, idx, isem, blk0 + blk, slot, tm)

    def fetch_rows(slot):
        _start_row_copies(idx, slot, tm, lambda r, k, d: pltpu.make_async_copy(
            yb_hbm.at[pl.ds(d, 1)], buf.at[slot, k, pl.ds(r, 1)], sem.at[slot]))

    @pl.when(i == 0)
    def _():
        records(0, 0).start()

        @pl.when(n > 1)
        def _():
            records(1, 1).start()

        records(0, 0).wait()
        fetch_rows(0)

    @pl.when(i + 2 < n)
    def _():
        records(i + 2, i % 2).start()

    for slot in range(2):
        @pl.when((i + 1 < n) & ((i + 1) % 2 == slot))
        def _(slot=slot):
            records(i + 1, slot).wait()
            fetch_rows(slot)

    slot = i % 2
    for k in range(2):
        pltpu.make_async_copy(yb_hbm.at[pl.ds(0, tm)], buf.at[slot, k], sem.at[slot]).wait()
    gate = rf_ref[...]
    y = buf[slot, 0] * gate[:, 0:1] + buf[slot, 1] * gate[:, 1:2]
    o_ref[...] = x_ref[...] + _mod_slices(mod_ref, 5) * y


def _combine(x1, mod, yb, dest, rf, row0, m):
    d = x1.shape[1]
    tm = 256
    blocks_per_group = GROUP_ROWS // tm
    blk0 = row0 // tm
    row = lambda i: (blk0 + i, 0)
    return pl.pallas_call(
        functools.partial(_combine_kernel, blk0=blk0),
        grid=(m // tm,),
        in_specs=[
            pl.BlockSpec(memory_space=pl.ANY),
            pl.BlockSpec(memory_space=pl.ANY),
            pl.BlockSpec((tm, d), row),
            pl.BlockSpec((1, 1, 6 * d), lambda i: ((blk0 + i) // blocks_per_group, 0, 0)),
            pl.BlockSpec((tm, LANES), row),
        ],
        out_specs=pl.BlockSpec((tm, d), lambda i: (i, 0)),
        out_shape=jax.ShapeDtypeStruct((m, d), _F32),
        scratch_shapes=[
            pltpu.SMEM((2, MOD_ROWS, tm), jnp.int32),
            pltpu.VMEM((2, 2, tm, d), _F32),
            pltpu.SemaphoreType.DMA((2,)),
            pltpu.SemaphoreType.DMA((2,)),
        ],
        compiler_params=_params(("arbitrary",)),
        name="combine",
    )(dest, yb, x1, mod, rf)


def _moe(x1, h2, ri, rf, counts, mod, w_gate, w_up, w_down, layer, splits):
    m = x1.shape[0]
    p_rows = (2 * m + N_EXPERTS * (MOE_ROWS - 1) + MOE_ROWS - 1) // MOE_ROWS * MOE_ROWS
    nb = p_rows // MOE_ROWS
    cnt = counts[0, :N_EXPERTS].astype(jnp.int32)
    blocks_e = (cnt + MOE_ROWS - 1) // MOE_ROWS
    end_blk = jnp.cumsum(blocks_e)
    start_blk = end_blk - blocks_e
    starts = (start_blk * MOE_ROWS).astype(jnp.int32)
    blk = jnp.arange(nb, dtype=jnp.int32)
    block_e = jnp.minimum(jnp.sum(end_blk[None, :] <= blk[:, None], axis=1), N_EXPERTS - 1).astype(jnp.int32)
    mine = block_e[:, None] == jnp.arange(N_EXPERTS, dtype=jnp.int32)[None, :]
    cnt_b = jnp.sum(jnp.where(mine, cnt[None, :], 0), axis=1)
    start_b = jnp.sum(jnp.where(mine, start_blk[None, :], 0), axis=1)
    block_valid = jnp.clip(cnt_b - (blk - start_b) * MOE_ROWS, 0, MOE_ROWS).astype(jnp.int32)
    ids = jnp.arange(N_EXPERTS, dtype=jnp.int32)
    later = (ids[None, :] > ids[:, None]) & (cnt[None, :] > 0)
    next_e = jnp.min(jnp.where(later, ids[None, :], N_EXPERTS), axis=1)
    next_e = jnp.where(next_e == N_EXPERTS, -1, next_e)
    block_next = jnp.sum(jnp.where(mine, next_e[None, :], 0), axis=1).astype(jnp.int32)
    experts = jnp.arange(N_EXPERTS, dtype=jnp.int32)[None, :, None]
    dest = jnp.sum(jnp.where(ri[0:2, None, :] == experts, starts[None, :, None], 0), axis=1) + ri[2:4]
    dest = jnp.concatenate([dest, jnp.zeros((MOD_ROWS - 2, m), jnp.int32)], axis=0)
    xs = _dispatch(h2, dest, p_rows)
    yb = _experts(xs, block_e, block_valid, block_next, w_gate, w_up, w_down, layer)
    return [_combine(x1, mod, yb, dest, rf, row0, rows) for row0, rows in splits]


def kernel(x_prompt, x_sample, c, cache_k_a, cache_v_a, cache_k_b, cache_v_b, c_ctx, w_mod, b_mod, norm1_g, norm2_g, w_qkv_a, w_o_a, q_norm_a, k_norm_a, rpb_a, w_qkv_b, w_o_b, q_norm_b, k_norm_b, w_router_group, b_router_group, w_router_expert, b_router_expert, w_gate, w_up, w_down):
    batch, seq, d = x_prompt.shape
    dec_batch, dec_seq, _ = x_sample.shape
    depth = w_mod.shape[0]
    mp = batch * seq
    ms = dec_batch * dec_seq
    assert mp == GROUP_ROWS and dec_seq == GROUP_ROWS and d == D_MODEL
    x = jnp.concatenate([x_prompt.reshape(mp, d), x_sample.reshape(ms, d)], axis=0)

    cond = jnp.concatenate([c_ctx[None], c, jnp.zeros((MOD_ROWS - 1 - dec_batch, d), _F32)], axis=0)
    mod_all = _modulation(cond, w_mod, b_mod)
    rope_tables = _rope_tables(dec_seq)

    w_router = jnp.concatenate([w_router_group, w_router_expert], axis=-1)
    n_route = w_router.shape[-1]
    w_router = jnp.pad(w_router, ((0, 0), (0, 0), (0, LANES - n_route))).astype(_BF16)
    b_router = jnp.pad(jnp.concatenate([b_router_group, b_router_expert], axis=-1),
                       ((0, 0), (0, LANES - n_route)))

    w_qkv_a, w_o_a, w_qkv_b, w_o_b = (_to_bf16(w) for w in (w_qkv_a, w_o_a, w_qkv_b, w_o_b))

    new_kv = [[], [], [], []]
    for i in range(depth):
        j = i // 2
        mod = mod_all[i].reshape(MOD_ROWS, 1, 6 * d)
        g1 = norm1_g[i].reshape(1, d)
        g2 = norm2_g[i].reshape(1, d)
        if i % 2 == 0:
            n_kv, w_qkv, w_o, qn, kn = N_HEADS, w_qkv_a, w_o_a, q_norm_a[j], k_norm_a[j]
            cache_k, cache_v, rope = cache_k_a[:, j], cache_v_a[:, j], None
        else:
            n_kv, w_qkv, w_o, qn, kn = N_KV_B, w_qkv_b, w_o_b, q_norm_b[j], k_norm_b[j]
            cache_k, cache_v, rope = cache_k_b[:, j], cache_v_b[:, j], rope_tables
        kv_dim = n_kv * HEAD_DIM
        qn, kn = qn.reshape(1, HEAD_DIM), kn.reshape(1, HEAD_DIM)
        qkv_p, k_p, v_p = _qkv(x, mod, g1, w_qkv, j, qn, kn, n_kv, row0=0, m=mp, rope_tables=None, emit_kv=True)
        (qkv_s,) = _qkv(x, mod, g1, w_qkv, j, qn, kn, n_kv, row0=mp, m=ms, rope_tables=rope, emit_kv=False)
        new_kv[2 * (i % 2)].append(k_p.reshape(batch, seq, n_kv, HEAD_DIM))
        new_kv[2 * (i % 2) + 1].append(v_p.reshape(batch, seq, n_kv, HEAD_DIM))

        attn = _context_attention(qkv_p, n_kv, seq, mp + ms)
        k_ctx = cache_k.reshape(dec_batch, -1, kv_dim).astype(_BF16)
        v_ctx = cache_v.reshape(dec_batch, -1, kv_dim).astype(_BF16)
        if i % 2 == 0:
            attn = _neighbourhood_attention(attn, qkv_s, k_ctx, v_ctx, _na_bias_table(rpb_a[j]), mp)
        else:
            attn = _gqa_attention(attn, qkv_s, k_ctx, v_ctx, mp)

        x1, h2, ri, rf, counts = _proj_router(attn, x, mod, g2, w_o, j, w_router[i],
                                              b_router[i].reshape(1, LANES))
        last = i == depth - 1
        outs = _moe(x1, h2, ri, rf, counts, mod, w_gate, w_up, w_down, i,
                    [(0, mp), (mp, ms)] if last else [(0, mp + ms)])
        x = outs[0]

    return (outs[0].reshape(batch, seq, d), outs[1].reshape(dec_batch, dec_seq, d),
            jnp.stack(new_kv[0], axis=1), jnp.stack(new_kv[1], axis=1),
            jnp.stack(new_kv[2], axis=1), jnp.stack(new_kv[3], axis=1))
```

```python
import functools
import math

import numpy as np

import jax
import jax.numpy as jnp
from jax import lax
from jax.experimental import pallas as pl
from jax.experimental.pallas import tpu as pltpu

D_MODEL = 2048
HEAD_DIM = 128
N_HEADS = D_MODEL // HEAD_DIM
N_KV_B = 4
GRID_W = 64
WIN_R = 8
WIN_C = 16
ROPE_THETA = 10000.0
N_GROUPS = 4
EXPERTS_PER_GROUP = 8
N_EXPERTS = N_GROUPS * EXPERTS_PER_GROUP
D_EXPERT = 768
EPS = 1e-6
NEG_INF = -1e30

LANES = 128
MOD_ROWS = 8
GROUP_ROWS = 4096
MOE_ROWS = 256
NA_Q_ROWS = 8
NA_K_ROWS = 16
NA_PAIRS = 30
NA_TABLE = 3 * NA_PAIRS + 1
NA_HEADS = 2
VMEM_LIMIT = 56 * 1024 * 1024
LOG2_E = math.log2(math.e)
Q_SCALE = HEAD_DIM ** -0.5 * LOG2_E

_BF16 = jnp.bfloat16
_F32 = jnp.float32


def _params(sem, vmem=VMEM_LIMIT):
    return pltpu.CompilerParams(dimension_semantics=sem, vmem_limit_bytes=vmem)


def _dot(a, b):
    return jnp.dot(a, b, preferred_element_type=_F32)


def _dot_nt(a, b):
    return lax.dot_general(a, b, (((1,), (1,)), ((), ())), preferred_element_type=_F32)


def _silu(x):
    return x / (1.0 + jnp.exp(-x))


def _cast_kernel(w_ref, o_ref):
    o_ref[...] = w_ref[...].astype(o_ref.dtype)


def _to_bf16(w):
    cols = w.shape[-1]
    w2 = w.reshape(-1, cols)
    rows = w2.shape[0]
    rb = max(8, min(rows, (4 * 1024 * 1024) // (4 * cols)) // 8 * 8)
    while rows % rb:
        rb -= 8
    out = pl.pallas_call(
        _cast_kernel,
        grid=(rows // rb,),
        in_specs=[pl.BlockSpec((rb, cols), lambda i: (i, 0))],
        out_specs=pl.BlockSpec((rb, cols), lambda i: (i, 0)),
        out_shape=jax.ShapeDtypeStruct((rows, cols), _BF16),
        compiler_params=_params(("arbitrary",)),
        name="to_bf16",
    )(w2)
    return out.reshape(w.shape)


def _mod_kernel(cond_ref, w_ref, b_ref, o_ref):
    s = _silu(cond_ref[...]).astype(_BF16)
    o_ref[...] = _dot(s, w_ref[...].astype(_BF16)) + b_ref[...]


def _modulation(cond, w_mod, b_mod):
    depth, d, n = w_mod.shape
    tn = 1024
    return pl.pallas_call(
        _mod_kernel,
        grid=(depth, n // tn),
        in_specs=[
            pl.BlockSpec((MOD_ROWS, d), lambda l, j: (0, 0)),
            pl.BlockSpec((None, d, tn), lambda l, j: (l, 0, j)),
            pl.BlockSpec((None, 1, tn), lambda l, j: (l, 0, j)),
        ],
        out_specs=pl.BlockSpec((None, MOD_ROWS, tn), lambda l, j: (l, 0, j)),
        out_shape=jax.ShapeDtypeStruct((depth, MOD_ROWS, n), _F32),
        compiler_params=_params(("arbitrary", "arbitrary")),
        name="modulation",
    )(cond, w_mod, b_mod.reshape(depth, 1, n))


def _mod_slices(mod_ref, which):
    return mod_ref[0, :, which * D_MODEL:(which + 1) * D_MODEL]


def _norm_modulate(x, g, shift, scale):
    y = x * lax.rsqrt(jnp.mean(x * x, axis=-1, keepdims=True) + EPS)
    return (y * g) * (1.0 + scale) + shift


def _head_norm(a, g):
    return a * lax.rsqrt(jnp.mean(a * a, axis=-1, keepdims=True) + EPS) * g


def _rope_partner_matrix():
    j = np.arange(HEAD_DIM)
    partner = np.where((j % 64) < 32, j + 32, j - 32)
    p = np.zeros((HEAD_DIM, HEAD_DIM), np.float32)
    p[partner, j] = 1.0
    return jnp.asarray(p, _BF16)


def _rope(a, cos, sin, partner_ref):
    return a * cos + _dot(a.astype(_BF16), partner_ref[...]) * sin


def _qkv_kernel(*refs, n_q, n_k, heads_per_tile, rope, emit_kv):
    x_ref, mod_ref, g_ref, w_ref, qn_ref, kn_ref = refs[:6]
    refs = refs[6:]
    if rope:
        cos_ref, sin_ref, partner_ref = refs[:3]
        refs = refs[3:]
    o_ref = refs[0]
    k32_ref, v32_ref = refs[1:3] if emit_kv else (None, None)
    h_scr = refs[-1]
    j = pl.program_id(1)

    @pl.when(j == 0)
    def _():
        h = _norm_modulate(x_ref[...], g_ref[...], _mod_slices(mod_ref, 0), _mod_slices(mod_ref, 1))
        h_scr[...] = h.astype(_BF16)

    acc = _dot(h_scr[...], w_ref[...])

    def heads(gain_ref, scale):
        gain = gain_ref[...] * scale
        outs = []
        for hh in range(heads_per_tile):
            a = _head_norm(acc[:, hh * HEAD_DIM:(hh + 1) * HEAD_DIM], gain)
            if rope:
                a = _rope(a, cos_ref[...], sin_ref[...], partner_ref)
            outs.append(a)
        return jnp.concatenate(outs, axis=1)

    @pl.when(j < n_q)
    def _():
        o_ref[...] = heads(qn_ref, Q_SCALE).astype(_BF16)

    @pl.when((j >= n_q) & (j < n_q + n_k))
    def _():
        k = heads(kn_ref, 1.0)
        o_ref[...] = k.astype(_BF16)
        if emit_kv:
            k32_ref[...] = k

    @pl.when(j >= n_q + n_k)
    def _():
        o_ref[...] = acc.astype(_BF16)
        if emit_kv:
            v32_ref[...] = acc


def _qkv(x, mod, g, w, layer, qn, kn, n_kv, *, row0, m, rope_tables, emit_kv):
    d = x.shape[1]
    n = w.shape[-1]
    tm, tn = 1024, 512
    heads_per_tile = tn // HEAD_DIM
    n_q = D_MODEL // tn
    n_k = n_kv * HEAD_DIM // tn
    rope = rope_tables is not None
    blocks_per_group = GROUP_ROWS // tm
    blk0 = row0 // tm

    def group(i):
        return (blk0 + i) // blocks_per_group

    in_specs = [
        pl.BlockSpec((tm, d), lambda i, j: (blk0 + i, 0)),
        pl.BlockSpec((1, 1, 6 * d), lambda i, j: (group(i), 0, 0)),
        pl.BlockSpec((1, d), lambda i, j: (0, 0)),
        pl.BlockSpec((None, d, tn), lambda i, j: (layer, 0, j)),
        pl.BlockSpec((1, HEAD_DIM), lambda i, j: (0, 0)),
        pl.BlockSpec((1, HEAD_DIM), lambda i, j: (0, 0)),
    ]
    args = [x, mod, g, w, qn, kn]
    if rope:
        pos = lambda i, j: (i % blocks_per_group, 0)
        in_specs += [pl.BlockSpec((tm, HEAD_DIM), pos), pl.BlockSpec((tm, HEAD_DIM), pos),
                     pl.BlockSpec((HEAD_DIM, HEAD_DIM), lambda i, j: (0, 0))]
        args += list(rope_tables) + [_rope_partner_matrix()]
    out_specs = [pl.BlockSpec((tm, tn), lambda i, j: (i, j))]
    out_shape = [jax.ShapeDtypeStruct((m, n), _BF16)]
    if emit_kv:
        out_specs += [pl.BlockSpec((tm, tn), lambda i, j: (i, jnp.clip(j - n_q, 0, n_k - 1))),
                      pl.BlockSpec((tm, tn), lambda i, j: (i, jnp.clip(j - n_q - n_k, 0, n_k - 1)))]
        out_shape += [jax.ShapeDtypeStruct((m, n_k * tn), _F32)] * 2
    return pl.pallas_call(
        functools.partial(_qkv_kernel, n_q=n_q, n_k=n_k, heads_per_tile=heads_per_tile,
                          rope=rope, emit_kv=emit_kv),
        grid=(m // tm, n // tn),
        in_specs=in_specs,
        out_specs=out_specs,
        out_shape=out_shape,
        scratch_shapes=[pltpu.VMEM((tm, d), _BF16)],
        compiler_params=_params(("arbitrary", "arbitrary")),
        name="qkv_rope" if rope else "qkv",
    )(*args)


def _rope_tables(n_tokens):
    t = jnp.arange(n_tokens)
    quarter = HEAD_DIM // 4
    inv = ROPE_THETA ** (-jnp.arange(quarter, dtype=_F32) / quarter)
    ang_r = (t // GRID_W).astype(_F32)[:, None] * inv
    ang_c = (t % GRID_W).astype(_F32)[:, None] * inv
    cr, sr, cc, sc = jnp.cos(ang_r), jnp.sin(ang_r), jnp.cos(ang_c), jnp.sin(ang_c)
    return (jnp.concatenate([cr, cr, cc, cc], axis=1), jnp.concatenate([-sr, sr, -sc, sc], axis=1))


def _softmax_pv(scores, values):
    m = functools.reduce(jnp.maximum, [jnp.max(s, axis=-1, keepdims=True) for s in scores])
    ps = [jnp.exp2(s - m) for s in scores]
    l = functools.reduce(jnp.add, [jnp.sum(p, axis=-1, keepdims=True) for p in ps])
    o = functools.reduce(jnp.add, [_dot(p.astype(_BF16), v) for p, v in zip(ps, values)])
    return o / l


def _ctx_attn_kernel(q_ref, k_ref, v_ref, o_ref, *, group):
    for h in range(N_HEADS):
        kv = h // group
        q = q_ref[:, h * HEAD_DIM:(h + 1) * HEAD_DIM]
        k = k_ref[:, kv * HEAD_DIM:(kv + 1) * HEAD_DIM]
        v = v_ref[:, kv * HEAD_DIM:(kv + 1) * HEAD_DIM]
        o = _softmax_pv([_dot_nt(q, k)], [v])
        o_ref[:, h * HEAD_DIM:(h + 1) * HEAD_DIM] = o.astype(_BF16)


def _context_attention(qkv, n_kv, seq, n_rows_total):
    m = qkv.shape[0]
    kv_dim = n_kv * HEAD_DIM
    k_blk = D_MODEL // kv_dim
    return pl.pallas_call(
        functools.partial(_ctx_attn_kernel, group=N_HEADS // n_kv),
        grid=(m // seq,),
        in_specs=[
            pl.BlockSpec((seq, D_MODEL), lambda b: (b, 0)),
            pl.BlockSpec((seq, kv_dim), lambda b: (b, k_blk)),
            pl.BlockSpec((seq, kv_dim), lambda b: (b, k_blk + 1)),
        ],
        out_specs=pl.BlockSpec((seq, D_MODEL), lambda b: (b, 0)),
        out_shape=jax.ShapeDtypeStruct((n_rows_total, D_MODEL), _BF16),
        compiler_params=_params(("arbitrary",)),
        name="context_attention",
    )(qkv, qkv, qkv)


def _na_table_kernel(rpb_ref, o_ref):
    h = pl.program_id(0)
    c = lax.broadcasted_iota(jnp.int32, (GRID_W, GRID_W), 0)
    kc = lax.broadcasted_iota(jnp.int32, (GRID_W, GRID_W), 1)
    start = jnp.clip(c - WIN_C // 2, 0, GRID_W - WIN_C)
    col_ok = (kc >= start) & (kc < start + WIN_C)
    rel = kc - c + WIN_C - 1
    n_dc = 2 * WIN_C - 1
    masked = jnp.full((GRID_W, GRID_W), NEG_INF, _F32)
    tiles = []
    for d in range(-8, NA_PAIRS - 8 + 1):
        if 0 <= d < 2 * WIN_R - 1:
            t = jnp.zeros((GRID_W, GRID_W), _F32)
            for jj in range(n_dc):
                t = jnp.where(rel == jj, rpb_ref[h, d * n_dc + jj] * LOG2_E, t)
            tiles.append(jnp.where(col_ok, t, NEG_INF))
        else:
            tiles.append(masked)
    for i in range(NA_PAIRS):
        o_ref[0, i] = jnp.concatenate([tiles[i], tiles[i + 1]], axis=1)
        o_ref[0, NA_PAIRS + i] = jnp.concatenate([tiles[i], masked], axis=1)
        o_ref[0, 2 * NA_PAIRS + i] = jnp.concatenate([masked, tiles[i + 1]], axis=1)
    o_ref[0, 3 * NA_PAIRS] = jnp.concatenate([masked, masked], axis=1)


def _na_bias_table(rpb):
    h = rpb.shape[0]
    return pl.pallas_call(
        _na_table_kernel,
        grid=(h,),
        in_specs=[pl.BlockSpec(memory_space=pltpu.SMEM)],
        out_specs=pl.BlockSpec((1, NA_TABLE, GRID_W, 2 * GRID_W), lambda i: (i, 0, 0, 0)),
        out_shape=jax.ShapeDtypeStruct((h, NA_TABLE, GRID_W, 2 * GRID_W), _F32),
        compiler_params=_params(("arbitrary",)),
        name="na_bias_table",
    )(rpb.reshape(h, -1))


def _na_tile_plan(rows):
    nblk = rows // NA_Q_ROWS
    plan = np.zeros((nblk, NA_Q_ROWS * (NA_K_ROWS // 2)), np.int32)
    for blk in range(nblk):
        rs = blk * NA_Q_ROWS
        ks = min(max(rs - WIN_R // 2, 0), rows - NA_K_ROWS)
        for qi in range(NA_Q_ROWS):
            r0 = min(max(rs + qi - WIN_R // 2, 0), rows - WIN_R)
            for p in range(NA_K_ROWS // 2):
                left = r0 <= ks + 2 * p < r0 + WIN_R
                right = r0 <= ks + 2 * p + 1 < r0 + WIN_R
                pair = ks - rs + 2 * WIN_R - 1 - qi + 2 * p
                if left and right:
                    entry = pair
                elif left:
                    entry = NA_PAIRS + pair
                elif right:
                    entry = 2 * NA_PAIRS + pair
                else:
                    entry = 3 * NA_PAIRS
                plan[blk, qi * (NA_K_ROWS // 2) + p] = entry
    return plan


def _na_attn_kernel(plan_ref, alias_ref, q_ref, k_ref, v_ref, kc_ref, vc_ref, tab_ref, o_ref, *, rows):
    del alias_ref
    blk = pl.program_id(2)
    rs = blk * NA_Q_ROWS
    ks = jnp.clip(rs - WIN_R // 2, 0, rows - NA_K_ROWS)
    k0 = pl.multiple_of(ks * GRID_W, 4 * GRID_W)
    n_keys = NA_K_ROWS * GRID_W
    pairs = NA_K_ROWS // 2
    for hh in range(NA_HEADS):
        cols = slice(hh * HEAD_DIM, (hh + 1) * HEAD_DIM)
        q = q_ref[:, cols]
        k = k_ref[pl.ds(k0, n_keys), cols]
        v = v_ref[pl.ds(k0, n_keys), cols]
        bias = jnp.concatenate(
            [jnp.concatenate([tab_ref[hh, plan_ref[blk, qi * pairs + p]] for p in range(pairs)], axis=1)
             for qi in range(NA_Q_ROWS)], axis=0)
        s = _dot_nt(q, k) + bias
        s_ctx = _dot_nt(q, kc_ref[0, :, cols])
        o_ref[:, cols] = _softmax_pv([s, s_ctx], [v, vc_ref[0, :, cols]]).astype(_BF16)


def _neighbourhood_attention(attn, qkv, k_ctx, v_ctx, table, row0):
    b, l, _ = k_ctx.shape
    t = qkv.shape[0] // b
    rows = t // GRID_W
    tq = NA_Q_ROWS * GRID_W
    nblk = t // tq
    w = NA_HEADS * HEAD_DIM
    hp = N_HEADS // NA_HEADS
    out_blk0 = row0 // tq
    return pl.pallas_call(
        functools.partial(_na_attn_kernel, rows=rows),
        grid_spec=pltpu.PrefetchScalarGridSpec(
            num_scalar_prefetch=1,
            grid=(b, hp, nblk),
            in_specs=[
                pl.BlockSpec(memory_space=pl.ANY),
                pl.BlockSpec((tq, w), lambda bi, h, i, plan: (bi * nblk + i, h)),
                pl.BlockSpec((t, w), lambda bi, h, i, plan: (bi, hp + h)),
                pl.BlockSpec((t, w), lambda bi, h, i, plan: (bi, 2 * hp + h)),
                pl.BlockSpec((1, l, w), lambda bi, h, i, plan: (bi, 0, h)),
                pl.BlockSpec((1, l, w), lambda bi, h, i, plan: (bi, 0, h)),
                pl.BlockSpec((NA_HEADS, NA_TABLE, GRID_W, 2 * GRID_W), lambda bi, h, i, plan: (h, 0, 0, 0)),
            ],
            out_specs=pl.BlockSpec((tq, w), lambda bi, h, i, plan: (out_blk0 + bi * nblk + i, h)),
        ),
        out_shape=jax.ShapeDtypeStruct(attn.shape, attn.dtype),
        input_output_aliases={1: 0},
        compiler_params=_params(("arbitrary", "arbitrary", "arbitrary")),
        name="neighbourhood_attention",
    )(jnp.asarray(_na_tile_plan(rows)), attn, qkv, qkv, qkv, k_ctx, v_ctx, table)


def _gqa_attn_kernel(alias_ref, q_ref, k_ref, v_ref, kc_ref, vc_ref, o_ref, *, group, chunk):
    del alias_ref
    tq = q_ref.shape[0]
    q = jnp.concatenate([q_ref[:, g * HEAD_DIM:(g + 1) * HEAD_DIM] for g in range(group)], axis=0)
    t = k_ref.shape[0]
    m = l = acc = None
    for c in range(t // chunk + 1):
        if c < t // chunk:
            k = k_ref[c * chunk:(c + 1) * chunk, :]
            v = v_ref[c * chunk:(c + 1) * chunk, :]
        else:
            k, v = kc_ref[0], vc_ref[0]
        s = _dot_nt(q, k)
        m_c = jnp.max(s, axis=-1, keepdims=True)
        if m is None:
            m = m_c
            p = jnp.exp2(s - m)
            l = jnp.sum(p, axis=-1, keepdims=True)
            acc = _dot(p.astype(_BF16), v)
        else:
            m_new = jnp.maximum(m, m_c)
            alpha = jnp.exp2(m - m_new)
            p = jnp.exp2(s - m_new)
            l = alpha * l + jnp.sum(p, axis=-1, keepdims=True)
            acc = alpha * acc + _dot(p.astype(_BF16), v)
            m = m_new
    o = acc / l
    for g in range(group):
        o_ref[:, g * HEAD_DIM:(g + 1) * HEAD_DIM] = o[g * tq:(g + 1) * tq].astype(_BF16)


def _gqa_attention(attn, qkv, k_ctx, v_ctx, row0):
    b, l, kv_dim = k_ctx.shape
    n_kv = kv_dim // HEAD_DIM
    group = N_HEADS // n_kv
    t = qkv.shape[0] // b
    tq = 256
    nblk = t // tq
    out_blk0 = row0 // tq
    return pl.pallas_call(
        functools.partial(_gqa_attn_kernel, group=group, chunk=1024),
        grid=(b, n_kv, nblk),
        in_specs=[
            pl.BlockSpec(memory_space=pl.ANY),
            pl.BlockSpec((tq, group * HEAD_DIM), lambda bi, h, i: (bi * nblk + i, h)),
            pl.BlockSpec((t, HEAD_DIM), lambda bi, h, i: (bi, N_HEADS + h)),
            pl.BlockSpec((t, HEAD_DIM), lambda bi, h, i: (bi, N_HEADS + n_kv + h)),
            pl.BlockSpec((1, l, HEAD_DIM), lambda bi, h, i: (bi, 0, h)),
            pl.BlockSpec((1, l, HEAD_DIM), lambda bi, h, i: (bi, 0, h)),
        ],
        out_specs=pl.BlockSpec((tq, group * HEAD_DIM), lambda bi, h, i: (out_blk0 + bi * nblk + i, h)),
        out_shape=jax.ShapeDtypeStruct(attn.shape, attn.dtype),
        input_output_aliases={0: 0},
        compiler_params=_params(("arbitrary", "arbitrary", "arbitrary")),
        name="gqa_attention",
    )(attn, qkv, qkv, qkv, k_ctx, v_ctx)


def _route(logits, lane):
    def masked_softmax(mask):
        z = jnp.where(mask, logits, -jnp.inf)
        e = jnp.exp(z - jnp.max(z, axis=-1, keepdims=True))
        return e / jnp.sum(e, axis=-1, keepdims=True)

    def top1(p, mask):
        best = jnp.max(jnp.where(mask, p, -1.0), axis=-1, keepdims=True)
        idx = jnp.min(jnp.where(mask & (p == best), lane, float(LANES)), axis=-1, keepdims=True)
        return best, idx

    g_mask = lane < N_GROUPS
    g_p, g_idx = top1(masked_softmax(g_mask), g_mask)
    lo = N_GROUPS + g_idx * EXPERTS_PER_GROUP
    e_mask = (lane >= lo) & (lane < lo + EXPERTS_PER_GROUP)
    e_prob = masked_softmax(e_mask)
    p1, i1 = top1(e_prob, e_mask)
    p2, i2 = top1(e_prob, e_mask & (lane != i1))
    return i1 - N_GROUPS, i2 - N_GROUPS, g_p * p1 / (p1 + p2), g_p * p2 / (p1 + p2)


def _proj_router_kernel(a_ref, x_ref, mod_ref, g_ref, wo_ref, wr_ref, br_ref,
                        x1_ref, h_ref, ri_ref, rf_ref, cnt_ref, carry):
    i = pl.program_id(0)
    tm = a_ref.shape[0]

    @pl.when(i == 0)
    def _():
        carry[...] = jnp.zeros_like(carry)

    o = _dot(a_ref[...], wo_ref[...])
    x1 = x_ref[...] + _mod_slices(mod_ref, 2) * o
    x1_ref[...] = x1
    h = _norm_modulate(x1, g_ref[...], _mod_slices(mod_ref, 3), _mod_slices(mod_ref, 4))
    h_ref[...] = h

    logits = _dot(h.astype(_BF16), wr_ref[...]) + br_ref[...]
    lane = lax.broadcasted_iota(jnp.int32, logits.shape, 1).astype(_F32)
    e1, e2, gate1, gate2 = _route(logits, lane)

    hit1, hit2 = lane == e1, lane == e2
    onehot = jnp.where(hit1 | hit2, 1.0, 0.0)
    row = lax.broadcasted_iota(jnp.int32, (tm, tm), 0)
    col = lax.broadcasted_iota(jnp.int32, (tm, tm), 1)
    lower = jnp.where(row > col, 1.0, 0.0).astype(_BF16)
    before = _dot(lower, onehot.astype(_BF16)) + carry[...]
    rank1 = jnp.sum(jnp.where(hit1, before, 0.0), axis=-1, keepdims=True)
    rank2 = jnp.sum(jnp.where(hit2, before, 0.0), axis=-1, keepdims=True)
    carry[...] = carry[...] + jnp.sum(onehot, axis=0, keepdims=True)

    record = jnp.where(lane == 0, e1, jnp.where(lane == 1, e2,
                       jnp.where(lane == 2, rank1, jnp.where(lane == 3, rank2, 0.0))))
    ri_ref[...] = record.T[:MOD_ROWS].astype(jnp.int32)
    rf_ref[...] = jnp.where(lane == 0, gate1, jnp.where(lane == 1, gate2, 0.0))
    cnt_ref[...] = jnp.broadcast_to(carry[...], cnt_ref.shape)


def _proj_router(attn, x, mod, g, w_o, layer, w_r, b_r):
    m, d = x.shape
    tm = 512
    blocks_per_group = GROUP_ROWS // tm
    row = lambda i: (i, 0)
    const = lambda i: (0, 0)
    return pl.pallas_call(
        _proj_router_kernel,
        grid=(m // tm,),
        in_specs=[
            pl.BlockSpec((tm, d), row),
            pl.BlockSpec((tm, d), row),
            pl.BlockSpec((1, 1, 6 * d), lambda i: (i // blocks_per_group, 0, 0)),
            pl.BlockSpec((1, d), const),
            pl.BlockSpec((None, d, d), lambda i: (layer, 0, 0)),
            pl.BlockSpec((d, LANES), const),
            pl.BlockSpec((1, LANES), const),
        ],
        out_specs=[
            pl.BlockSpec((tm, d), row),
            pl.BlockSpec((tm, d), row),
            pl.BlockSpec((MOD_ROWS, tm), lambda i: (0, i)),
            pl.BlockSpec((tm, LANES), row),
            pl.BlockSpec((MOD_ROWS, LANES), const),
        ],
        out_shape=[
            jax.ShapeDtypeStruct((m, d), _F32),
            jax.ShapeDtypeStruct((m, d), _F32),
            jax.ShapeDtypeStruct((MOD_ROWS, m), jnp.int32),
            jax.ShapeDtypeStruct((m, LANES), _F32),
            jax.ShapeDtypeStruct((MOD_ROWS, LANES), _F32),
        ],
        scratch_shapes=[pltpu.VMEM((1, LANES), _F32)],
        compiler_params=_params(("arbitrary",)),
        name="proj_router",
    )(attn, x, mod, g, w_o, w_r, b_r)


def _record_copy(dest_hbm, idx, isem, blk, slot, tm):
    return pltpu.make_async_copy(dest_hbm.at[:, pl.ds(blk * tm, tm)], idx.at[slot], isem.at[slot])


def _start_row_copies(idx, slot, tm, make_copy):
    for chunk in range(tm // LANES):
        def body(j, carry, chunk=chunk):
            base = chunk * LANES + pl.multiple_of(j * 8, 8)
            for u in range(8):
                for k in range(2):
                    make_copy(base + u, k, idx[slot, k, base + u]).start(priority=k)
            return carry
        lax.fori_loop(0, LANES // 8, body, 0)


def _dispatch_kernel(dest_hbm, h_ref, xs_hbm, idx, isem, sem):
    i = pl.program_id(0)
    n = pl.num_programs(0)
    tm = h_ref.shape[0]

    @pl.when(i == 0)
    def _():
        _record_copy(dest_hbm, idx, isem, 0, 0, tm).start()

    @pl.when(i + 1 < n)
    def _():
        _record_copy(dest_hbm, idx, isem, i + 1, (i + 1) % 2, tm).start()

    _record_copy(dest_hbm, idx, isem, i, i % 2, tm).wait()
    for slot in range(2):
        @pl.when(i % 2 == slot)
        def _(slot=slot):
            _start_row_copies(idx, slot, tm, lambda r, k, d: pltpu.make_async_copy(
                h_ref.at[pl.ds(r, 1)], xs_hbm.at[pl.ds(d, 1)], sem))

    for _ in range(2):
        pltpu.make_async_copy(h_ref, xs_hbm.at[pl.ds(0, tm)], sem).wait()


def _dispatch(h2, dest, p_rows):
    m, d = h2.shape
    tm = 256
    return pl.pallas_call(
        _dispatch_kernel,
        grid=(m // tm,),
        in_specs=[
            pl.BlockSpec(memory_space=pl.ANY),
            pl.BlockSpec((tm, d), lambda i: (i, 0)),
        ],
        out_specs=pl.BlockSpec(memory_space=pl.ANY),
        out_shape=jax.ShapeDtypeStruct((p_rows, d), h2.dtype),
        scratch_shapes=[
            pltpu.SMEM((2, MOD_ROWS, tm), jnp.int32),
            pltpu.SemaphoreType.DMA((2,)),
            pltpu.SemaphoreType.DMA,
        ],
        compiler_params=_params(("arbitrary",)),
        name="dispatch",
    )(dest, h2)


def _expert_kernel(be_ref, valid_ref, next_ref, x_ref, wg_hbm, wu_hbm, wd_hbm, o_ref,
                   stage_g, stage_u, stage_d, wg, wu, wd, sem, *, layer):
    b = pl.program_id(0)
    valid = valid_ref[b]
    expert = be_ref[b]

    def fetch(e):
        return [pltpu.make_async_copy(src.at[layer, e], dst, sem)
                for src, dst in ((wg_hbm, stage_g), (wu_hbm, stage_u), (wd_hbm, stage_d))]

    @pl.when(b == 0)
    def _():
        for copy in fetch(expert):
            copy.start()

    first = (b == 0) | (expert != be_ref[jnp.maximum(b - 1, 0)])

    @pl.when(first & (valid > 0))
    def _():
        for copy in fetch(expert):
            copy.wait()
        rows = 256
        for stage, dst in ((stage_g, wg), (stage_u, wu), (stage_d, wd)):
            for r in range(0, stage.shape[0], rows):
                dst[r:r + rows, :] = stage[r:r + rows, :].astype(_BF16)

        @pl.when(next_ref[b] >= 0)
        def _():
            for copy in fetch(next_ref[b]):
                copy.start()

    @pl.when(valid > 0)
    def _():
        row = lax.broadcasted_iota(jnp.int32, x_ref.shape, 0)
        x = jnp.where(row < valid, x_ref[...], 0.0).astype(_BF16)
        hidden = _silu(_dot(x, wg[...])) * _dot(x, wu[...])
        o_ref[...] = _dot(hidden.astype(_BF16), wd[...])

    @pl.when(valid <= 0)
    def _():
        o_ref[...] = jnp.zeros_like(o_ref)


def _experts(xs, block_e, block_valid, block_next, w_gate, w_up, w_down, layer):
    p = xs.shape[0]
    d, de = w_gate.shape[-2:]
    nb = p // MOE_ROWS
    return pl.pallas_call(
        functools.partial(_expert_kernel, layer=layer),
        grid_spec=pltpu.PrefetchScalarGridSpec(
            num_scalar_prefetch=3,
            grid=(nb,),
            in_specs=[
                pl.BlockSpec((MOE_ROWS, d), lambda b, be, nv, nx: (b, 0)),
                pl.BlockSpec(memory_space=pl.ANY),
                pl.BlockSpec(memory_space=pl.ANY),
                pl.BlockSpec(memory_space=pl.ANY),
            ],
            out_specs=pl.BlockSpec((MOE_ROWS, d), lambda b, be, nv, nx: (b, 0)),
            scratch_shapes=[
                pltpu.VMEM((d, de), _F32), pltpu.VMEM((d, de), _F32), pltpu.VMEM((de, d), _F32),
                pltpu.VMEM((d, de), _BF16), pltpu.VMEM((d, de), _BF16), pltpu.VMEM((de, d), _BF16),
                pltpu.SemaphoreType.DMA,
            ],
        ),
        out_shape=jax.ShapeDtypeStruct((p, d), _F32),
        compiler_params=_params(("arbitrary",)),
        name="experts",
    )(block_e, block_valid, block_next, xs, w_gate, w_up, w_down)


def _combine_kernel(dest_hbm, yb_hbm, x_ref, mod_ref, rf_ref, o_ref, idx, buf, isem, sem, *, blk0):
    i = pl.program_id(0)
    n = pl.num_programs(0)
    tm = x_ref.shape[0]

    def records(blk, slot):
        return _record_copy(dest_hbm, idx, isem, blk0 + blk, slot, tm)

    def fetch_rows(slot):
        _start_row_copies(idx, slot, tm, lambda r, k, d: pltpu.make_async_copy(
            yb_hbm.at[pl.ds(d, 1)], buf.at[slot, k, pl.ds(r, 1)], sem.at[slot]))

    @pl.when(i == 0)
    def _():
        records(0, 0).start()

        @pl.when(n > 1)
        def _():
            records(1, 1).start()

        records(0, 0).wait()
        fetch_rows(0)

    @pl.when(i + 2 < n)
    def _():
        records(i + 2, i % 2).start()

    for slot in range(2):
        @pl.when((i + 1 < n) & ((i + 1) % 2 == slot))
        def _(slot=slot):
            records(i + 1, slot).wait()
            fetch_rows(slot)

    slot = i % 2
    for k in range(2):
        pltpu.make_async_copy(yb_hbm.at[pl.ds(0, tm)], buf.at[slot, k], sem.at[slot]).wait()
    gate = rf_ref[...]
    y = buf[slot, 0] * gate[:, 0:1] + buf[slot, 1] * gate[:, 1:2]
    o_ref[...] = x_ref[...] + _mod_slices(mod_ref, 5) * y


def _combine(x1, mod, yb, dest, rf, row0, m):
    d = x1.shape[1]
    tm = 256
    blocks_per_group = GROUP_ROWS // tm
    blk0 = row0 // tm
    row = lambda i: (blk0 + i, 0)
    return pl.pallas_call(
        functools.partial(_combine_kernel, blk0=blk0),
        grid=(m // tm,),
        in_specs=[
            pl.BlockSpec(memory_space=pl.ANY),
            pl.BlockSpec(memory_space=pl.ANY),
            pl.BlockSpec((tm, d), row),
            pl.BlockSpec((1, 1, 6 * d), lambda i: ((blk0 + i) // blocks_per_group, 0, 0)),
            pl.BlockSpec((tm, LANES), row),
        ],
        out_specs=pl.BlockSpec((tm, d), lambda i: (i, 0)),
        out_shape=jax.ShapeDtypeStruct((m, d), _F32),
        scratch_shapes=[
            pltpu.SMEM((2, MOD_ROWS, tm), jnp.int32),
            pltpu.VMEM((2, 2, tm, d), _F32),
            pltpu.SemaphoreType.DMA((2,)),
            pltpu.SemaphoreType.DMA((2,)),
        ],
        compiler_params=_params(("arbitrary",)),
        name="combine",
    )(dest, yb, x1, mod, rf)


def _moe(x1, h2, ri, rf, counts, mod, w_gate, w_up, w_down, layer, splits):
    m = x1.shape[0]
    p_rows = (2 * m + N_EXPERTS * (MOE_ROWS - 1) + MOE_ROWS - 1) // MOE_ROWS * MOE_ROWS
    nb = p_rows // MOE_ROWS
    cnt = counts[0, :N_EXPERTS].astype(jnp.int32)
    blocks_e = (cnt + MOE_ROWS - 1) // MOE_ROWS
    end_blk = jnp.cumsum(blocks_e)
    start_blk = end_blk - blocks_e
    starts = (start_blk * MOE_ROWS).astype(jnp.int32)
    blk = jnp.arange(nb, dtype=jnp.int32)
    block_e = jnp.minimum(jnp.sum(end_blk[None, :] <= blk[:, None], axis=1), N_EXPERTS - 1).astype(jnp.int32)
    mine = block_e[:, None] == jnp.arange(N_EXPERTS, dtype=jnp.int32)[None, :]
    cnt_b = jnp.sum(jnp.where(mine, cnt[None, :], 0), axis=1)
    start_b = jnp.sum(jnp.where(mine, start_blk[None, :], 0), axis=1)
    block_valid = jnp.clip(cnt_b - (blk - start_b) * MOE_ROWS, 0, MOE_ROWS).astype(jnp.int32)
    ids = jnp.arange(N_EXPERTS, dtype=jnp.int32)
    later = (ids[None, :] > ids[:, None]) & (cnt[None, :] > 0)
    next_e = jnp.min(jnp.where(later, ids[None, :], N_EXPERTS), axis=1)
    next_e = jnp.where(next_e == N_EXPERTS, -1, next_e)
    block_next = jnp.sum(jnp.where(mine, next_e[None, :], 0), axis=1).astype(jnp.int32)
    experts = jnp.arange(N_EXPERTS, dtype=jnp.int32)[None, :, None]
    dest = jnp.sum(jnp.where(ri[0:2, None, :] == experts, starts[None, :, None], 0), axis=1) + ri[2:4]
    dest = jnp.concatenate([dest, jnp.zeros((MOD_ROWS - 2, m), jnp.int32)], axis=0)
    xs = _dispatch(h2, dest, p_rows)
    yb = _experts(xs, block_e, block_valid, block_next, w_gate, w_up, w_down, layer)
    return [_combine(x1, mod, yb, dest, rf, row0, rows) for row0, rows in splits]


def kernel(x_prompt, x_sample, c, cache_k_a, cache_v_a, cache_k_b, cache_v_b, c_ctx, w_mod, b_mod, norm1_g, norm2_g, w_qkv_a, w_o_a, q_norm_a, k_norm_a, rpb_a, w_qkv_b, w_o_b, q_norm_b, k_norm_b, w_router_group, b_router_group, w_router_expert, b_router_expert, w_gate, w_up, w_down):
    batch, seq, d = x_prompt.shape
    dec_batch, dec_seq, _ = x_sample.shape
    depth = w_mod.shape[0]
    mp = batch * seq
    ms = dec_batch * dec_seq
    assert mp == GROUP_ROWS and dec_seq == GROUP_ROWS and d == D_MODEL
    x = jnp.concatenate([x_prompt.reshape(mp, d), x_sample.reshape(ms, d)], axis=0)

    cond = jnp.concatenate([c_ctx[None], c, jnp.zeros((MOD_ROWS - 1 - dec_batch, d), _F32)], axis=0)
    mod_all = _modulation(cond, w_mod, b_mod)
    rope_tables = _rope_tables(dec_seq)

    w_router = jnp.concatenate([w_router_group, w_router_expert], axis=-1)
    n_route = w_router.shape[-1]
    w_router = jnp.pad(w_router, ((0, 0), (0, 0), (0, LANES - n_route))).astype(_BF16)
    b_router = jnp.pad(jnp.concatenate([b_router_group, b_router_expert], axis=-1),
                       ((0, 0), (0, LANES - n_route)))

    w_qkv_a, w_o_a, w_qkv_b, w_o_b = (_to_bf16(w) for w in (w_qkv_a, w_o_a, w_qkv_b, w_o_b))

    new_kv = [[], [], [], []]
    for i in range(depth):
        j = i // 2
        mod = mod_all[i].reshape(MOD_ROWS, 1, 6 * d)
        g1 = norm1_g[i].reshape(1, d)
        g2 = norm2_g[i].reshape(1, d)
        if i % 2 == 0:
            n_kv, w_qkv, w_o, qn, kn = N_HEADS, w_qkv_a, w_o_a, q_norm_a[j], k_norm_a[j]
            cache_k, cache_v, rope = cache_k_a[:, j], cache_v_a[:, j], None
        else:
            n_kv, w_qkv, w_o, qn, kn = N_KV_B, w_qkv_b, w_o_b, q_norm_b[j], k_norm_b[j]
            cache_k, cache_v, rope = cache_k_b[:, j], cache_v_b[:, j], rope_tables
        kv_dim = n_kv * HEAD_DIM
        qn, kn = qn.reshape(1, HEAD_DIM), kn.reshape(1, HEAD_DIM)
        qkv_p, k_p, v_p = _qkv(x, mod, g1, w_qkv, j, qn, kn, n_kv, row0=0, m=mp, rope_tables=None, emit_kv=True)
        (qkv_s,) = _qkv(x, mod, g1, w_qkv, j, qn, kn, n_kv, row0=mp, m=ms, rope_tables=rope, emit_kv=False)
        new_kv[2 * (i % 2)].append(k_p.reshape(batch, seq, n_kv, HEAD_DIM))
        new_kv[2 * (i % 2) + 1].append(v_p.reshape(batch, seq, n_kv, HEAD_DIM))

        attn = _context_attention(qkv_p, n_kv, seq, mp + ms)
        k_ctx = cache_k.reshape(dec_batch, -1, kv_dim).astype(_BF16)
        v_ctx = cache_v.reshape(dec_batch, -1, kv_dim).astype(_BF16)
        if i % 2 == 0:
            attn = _neighbourhood_attention(attn, qkv_s, k_ctx, v_ctx, _na_bias_table(rpb_a[j]), mp)
        else:
            attn = _gqa_attention(attn, qkv_s, k_ctx, v_ctx, mp)

        x1, h2, ri, rf, counts = _proj_router(attn, x, mod, g2, w_o, j, w_router[i],
                                              b_router[i].reshape(1, LANES))
        last = i == depth - 1
        outs = _moe(x1, h2, ri, rf, counts, mod, w_gate, w_up, w_down, i,
                    [(0, mp), (mp, ms)] if last else [(0, mp + ms)])
        x = outs[0]

    return (outs[0].reshape(batch, seq, d), outs[1].reshape(dec_batch, dec_seq, d),
            jnp.stack(new_kv[0], axis=1), jnp.stack(new_kv[1], axis=1),
            jnp.stack(new_kv[2], axis=1), jnp.stack(new_kv[3], axis=1))
```

```python
import functools
import math

import numpy as np

import jax
import jax.numpy as jnp
from jax import lax
from jax.experimental import pallas as pl
from jax.experimental.pallas import tpu as pltpu

D_MODEL = 2048
HEAD_DIM = 128
N_HEADS = D_MODEL // HEAD_DIM
N_KV_B = 4
GRID_W = 64
WIN_R = 8
WIN_C = 16
ROPE_THETA = 10000.0
N_GROUPS = 4
EXPERTS_PER_GROUP = 8
N_EXPERTS = N_GROUPS * EXPERTS_PER_GROUP
D_EXPERT = 768
EPS = 1e-6
NEG_INF = -1e30

LANES = 128
MOD_ROWS = 8
GROUP_ROWS = 4096
MOE_ROWS = 256
NA_Q_ROWS = 8
NA_K_ROWS = 16
NA_PAIRS = 30
NA_TABLE = 3 * NA_PAIRS + 1
NA_HEADS = 4
VMEM_LIMIT = 56 * 1024 * 1024
LOG2_E = math.log2(math.e)
Q_SCALE = HEAD_DIM ** -0.5 * LOG2_E

_BF16 = jnp.bfloat16
_F32 = jnp.float32


def _params(sem, vmem=VMEM_LIMIT):
    return pltpu.CompilerParams(dimension_semantics=sem, vmem_limit_bytes=vmem)


def _dot(a, b):
    return jnp.dot(a, b, preferred_element_type=_F32)


def _dot_nt(a, b):
    return lax.dot_general(a, b, (((1,), (1,)), ((), ())), preferred_element_type=_F32)


def _silu(x):
    return x / (1.0 + jnp.exp(-x))


def _cast_kernel(w_ref, o_ref):
    o_ref[...] = w_ref[...].astype(o_ref.dtype)


def _to_bf16(w):
    cols = w.shape[-1]
    w2 = w.reshape(-1, cols)
    rows = w2.shape[0]
    rb = max(8, min(rows, (4 * 1024 * 1024) // (4 * cols)) // 8 * 8)
    while rows % rb:
        rb -= 8
    out = pl.pallas_call(
        _cast_kernel,
        grid=(rows // rb,),
        in_specs=[pl.BlockSpec((rb, cols), lambda i: (i, 0))],
        out_specs=pl.BlockSpec((rb, cols), lambda i: (i, 0)),
        out_shape=jax.ShapeDtypeStruct((rows, cols), _BF16),
        compiler_params=_params(("arbitrary",)),
        name="to_bf16",
    )(w2)
    return out.reshape(w.shape)


def _mod_kernel(cond_ref, w_ref, b_ref, o_ref):
    s = _silu(cond_ref[...]).astype(_BF16)
    o_ref[...] = _dot(s, w_ref[...].astype(_BF16)) + b_ref[...]


def _modulation(cond, w_mod, b_mod):
    depth, d, n = w_mod.shape
    tn = 1024
    return pl.pallas_call(
        _mod_kernel,
        grid=(depth, n // tn),
        in_specs=[
            pl.BlockSpec((MOD_ROWS, d), lambda l, j: (0, 0)),
            pl.BlockSpec((None, d, tn), lambda l, j: (l, 0, j)),
            pl.BlockSpec((None, 1, tn), lambda l, j: (l, 0, j)),
        ],
        out_specs=pl.BlockSpec((None, MOD_ROWS, tn), lambda l, j: (l, 0, j)),
        out_shape=jax.ShapeDtypeStruct((depth, MOD_ROWS, n), _F32),
        compiler_params=_params(("arbitrary", "arbitrary")),
        name="modulation",
    )(cond, w_mod, b_mod.reshape(depth, 1, n))


def _mod_slices(mod_ref, which):
    return mod_ref[0, :, which * D_MODEL:(which + 1) * D_MODEL]


def _norm_modulate(x, g, shift, scale):
    y = x * lax.rsqrt(jnp.mean(x * x, axis=-1, keepdims=True) + EPS)
    return (y * g) * (1.0 + scale) + shift


def _head_norm(a, g):
    return a * lax.rsqrt(jnp.mean(a * a, axis=-1, keepdims=True) + EPS) * g


def _rope_partner_matrix():
    j = np.arange(HEAD_DIM)
    partner = np.where((j % 64) < 32, j + 32, j - 32)
    p = np.zeros((HEAD_DIM, HEAD_DIM), np.float32)
    p[partner, j] = 1.0
    return jnp.asarray(p, _BF16)


def _rope(a, cos, sin, partner_ref):
    return a * cos + _dot(a.astype(_BF16), partner_ref[...]) * sin


def _qkv_kernel(*refs, n_q, n_k, heads_per_tile, rope, emit_kv):
    x_ref, mod_ref, g_ref, w_ref, qn_ref, kn_ref = refs[:6]
    refs = refs[6:]
    if rope:
        cos_ref, sin_ref, partner_ref = refs[:3]
        refs = refs[3:]
    o_ref = refs[0]
    k32_ref, v32_ref = refs[1:3] if emit_kv else (None, None)
    h_scr = refs[-1]
    j = pl.program_id(1)

    @pl.when(j == 0)
    def _():
        h = _norm_modulate(x_ref[...], g_ref[...], _mod_slices(mod_ref, 0), _mod_slices(mod_ref, 1))
        h_scr[...] = h.astype(_BF16)

    acc = _dot(h_scr[...], w_ref[...])

    def heads(gain_ref, scale):
        gain = gain_ref[...] * scale
        outs = []
        for hh in range(heads_per_tile):
            a = _head_norm(acc[:, hh * HEAD_DIM:(hh + 1) * HEAD_DIM], gain)
            if rope:
                a = _rope(a, cos_ref[...], sin_ref[...], partner_ref)
            outs.append(a)
        return jnp.concatenate(outs, axis=1)

    @pl.when(j < n_q)
    def _():
        o_ref[...] = heads(qn_ref, Q_SCALE).astype(_BF16)

    @pl.when((j >= n_q) & (j < n_q + n_k))
    def _():
        k = heads(kn_ref, 1.0)
        o_ref[...] = k.astype(_BF16)
        if emit_kv:
            k32_ref[...] = k

    @pl.when(j >= n_q + n_k)
    def _():
        o_ref[...] = acc.astype(_BF16)
        if emit_kv:
            v32_ref[...] = acc


def _qkv(x, mod, g, w, layer, qn, kn, n_kv, *, row0, m, rope_tables, emit_kv):
    d = x.shape[1]
    n = w.shape[-1]
    tm, tn = 1024, 512
    heads_per_tile = tn // HEAD_DIM
    n_q = D_MODEL // tn
    n_k = n_kv * HEAD_DIM // tn
    rope = rope_tables is not None
    blocks_per_group = GROUP_ROWS // tm
    blk0 = row0 // tm

    def group(i):
        return (blk0 + i) // blocks_per_group

    in_specs = [
        pl.BlockSpec((tm, d), lambda i, j: (blk0 + i, 0)),
        pl.BlockSpec((1, 1, 6 * d), lambda i, j: (group(i), 0, 0)),
        pl.BlockSpec((1, d), lambda i, j: (0, 0)),
        pl.BlockSpec((None, d, tn), lambda i, j: (layer, 0, j)),
        pl.BlockSpec((1, HEAD_DIM), lambda i, j: (0, 0)),
        pl.BlockSpec((1, HEAD_DIM), lambda i, j: (0, 0)),
    ]
    args = [x, mod, g, w, qn, kn]
    if rope:
        pos = lambda i, j: (i % blocks_per_group, 0)
        in_specs += [pl.BlockSpec((tm, HEAD_DIM), pos), pl.BlockSpec((tm, HEAD_DIM), pos),
                     pl.BlockSpec((HEAD_DIM, HEAD_DIM), lambda i, j: (0, 0))]
        args += list(rope_tables) + [_rope_partner_matrix()]
    out_specs = [pl.BlockSpec((tm, tn), lambda i, j: (i, j))]
    out_shape = [jax.ShapeDtypeStruct((m, n), _BF16)]
    if emit_kv:
        out_specs += [pl.BlockSpec((tm, tn), lambda i, j: (i, jnp.clip(j - n_q, 0, n_k - 1))),
                      pl.BlockSpec((tm, tn), lambda i, j: (i, jnp.clip(j - n_q - n_k, 0, n_k - 1)))]
        out_shape += [jax.ShapeDtypeStruct((m, n_k * tn), _F32)] * 2
    return pl.pallas_call(
        functools.partial(_qkv_kernel, n_q=n_q, n_k=n_k, heads_per_tile=heads_per_tile,
                          rope=rope, emit_kv=emit_kv),
        grid=(m // tm, n // tn),
        in_specs=in_specs,
        out_specs=out_specs,
        out_shape=out_shape,
        scratch_shapes=[pltpu.VMEM((tm, d), _BF16)],
        compiler_params=_params(("arbitrary", "arbitrary")),
        name="qkv_rope" if rope else "qkv",
    )(*args)


def _rope_tables(n_tokens):
    t = jnp.arange(n_tokens)
    quarter = HEAD_DIM // 4
    inv = ROPE_THETA ** (-jnp.arange(quarter, dtype=_F32) / quarter)
    ang_r = (t // GRID_W).astype(_F32)[:, None] * inv
    ang_c = (t % GRID_W).astype(_F32)[:, None] * inv
    cr, sr, cc, sc = jnp.cos(ang_r), jnp.sin(ang_r), jnp.cos(ang_c), jnp.sin(ang_c)
    return (jnp.concatenate([cr, cr, cc, cc], axis=1), jnp.concatenate([-sr, sr, -sc, sc], axis=1))


def _softmax_pv(scores, values):
    m = functools.reduce(jnp.maximum, [jnp.max(s, axis=-1, keepdims=True) for s in scores])
    ps = [jnp.exp2(s - m) for s in scores]
    l = functools.reduce(jnp.add, [jnp.sum(p, axis=-1, keepdims=True) for p in ps])
    o = functools.reduce(jnp.add, [_dot(p.astype(_BF16), v) for p, v in zip(ps, values)])
    return o / l


def _ctx_attn_kernel(q_ref, k_ref, v_ref, o_ref, *, group):
    for h in range(N_HEADS):
        kv = h // group
        q = q_ref[:, h * HEAD_DIM:(h + 1) * HEAD_DIM]
        k = k_ref[:, kv * HEAD_DIM:(kv + 1) * HEAD_DIM]
        v = v_ref[:, kv * HEAD_DIM:(kv + 1) * HEAD_DIM]
        o = _softmax_pv([_dot_nt(q, k)], [v])
        o_ref[:, h * HEAD_DIM:(h + 1) * HEAD_DIM] = o.astype(_BF16)


def _context_attention(qkv, n_kv, seq, n_rows_total):
    m = qkv.shape[0]
    kv_dim = n_kv * HEAD_DIM
    k_blk = D_MODEL // kv_dim
    return pl.pallas_call(
        functools.partial(_ctx_attn_kernel, group=N_HEADS // n_kv),
        grid=(m // seq,),
        in_specs=[
            pl.BlockSpec((seq, D_MODEL), lambda b: (b, 0)),
            pl.BlockSpec((seq, kv_dim), lambda b: (b, k_blk)),
            pl.BlockSpec((seq, kv_dim), lambda b: (b, k_blk + 1)),
        ],
        out_specs=pl.BlockSpec((seq, D_MODEL), lambda b: (b, 0)),
        out_shape=jax.ShapeDtypeStruct((n_rows_total, D_MODEL), _BF16),
        compiler_params=_params(("arbitrary",)),
        name="context_attention",
    )(qkv, qkv, qkv)


def _na_table_kernel(rpb_ref, o_ref):
    h = pl.program_id(0)
    c = lax.broadcasted_iota(jnp.int32, (GRID_W, GRID_W), 0)
    kc = lax.broadcasted_iota(jnp.int32, (GRID_W, GRID_W), 1)
    start = jnp.clip(c - WIN_C // 2, 0, GRID_W - WIN_C)
    col_ok = (kc >= start) & (kc < start + WIN_C)
    rel = kc - c + WIN_C - 1
    n_dc = 2 * WIN_C - 1
    masked = jnp.full((GRID_W, GRID_W), NEG_INF, _F32)
    tiles = []
    for d in range(-8, NA_PAIRS - 8 + 1):
        if 0 <= d < 2 * WIN_R - 1:
            t = jnp.zeros((GRID_W, GRID_W), _F32)
            for jj in range(n_dc):
                t = jnp.where(rel == jj, rpb_ref[h, d * n_dc + jj] * LOG2_E, t)
            tiles.append(jnp.where(col_ok, t, NEG_INF))
        else:
            tiles.append(masked)
    for i in range(NA_PAIRS):
        o_ref[0, i] = jnp.concatenate([tiles[i], tiles[i + 1]], axis=1)
        o_ref[0, NA_PAIRS + i] = jnp.concatenate([tiles[i], masked], axis=1)
        o_ref[0, 2 * NA_PAIRS + i] = jnp.concatenate([masked, tiles[i + 1]], axis=1)
    o_ref[0, 3 * NA_PAIRS] = jnp.concatenate([masked, masked], axis=1)


def _na_bias_table(rpb):
    h = rpb.shape[0]
    return pl.pallas_call(
        _na_table_kernel,
        grid=(h,),
        in_specs=[pl.BlockSpec(memory_space=pltpu.SMEM)],
        out_specs=pl.BlockSpec((1, NA_TABLE, GRID_W, 2 * GRID_W), lambda i: (i, 0, 0, 0)),
        out_shape=jax.ShapeDtypeStruct((h, NA_TABLE, GRID_W, 2 * GRID_W), _F32),
        compiler_params=_params(("arbitrary",)),
        name="na_bias_table",
    )(rpb.reshape(h, -1))


def _na_tile_plan(rows):
    nblk = rows // NA_Q_ROWS
    plan = np.zeros((nblk, NA_Q_ROWS * (NA_K_ROWS // 2)), np.int32)
    for blk in range(nblk):
        rs = blk * NA_Q_ROWS
        ks = min(max(rs - WIN_R // 2, 0), rows - NA_K_ROWS)
        for qi in range(NA_Q_ROWS):
            r0 = min(max(rs + qi - WIN_R // 2, 0), rows - WIN_R)
            for p in range(NA_K_ROWS // 2):
                left = r0 <= ks + 2 * p < r0 + WIN_R
                right = r0 <= ks + 2 * p + 1 < r0 + WIN_R
                pair = ks - rs + 2 * WIN_R - 1 - qi + 2 * p
                if left and right:
                    entry = pair
                elif left:
                    entry = NA_PAIRS + pair
                elif right:
                    entry = 2 * NA_PAIRS + pair
                else:
                    entry = 3 * NA_PAIRS
                plan[blk, qi * (NA_K_ROWS // 2) + p] = entry
    return plan


def _na_attn_kernel(plan_ref, alias_ref, q_ref, k_ref, v_ref, kc_ref, vc_ref, tab_ref, o_ref, *, rows):
    del alias_ref
    blk = pl.program_id(2)
    rs = blk * NA_Q_ROWS
    ks = jnp.clip(rs - WIN_R // 2, 0, rows - NA_K_ROWS)
    k0 = pl.multiple_of(ks * GRID_W, 4 * GRID_W)
    n_keys = NA_K_ROWS * GRID_W
    pairs = NA_K_ROWS // 2
    for hh in range(NA_HEADS):
        cols = slice(hh * HEAD_DIM, (hh + 1) * HEAD_DIM)
        q = q_ref[:, cols]
        k = k_ref[pl.ds(k0, n_keys), cols]
        v = v_ref[pl.ds(k0, n_keys), cols]
        bias = jnp.concatenate(
            [jnp.concatenate([tab_ref[hh, plan_ref[blk, qi * pairs + p]] for p in range(pairs)], axis=1)
             for qi in range(NA_Q_ROWS)], axis=0)
        s = _dot_nt(q, k) + bias
        s_ctx = _dot_nt(q, kc_ref[0, :, cols])
        o_ref[:, cols] = _softmax_pv([s, s_ctx], [v, vc_ref[0, :, cols]]).astype(_BF16)


def _neighbourhood_attention(attn, qkv, k_ctx, v_ctx, table, row0):
    b, l, _ = k_ctx.shape
    t = qkv.shape[0] // b
    rows = t // GRID_W
    tq = NA_Q_ROWS * GRID_W
    nblk = t // tq
    w = NA_HEADS * HEAD_DIM
    hp = N_HEADS // NA_HEADS
    out_blk0 = row0 // tq
    return pl.pallas_call(
        functools.partial(_na_attn_kernel, rows=rows),
        grid_spec=pltpu.PrefetchScalarGridSpec(
            num_scalar_prefetch=1,
            grid=(b, hp, nblk),
            in_specs=[
                pl.BlockSpec(memory_space=pl.ANY),
                pl.BlockSpec((tq, w), lambda bi, h, i, plan: (bi * nblk + i, h)),
                pl.BlockSpec((t, w), lambda bi, h, i, plan: (bi, hp + h)),
                pl.BlockSpec((t, w), lambda bi, h, i, plan: (bi, 2 * hp + h)),
                pl.BlockSpec((1, l, w), lambda bi, h, i, plan: (bi, 0, h)),
                pl.BlockSpec((1, l, w), lambda bi, h, i, plan: (bi, 0, h)),
                pl.BlockSpec((NA_HEADS, NA_TABLE, GRID_W, 2 * GRID_W), lambda bi, h, i, plan: (h, 0, 0, 0)),
            ],
            out_specs=pl.BlockSpec((tq, w), lambda bi, h, i, plan: (out_blk0 + bi * nblk + i, h)),
        ),
        out_shape=jax.ShapeDtypeStruct(attn.shape, attn.dtype),
        input_output_aliases={1: 0},
        compiler_params=_params(("arbitrary", "arbitrary", "arbitrary")),
        name="neighbourhood_attention",
    )(jnp.asarray(_na_tile_plan(rows)), attn, qkv, qkv, qkv, k_ctx, v_ctx, table)


def _gqa_attn_kernel(alias_ref, q_ref, k_ref, v_ref, kc_ref, vc_ref, o_ref, *, group, chunk):
    del alias_ref
    tq = q_ref.shape[0]
    q = jnp.concatenate([q_ref[:, g * HEAD_DIM:(g + 1) * HEAD_DIM] for g in range(group)], axis=0)
    t = k_ref.shape[0]
    m = l = acc = None
    for c in range(t // chunk + 1):
        if c < t // chunk:
            k = k_ref[c * chunk:(c + 1) * chunk, :]
            v = v_ref[c * chunk:(c + 1) * chunk, :]
        else:
            k, v = kc_ref[0], vc_ref[0]
        s = _dot_nt(q, k)
        m_c = jnp.max(s, axis=-1, keepdims=True)
        if m is None:
            m = m_c
            p = jnp.exp2(s - m)
            l = jnp.sum(p, axis=-1, keepdims=True)
            acc = _dot(p.astype(_BF16), v)
        else:
            m_new = jnp.maximum(m, m_c)
            alpha = jnp.exp2(m - m_new)
            p = jnp.exp2(s - m_new)
            l = alpha * l + jnp.sum(p, axis=-1, keepdims=True)
            acc = alpha * acc + _dot(p.astype(_BF16), v)
            m = m_new
    o = acc / l
    for g in range(group):
        o_ref[:, g * HEAD_DIM:(g + 1) * HEAD_DIM] = o[g * tq:(g + 1) * tq].astype(_BF16)


def _gqa_attention(attn, qkv, k_ctx, v_ctx, row0):
    b, l, kv_dim = k_ctx.shape
    n_kv = kv_dim // HEAD_DIM
    group = N_HEADS // n_kv
    t = qkv.shape[0] // b
    tq = 512
    nblk = t // tq
    out_blk0 = row0 // tq
    return pl.pallas_call(
        functools.partial(_gqa_attn_kernel, group=group, chunk=1024),
        grid=(b, n_kv, nblk),
        in_specs=[
            pl.BlockSpec(memory_space=pl.ANY),
            pl.BlockSpec((tq, group * HEAD_DIM), lambda bi, h, i: (bi * nblk + i, h)),
            pl.BlockSpec((t, HEAD_DIM), lambda bi, h, i: (bi, N_HEADS + h)),
            pl.BlockSpec((t, HEAD_DIM), lambda bi, h, i: (bi, N_HEADS + n_kv + h)),
            pl.BlockSpec((1, l, HEAD_DIM), lambda bi, h, i: (bi, 0, h)),
            pl.BlockSpec((1, l, HEAD_DIM), lambda bi, h, i: (bi, 0, h)),
        ],
        out_specs=pl.BlockSpec((tq, group * HEAD_DIM), lambda bi, h, i: (out_blk0 + bi * nblk + i, h)),
        out_shape=jax.ShapeDtypeStruct(attn.shape, attn.dtype),
        input_output_aliases={0: 0},
        compiler_params=_params(("arbitrary", "arbitrary", "arbitrary")),
        name="gqa_attention",
    )(attn, qkv, qkv, qkv, k_ctx, v_ctx)


def _pack_bf16_halves(h):
    half = h.shape[1] // 2

    def rounded(v):
        b = lax.bitcast_convert_type(v, jnp.int32)
        return b + 0x7FFF + ((b >> 16) & 1)

    lo, hi = rounded(h[:, :half]), rounded(h[:, half:])
    return (hi & -65536) | ((lo >> 16) & 0xFFFF)


def _unpack_bf16_halves(p):
    lo = lax.bitcast_convert_type(p << 16, _F32)
    hi = lax.bitcast_convert_type(p & -65536, _F32)
    return jnp.concatenate([lo, hi], axis=1).astype(_BF16)


def _route(logits, lane):
    def masked_softmax(mask):
        z = jnp.where(mask, logits, -jnp.inf)
        e = jnp.exp(z - jnp.max(z, axis=-1, keepdims=True))
        return e / jnp.sum(e, axis=-1, keepdims=True)

    def top1(p, mask):
        best = jnp.max(jnp.where(mask, p, -1.0), axis=-1, keepdims=True)
        idx = jnp.min(jnp.where(mask & (p == best), lane, float(LANES)), axis=-1, keepdims=True)
        return best, idx

    g_mask = lane < N_GROUPS
    g_p, g_idx = top1(masked_softmax(g_mask), g_mask)
    lo = N_GROUPS + g_idx * EXPERTS_PER_GROUP
    e_mask = (lane >= lo) & (lane < lo + EXPERTS_PER_GROUP)
    e_prob = masked_softmax(e_mask)
    p1, i1 = top1(e_prob, e_mask)
    p2, i2 = top1(e_prob, e_mask & (lane != i1))
    return i1 - N_GROUPS, i2 - N_GROUPS, g_p * p1 / (p1 + p2), g_p * p2 / (p1 + p2)


def _proj_router_kernel(a_ref, x_ref, mod_ref, g_ref, wo_ref, wr_ref, br_ref,
                        x1_ref, h_ref, ri_ref, rf_ref, cnt_ref, carry):
    i = pl.program_id(0)
    tm = a_ref.shape[0]

    @pl.when(i == 0)
    def _():
        carry[...] = jnp.zeros_like(carry)

    o = _dot(a_ref[...], wo_ref[...])
    x1 = x_ref[...] + _mod_slices(mod_ref, 2) * o
    x1_ref[...] = x1
    h = _norm_modulate(x1, g_ref[...], _mod_slices(mod_ref, 3), _mod_slices(mod_ref, 4))
    h_ref[...] = _pack_bf16_halves(h)

    logits = _dot(h.astype(_BF16), wr_ref[...]) + br_ref[...]
    lane = lax.broadcasted_iota(jnp.int32, logits.shape, 1).astype(_F32)
    e1, e2, gate1, gate2 = _route(logits, lane)

    hit1, hit2 = lane == e1, lane == e2
    onehot = jnp.where(hit1 | hit2, 1.0, 0.0)
    row = lax.broadcasted_iota(jnp.int32, (tm, tm), 0)
    col = lax.broadcasted_iota(jnp.int32, (tm, tm), 1)
    lower = jnp.where(row > col, 1.0, 0.0).astype(_BF16)
    before = _dot(lower, onehot.astype(_BF16)) + carry[...]
    rank1 = jnp.sum(jnp.where(hit1, before, 0.0), axis=-1, keepdims=True)
    rank2 = jnp.sum(jnp.where(hit2, before, 0.0), axis=-1, keepdims=True)
    carry[...] = carry[...] + jnp.sum(onehot, axis=0, keepdims=True)

    record = jnp.where(lane == 0, e1, jnp.where(lane == 1, e2,
                       jnp.where(lane == 2, rank1, jnp.where(lane == 3, rank2, 0.0))))
    ri_ref[...] = record.T[:MOD_ROWS].astype(jnp.int32)
    rf_ref[...] = jnp.where(lane == 0, gate1, jnp.where(lane == 1, gate2, 0.0))
    cnt_ref[...] = jnp.broadcast_to(carry[...], cnt_ref.shape)


def _proj_router(attn, x, mod, g, w_o, layer, w_r, b_r):
    m, d = x.shape
    tm = 512
    blocks_per_group = GROUP_ROWS // tm
    row = lambda i: (i, 0)
    const = lambda i: (0, 0)
    return pl.pallas_call(
        _proj_router_kernel,
        grid=(m // tm,),
        in_specs=[
            pl.BlockSpec((tm, d), row),
            pl.BlockSpec((tm, d), row),
            pl.BlockSpec((1, 1, 6 * d), lambda i: (i // blocks_per_group, 0, 0)),
            pl.BlockSpec((1, d), const),
            pl.BlockSpec((None, d, d), lambda i: (layer, 0, 0)),
            pl.BlockSpec((d, LANES), const),
            pl.BlockSpec((1, LANES), const),
        ],
        out_specs=[
            pl.BlockSpec((tm, d), row),
            pl.BlockSpec((tm, d // 2), row),
            pl.BlockSpec((MOD_ROWS, tm), lambda i: (0, i)),
            pl.BlockSpec((tm, LANES), row),
            pl.BlockSpec((MOD_ROWS, LANES), const),
        ],
        out_shape=[
            jax.ShapeDtypeStruct((m, d), _F32),
            jax.ShapeDtypeStruct((m, d // 2), jnp.int32),
            jax.ShapeDtypeStruct((MOD_ROWS, m), jnp.int32),
            jax.ShapeDtypeStruct((m, LANES), _F32),
            jax.ShapeDtypeStruct((MOD_ROWS, LANES), _F32),
        ],
        scratch_shapes=[pltpu.VMEM((1, LANES), _F32)],
        compiler_params=_params(("arbitrary",)),
        name="proj_router",
    )(attn, x, mod, g, w_o, w_r, b_r)


def _record_copy(dest_hbm, idx, isem, blk, slot, tm):
    return pltpu.make_async_copy(dest_hbm.at[:, pl.ds(blk * tm, tm)], idx.at[slot], isem.at[slot])


def _start_row_copies(idx, slot, tm, make_copy):
    for chunk in range(tm // LANES):
        def body(j, carry, chunk=chunk):
            base = chunk * LANES + pl.multiple_of(j * 8, 8)
            for u in range(8):
                for k in range(2):
                    make_copy(base + u, k, idx[slot, k, base + u]).start(priority=k)
            return carry
        lax.fori_loop(0, LANES // 8, body, 0)


def _dispatch_kernel(dest_hbm, h_ref, xs_hbm, idx, isem, sem):
    i = pl.program_id(0)
    n = pl.num_programs(0)
    tm = h_ref.shape[0]

    @pl.when(i == 0)
    def _():
        _record_copy(dest_hbm, idx, isem, 0, 0, tm).start()

    @pl.when(i + 1 < n)
    def _():
        _record_copy(dest_hbm, idx, isem, i + 1, (i + 1) % 2, tm).start()

    _record_copy(dest_hbm, idx, isem, i, i % 2, tm).wait()
    for slot in range(2):
        @pl.when(i % 2 == slot)
        def _(slot=slot):
            _start_row_copies(idx, slot, tm, lambda r, k, d: pltpu.make_async_copy(
                h_ref.at[pl.ds(r, 1)], xs_hbm.at[pl.ds(d, 1)], sem))

    for _ in range(2):
        pltpu.make_async_copy(h_ref, xs_hbm.at[pl.ds(0, tm)], sem).wait()


def _dispatch(h2, dest, p_rows):
    m, d = h2.shape
    tm = 512
    return pl.pallas_call(
        _dispatch_kernel,
        grid=(m // tm,),
        in_specs=[
            pl.BlockSpec(memory_space=pl.ANY),
            pl.BlockSpec((tm, d), lambda i: (i, 0)),
        ],
        out_specs=pl.BlockSpec(memory_space=pl.ANY),
        out_shape=jax.ShapeDtypeStruct((p_rows, d), h2.dtype),
        scratch_shapes=[
            pltpu.SMEM((2, MOD_ROWS, tm), jnp.int32),
            pltpu.SemaphoreType.DMA((2,)),
            pltpu.SemaphoreType.DMA,
        ],
        compiler_params=_params(("arbitrary",)),
        name="dispatch",
    )(dest, h2)


def _expert_kernel(be_ref, valid_ref, next_ref, x_ref, wg_hbm, wu_hbm, wd_hbm, o_ref,
                   stage_g, stage_u, stage_d, wg, wu, wd, sem, *, layer):
    b = pl.program_id(0)
    valid = valid_ref[b]
    expert = be_ref[b]

    def fetch(e):
        return [pltpu.make_async_copy(src.at[layer, e], dst, sem)
                for src, dst in ((wg_hbm, stage_g), (wu_hbm, stage_u), (wd_hbm, stage_d))]

    @pl.when(b == 0)
    def _():
        for copy in fetch(expert):
            copy.start()

    first = (b == 0) | (expert != be_ref[jnp.maximum(b - 1, 0)])

    @pl.when(first & (valid > 0))
    def _():
        for copy in fetch(expert):
            copy.wait()
        rows = 256
        for stage, dst in ((stage_g, wg), (stage_u, wu), (stage_d, wd)):
            for r in range(0, stage.shape[0], rows):
                dst[r:r + rows, :] = stage[r:r + rows, :].astype(_BF16)

        @pl.when(next_ref[b] >= 0)
        def _():
            for copy in fetch(next_ref[b]):
                copy.start()

    @pl.when(valid > 0)
    def _():
        row = lax.broadcasted_iota(jnp.int32, x_ref.shape, 0)
        x = _unpack_bf16_halves(jnp.where(row < valid, x_ref[...], 0))
        hidden = _silu(_dot(x, wg[...])) * _dot(x, wu[...])
        o_ref[...] = _dot(hidden.astype(_BF16), wd[...])

    @pl.when(valid <= 0)
    def _():
        o_ref[...] = jnp.zeros_like(o_ref)


def _experts(xs, block_e, block_valid, block_next, w_gate, w_up, w_down, layer):
    p = xs.shape[0]
    d, de = w_gate.shape[-2:]
    nb = p // MOE_ROWS
    return pl.pallas_call(
        functools.partial(_expert_kernel, layer=layer),
        grid_spec=pltpu.PrefetchScalarGridSpec(
            num_scalar_prefetch=3,
            grid=(nb,),
            in_specs=[
                pl.BlockSpec((MOE_ROWS, d // 2), lambda b, be, nv, nx: (b, 0)),
                pl.BlockSpec(memory_space=pl.ANY),
                pl.BlockSpec(memory_space=pl.ANY),
                pl.BlockSpec(memory_space=pl.ANY),
            ],
            out_specs=pl.BlockSpec((MOE_ROWS, d), lambda b, be, nv, nx: (b, 0)),
            scratch_shapes=[
                pltpu.VMEM((d, de), _F32), pltpu.VMEM((d, de), _F32), pltpu.VMEM((de, d), _F32),
                pltpu.VMEM((d, de), _BF16), pltpu.VMEM((d, de), _BF16), pltpu.VMEM((de, d), _BF16),
                pltpu.SemaphoreType.DMA,
            ],
        ),
        out_shape=jax.ShapeDtypeStruct((p, d), _F32),
        compiler_params=_params(("arbitrary",)),
        name="experts",
    )(block_e, block_valid, block_next, xs, w_gate, w_up, w_down)


def _combine_kernel(dest_hbm, yb_hbm, x_ref, mod_ref, rf_ref, o_ref, idx, buf, isem, sem, *, blk0):
    i = pl.program_id(0)
    n = pl.num_programs(0)
    tm = x_ref.shape[0]

    def records(blk, slot):
        return _record_copy(dest_hbm, idx, isem, blk0 + blk, slot, tm)

    def fetch_rows(slot):
        _start_row_copies(idx, slot, tm, lambda r, k, d: pltpu.make_async_copy(
            yb_hbm.at[pl.ds(d, 1)], buf.at[slot, k, pl.ds(r, 1)], sem.at[slot]))

    @pl.when(i == 0)
    def _():
        records(0, 0).start()

        @pl.when(n > 1)
        def _():
            records(1, 1).start()

        records(0, 0).wait()
        fetch_rows(0)

    @pl.when(i + 2 < n)
    def _():
        records(i + 2, i % 2).start()

    for slot in range(2):
        @pl.when((i + 1 < n) & ((i + 1) % 2 == slot))
        def _(slot=slot):
            records(i + 1, slot).wait()
            fetch_rows(slot)

    slot = i % 2
    for k in range(2):
        pltpu.make_async_copy(yb_hbm.at[pl.ds(0, tm)], buf.at[slot, k], sem.at[slot]).wait()
    gate = rf_ref[...]
    y = buf[slot, 0] * gate[:, 0:1] + buf[slot, 1] * gate[:, 1:2]
    o_ref[...] = x_ref[...] + _mod_slices(mod_ref, 5) * y


def _combine(x1, mod, yb, dest, rf, row0, m):
    d = x1.shape[1]
    tm = 512
    blocks_per_group = GROUP_ROWS // tm
    blk0 = row0 // tm
    row = lambda i: (blk0 + i, 0)
    return pl.pallas_call(
        functools.partial(_combine_kernel, blk0=blk0),
        grid=(m // tm,),
        in_specs=[
            pl.BlockSpec(memory_space=pl.ANY),
            pl.BlockSpec(memory_space=pl.ANY),
            pl.BlockSpec((tm, d), row),
            pl.BlockSpec((1, 1, 6 * d), lambda i: ((blk0 + i) // blocks_per_group, 0, 0)),
            pl.BlockSpec((tm, LANES), row),
        ],
        out_specs=pl.BlockSpec((tm, d), lambda i: (i, 0)),
        out_shape=jax.ShapeDtypeStruct((m, d), _F32),
        scratch_shapes=[
            pltpu.SMEM((2, MOD_ROWS, tm), jnp.int32),
            pltpu.VMEM((2, 2, tm, d), _F32),
            pltpu.SemaphoreType.DMA((2,)),
            pltpu.SemaphoreType.DMA((2,)),
        ],
        compiler_params=_params(("arbitrary",)),
        name="combine",
    )(dest, yb, x1, mod, rf)


def _moe(x1, h2, ri, rf, counts, mod, w_gate, w_up, w_down, layer, splits):
    m = x1.shape[0]
    p_rows = (2 * m + N_EXPERTS * (MOE_ROWS - 1) + MOE_ROWS - 1) // MOE_ROWS * MOE_ROWS
    nb = p_rows // MOE_ROWS
    cnt = counts[0, :N_EXPERTS].astype(jnp.int32)
    blocks_e = (cnt + MOE_ROWS - 1) // MOE_ROWS
    end_blk = jnp.cumsum(blocks_e)
    start_blk = end_blk - blocks_e
    starts = (start_blk * MOE_ROWS).astype(jnp.int32)
    blk = jnp.arange(nb, dtype=jnp.int32)
    block_e = jnp.minimum(jnp.sum(end_blk[None, :] <= blk[:, None], axis=1), N_EXPERTS - 1).astype(jnp.int32)
    mine = block_e[:, None] == jnp.arange(N_EXPERTS, dtype=jnp.int32)[None, :]
    cnt_b = jnp.sum(jnp.where(mine, cnt[None, :], 0), axis=1)
    start_b = jnp.sum(jnp.where(mine, start_blk[None, :], 0), axis=1)
    block_valid = jnp.clip(cnt_b - (blk - start_b) * MOE_ROWS, 0, MOE_ROWS).astype(jnp.int32)
    ids = jnp.arange(N_EXPERTS, dtype=jnp.int32)
    later = (ids[None, :] > ids[:, None]) & (cnt[None, :] > 0)
    next_e = jnp.min(jnp.where(later, ids[None, :], N_EXPERTS), axis=1)
    next_e = jnp.where(next_e == N_EXPERTS, -1, next_e)
    block_next = jnp.sum(jnp.where(mine, next_e[None, :], 0), axis=1).astype(jnp.int32)
    experts = jnp.arange(N_EXPERTS, dtype=jnp.int32)[None, :, None]
    dest = jnp.sum(jnp.where(ri[0:2, None, :] == experts, starts[None, :, None], 0), axis=1) + ri[2:4]
    dest = jnp.concatenate([dest, jnp.zeros((MOD_ROWS - 2, m), jnp.int32)], axis=0)
    xs = _dispatch(h2, dest, p_rows)
    yb = _experts(xs, block_e, block_valid, block_next, w_gate, w_up, w_down, layer)
    return [_combine(x1, mod, yb, dest, rf, row0, rows) for row0, rows in splits]


def kernel(x_prompt, x_sample, c, cache_k_a, cache_v_a, cache_k_b, cache_v_b, c_ctx, w_mod, b_mod, norm1_g, norm2_g, w_qkv_a, w_o_a, q_norm_a, k_norm_a, rpb_a, w_qkv_b, w_o_b, q_norm_b, k_norm_b, w_router_group, b_router_group, w_router_expert, b_router_expert, w_gate, w_up, w_down):
    batch, seq, d = x_prompt.shape
    dec_batch, dec_seq, _ = x_sample.shape
    depth = w_mod.shape[0]
    mp = batch * seq
    ms = dec_batch * dec_seq
    assert mp == GROUP_ROWS and dec_seq == GROUP_ROWS and d == D_MODEL
    x = jnp.concatenate([x_prompt.reshape(mp, d), x_sample.reshape(ms, d)], axis=0)

    cond = jnp.concatenate([c_ctx[None], c, jnp.zeros((MOD_ROWS - 1 - dec_batch, d), _F32)], axis=0)
    mod_all = _modulation(cond, w_mod, b_mod)
    rope_tables = _rope_tables(dec_seq)

    w_router = jnp.concatenate([w_router_group, w_router_expert], axis=-1)
    n_route = w_router.shape[-1]
    w_router = jnp.pad(w_router, ((0, 0), (0, 0), (0, LANES - n_route))).astype(_BF16)
    b_router = jnp.pad(jnp.concatenate([b_router_group, b_router_expert], axis=-1),
                       ((0, 0), (0, LANES - n_route)))

    w_qkv_a, w_o_a, w_qkv_b, w_o_b = (_to_bf16(w) for w in (w_qkv_a, w_o_a, w_qkv_b, w_o_b))

    new_kv = [[], [], [], []]
    for i in range(depth):
        j = i // 2
        mod = mod_all[i].reshape(MOD_ROWS, 1, 6 * d)
        g1 = norm1_g[i].reshape(1, d)
        g2 = norm2_g[i].reshape(1, d)
        if i % 2 == 0:
            n_kv, w_qkv, w_o, qn, kn = N_HEADS, w_qkv_a, w_o_a, q_norm_a[j], k_norm_a[j]
            cache_k, cache_v, rope = cache_k_a[:, j], cache_v_a[:, j], None
        else:
            n_kv, w_qkv, w_o, qn, kn = N_KV_B, w_qkv_b, w_o_b, q_norm_b[j], k_norm_b[j]
            cache_k, cache_v, rope = cache_k_b[:, j], cache_v_b[:, j], rope_tables
        kv_dim = n_kv * HEAD_DIM
        qn, kn = qn.reshape(1, HEAD_DIM), kn.reshape(1, HEAD_DIM)
        qkv_p, k_p, v_p = _qkv(x, mod, g1, w_qkv, j, qn, kn, n_kv, row0=0, m=mp, rope_tables=None, emit_kv=True)
        (qkv_s,) = _qkv(x, mod, g1, w_qkv, j, qn, kn, n_kv, row0=mp, m=ms, rope_tables=rope, emit_kv=False)
        new_kv[2 * (i % 2)].append(k_p.reshape(batch, seq, n_kv, HEAD_DIM))
        new_kv[2 * (i % 2) + 1].append(v_p.reshape(batch, seq, n_kv, HEAD_DIM))

        attn = _context_attention(qkv_p, n_kv, seq, mp + ms)
        k_ctx = cache_k.reshape(dec_batch, -1, kv_dim).astype(_BF16)
        v_ctx = cache_v.reshape(dec_batch, -1, kv_dim).astype(_BF16)
        if i % 2 == 0:
            attn = _neighbourhood_attention(attn, qkv_s, k_ctx, v_ctx, _na_bias_table(rpb_a[j]), mp)
        else:
            attn = _gqa_attention(attn, qkv_s, k_ctx, v_ctx, mp)

        x1, h2, ri, rf, counts = _proj_router(attn, x, mod, g2, w_o, j, w_router[i],
                                              b_router[i].reshape(1, LANES))
        last = i == depth - 1
        outs = _moe(x1, h2, ri, rf, counts, mod, w_gate, w_up, w_down, i,
                    [(0, mp), (mp, ms)] if last else [(0, mp + ms)])
        x = outs[0]

    return (outs[0].reshape(batch, seq, d), outs[1].reshape(dec_batch, dec_seq, d),
            jnp.stack(new_kv[0], axis=1), jnp.stack(new_kv[1], axis=1),
            jnp.stack(new_kv[2], axis=1), jnp.stack(new_kv[3], axis=1))
```

```python
import functools
import math

import numpy as np

import jax
import jax.numpy as jnp
from jax import lax
from jax.experimental import pallas as pl
from jax.experimental.pallas import tpu as pltpu

D_MODEL = 2048
HEAD_DIM = 128
N_HEADS = D_MODEL // HEAD_DIM
N_KV_B = 4
GRID_W = 64
WIN_R = 8
WIN_C = 16
ROPE_THETA = 10000.0
N_GROUPS = 4
EXPERTS_PER_GROUP = 8
N_EXPERTS = N_GROUPS * EXPERTS_PER_GROUP
D_EXPERT = 768
EPS = 1e-6
NEG_INF = -1e30

LANES = 128
MOD_ROWS = 8
GROUP_ROWS = 4096
MOE_ROWS = 256
NA_Q_ROWS = 8
NA_K_ROWS = 16
NA_PAIRS = 30
NA_TABLE = 3 * NA_PAIRS + 1
NA_HEADS = 4
VMEM_LIMIT = 56 * 1024 * 1024
LOG2_E = math.log2(math.e)
Q_SCALE = HEAD_DIM ** -0.5 * LOG2_E

_BF16 = jnp.bfloat16
_F32 = jnp.float32


def _params(sem, vmem=VMEM_LIMIT):
    return pltpu.CompilerParams(dimension_semantics=sem, vmem_limit_bytes=vmem)


def _dot(a, b):
    return jnp.dot(a, b, preferred_element_type=_F32)


def _dot_nt(a, b):
    return lax.dot_general(a, b, (((1,), (1,)), ((), ())), preferred_element_type=_F32)


def _silu(x):
    return x / (1.0 + jnp.exp(-x))


def _cast_kernel(w_ref, o_ref):
    o_ref[...] = w_ref[...].astype(o_ref.dtype)


def _to_bf16(w):
    cols = w.shape[-1]
    w2 = w.reshape(-1, cols)
    rows = w2.shape[0]
    rb = max(8, min(rows, (4 * 1024 * 1024) // (4 * cols)) // 8 * 8)
    while rows % rb:
        rb -= 8
    out = pl.pallas_call(
        _cast_kernel,
        grid=(rows // rb,),
        in_specs=[pl.BlockSpec((rb, cols), lambda i: (i, 0))],
        out_specs=pl.BlockSpec((rb, cols), lambda i: (i, 0)),
        out_shape=jax.ShapeDtypeStruct((rows, cols), _BF16),
        compiler_params=_params(("arbitrary",)),
        name="to_bf16",
    )(w2)
    return out.reshape(w.shape)


def _mod_kernel(cond_ref, w_ref, b_ref, o_ref):
    s = _silu(cond_ref[...]).astype(_BF16)
    o_ref[...] = _dot(s, w_ref[...].astype(_BF16)) + b_ref[...]


def _modulation(cond, w_mod, b_mod):
    depth, d, n = w_mod.shape
    tn = 1024
    return pl.pallas_call(
        _mod_kernel,
        grid=(depth, n // tn),
        in_specs=[
            pl.BlockSpec((MOD_ROWS, d), lambda l, j: (0, 0)),
            pl.BlockSpec((None, d, tn), lambda l, j: (l, 0, j)),
            pl.BlockSpec((None, 1, tn), lambda l, j: (l, 0, j)),
        ],
        out_specs=pl.BlockSpec((None, MOD_ROWS, tn), lambda l, j: (l, 0, j)),
        out_shape=jax.ShapeDtypeStruct((depth, MOD_ROWS, n), _F32),
        compiler_params=_params(("arbitrary", "arbitrary")),
        name="modulation",
    )(cond, w_mod, b_mod.reshape(depth, 1, n))


def _mod_slices(mod_ref, which):
    return mod_ref[0, :, which * D_MODEL:(which + 1) * D_MODEL]


def _norm_modulate(x, g, shift, scale):
    y = x * lax.rsqrt(jnp.mean(x * x, axis=-1, keepdims=True) + EPS)
    return (y * g) * (1.0 + scale) + shift


def _head_norm(a, g):
    return a * lax.rsqrt(jnp.mean(a * a, axis=-1, keepdims=True) + EPS) * g


def _rope_partner_matrix():
    j = np.arange(HEAD_DIM)
    partner = np.where((j % 64) < 32, j + 32, j - 32)
    p = np.zeros((HEAD_DIM, HEAD_DIM), np.float32)
    p[partner, j] = 1.0
    return jnp.asarray(p, _BF16)


def _rope(a, cos, sin, partner_ref):
    return a * cos + _dot(a.astype(_BF16), partner_ref[...]) * sin


def _store_requests(ref, rows):
    seq = ref.shape[1]
    for b in range(ref.shape[0]):
        ref[b] = rows[b * seq:(b + 1) * seq, :]


def _qkv_kernel(*refs, n_q, n_k, heads_per_tile, rope, emit_kv, n_aliased):
    x_ref, mod_ref, g_ref, w_ref, qn_ref, kn_ref = refs[:6]
    refs = refs[6:]
    if rope:
        cos_ref, sin_ref, partner_ref = refs[:3]
        refs = refs[3:]
    refs = refs[n_aliased:]
    o_ref = refs[0]
    k32_ref, v32_ref = refs[1:3] if emit_kv else (None, None)
    h_scr = refs[-1]
    j = pl.program_id(1)

    @pl.when(j == 0)
    def _():
        h = _norm_modulate(x_ref[...], g_ref[...], _mod_slices(mod_ref, 0), _mod_slices(mod_ref, 1))
        h_scr[...] = h.astype(_BF16)

    acc = _dot(h_scr[...], w_ref[...])

    def heads(gain_ref, scale):
        gain = gain_ref[...] * scale
        outs = []
        for hh in range(heads_per_tile):
            a = _head_norm(acc[:, hh * HEAD_DIM:(hh + 1) * HEAD_DIM], gain)
            if rope:
                a = _rope(a, cos_ref[...], sin_ref[...], partner_ref)
            outs.append(a)
        return jnp.concatenate(outs, axis=1)

    @pl.when(j < n_q)
    def _():
        o_ref[...] = heads(qn_ref, Q_SCALE).astype(_BF16)

    @pl.when((j >= n_q) & (j < n_q + n_k))
    def _():
        k = heads(kn_ref, 1.0)
        o_ref[...] = k.astype(_BF16)
        if emit_kv:
            _store_requests(k32_ref, k)

    @pl.when(j >= n_q + n_k)
    def _():
        o_ref[...] = acc.astype(_BF16)
        if emit_kv:
            _store_requests(v32_ref, acc)


def _qkv(x, mod, g, w, layer, qn, kn, n_kv, *, row0, m, group0, rope_tables, kv_out=None):
    d = x.shape[1]
    n = w.shape[-1]
    tm, tn = 1024, 512
    heads_per_tile = tn // HEAD_DIM
    n_q = D_MODEL // tn
    n_k = n_kv * HEAD_DIM // tn
    rope = rope_tables is not None
    emit_kv = kv_out is not None
    blocks_per_group = GROUP_ROWS // tm
    blk0 = row0 // tm

    def group(i):
        return group0 + i // blocks_per_group

    in_specs = [
        pl.BlockSpec((tm, d), lambda i, j: (blk0 + i, 0)),
        pl.BlockSpec((1, 1, 6 * d), lambda i, j: (group(i), 0, 0)),
        pl.BlockSpec((1, d), lambda i, j: (0, 0)),
        pl.BlockSpec((None, d, tn), lambda i, j: (layer, 0, j)),
        pl.BlockSpec((1, HEAD_DIM), lambda i, j: (0, 0)),
        pl.BlockSpec((1, HEAD_DIM), lambda i, j: (0, 0)),
    ]
    args = [x, mod, g, w, qn, kn]
    if rope:
        pos = lambda i, j: (i % blocks_per_group, 0)
        in_specs += [pl.BlockSpec((tm, HEAD_DIM), pos), pl.BlockSpec((tm, HEAD_DIM), pos),
                     pl.BlockSpec((HEAD_DIM, HEAD_DIM), lambda i, j: (0, 0))]
        args += list(rope_tables) + [_rope_partner_matrix()]
    out_specs = [pl.BlockSpec((tm, tn), lambda i, j: (i, j))]
    out_shape = [jax.ShapeDtypeStruct((m, n), _BF16)]
    aliases = {}
    n_aliased = 0
    if emit_kv:
        seq, slots, slot, previous = kv_out
        per_block = tm // seq
        out_specs += [pl.BlockSpec((per_block, seq, tn), lambda i, j: (i, slot, jnp.clip(j - n_q, 0, n_k - 1))),
                      pl.BlockSpec((per_block, seq, tn), lambda i, j: (i, slot, jnp.clip(j - n_q - n_k, 0, n_k - 1)))]
        out_shape += [jax.ShapeDtypeStruct((m // seq, slots * seq, n_k * tn), _F32)] * 2
        if previous is not None:
            n_aliased = 2
            aliases = {len(args): 1, len(args) + 1: 2}
            in_specs += [pl.BlockSpec(memory_space=pl.ANY)] * 2
            args += list(previous)
    return pl.pallas_call(
        functools.partial(_qkv_kernel, n_q=n_q, n_k=n_k, heads_per_tile=heads_per_tile,
                          rope=rope, emit_kv=emit_kv, n_aliased=n_aliased),
        grid=(m // tm, n // tn),
        in_specs=in_specs,
        out_specs=out_specs,
        out_shape=out_shape,
        input_output_aliases=aliases,
        scratch_shapes=[pltpu.VMEM((tm, d), _BF16)],
        compiler_params=_params(("arbitrary", "arbitrary")),
        name="qkv_rope" if rope else "qkv",
    )(*args)


def _rope_tables(n_tokens):
    t = jnp.arange(n_tokens)
    quarter = HEAD_DIM // 4
    inv = ROPE_THETA ** (-jnp.arange(quarter, dtype=_F32) / quarter)
    ang_r = (t // GRID_W).astype(_F32)[:, None] * inv
    ang_c = (t % GRID_W).astype(_F32)[:, None] * inv
    cr, sr, cc, sc = jnp.cos(ang_r), jnp.sin(ang_r), jnp.cos(ang_c), jnp.sin(ang_c)
    return (jnp.concatenate([cr, cr, cc, cc], axis=1), jnp.concatenate([-sr, sr, -sc, sc], axis=1))


def _softmax_pv(scores, values):
    m = functools.reduce(jnp.maximum, [jnp.max(s, axis=-1, keepdims=True) for s in scores])
    ps = [jnp.exp2(s - m) for s in scores]
    l = functools.reduce(jnp.add, [jnp.sum(p, axis=-1, keepdims=True) for p in ps])
    o = functools.reduce(jnp.add, [_dot(p.astype(_BF16), v) for p, v in zip(ps, values)])
    return o / l


def _ctx_attn_kernel(q_ref, k_ref, v_ref, o_ref, *, group):
    for h in range(N_HEADS):
        kv = h // group
        q = q_ref[:, h * HEAD_DIM:(h + 1) * HEAD_DIM]
        k = k_ref[:, kv * HEAD_DIM:(kv + 1) * HEAD_DIM]
        v = v_ref[:, kv * HEAD_DIM:(kv + 1) * HEAD_DIM]
        o = _softmax_pv([_dot_nt(q, k)], [v])
        o_ref[:, h * HEAD_DIM:(h + 1) * HEAD_DIM] = o.astype(_BF16)


def _context_attention(qkv, n_kv, seq, n_rows_total):
    m = qkv.shape[0]
    kv_dim = n_kv * HEAD_DIM
    k_blk = D_MODEL // kv_dim
    return pl.pallas_call(
        functools.partial(_ctx_attn_kernel, group=N_HEADS // n_kv),
        grid=(m // seq,),
        in_specs=[
            pl.BlockSpec((seq, D_MODEL), lambda b: (b, 0)),
            pl.BlockSpec((seq, kv_dim), lambda b: (b, k_blk)),
            pl.BlockSpec((seq, kv_dim), lambda b: (b, k_blk + 1)),
        ],
        out_specs=pl.BlockSpec((seq, D_MODEL), lambda b: (b, 0)),
        out_shape=jax.ShapeDtypeStruct((n_rows_total, D_MODEL), _BF16),
        compiler_params=_params(("arbitrary",)),
        name="context_attention",
    )(qkv, qkv, qkv)


def _na_table_kernel(rpb_ref, o_ref):
    h = pl.program_id(0)
    c = lax.broadcasted_iota(jnp.int32, (GRID_W, GRID_W), 0)
    kc = lax.broadcasted_iota(jnp.int32, (GRID_W, GRID_W), 1)
    start = jnp.clip(c - WIN_C // 2, 0, GRID_W - WIN_C)
    col_ok = (kc >= start) & (kc < start + WIN_C)
    rel = kc - c + WIN_C - 1
    n_dc = 2 * WIN_C - 1
    masked = jnp.full((GRID_W, GRID_W), NEG_INF, _F32)
    tiles = []
    for d in range(-8, NA_PAIRS - 8 + 1):
        if 0 <= d < 2 * WIN_R - 1:
            t = jnp.zeros((GRID_W, GRID_W), _F32)
            for jj in range(n_dc):
                t = jnp.where(rel == jj, rpb_ref[h, d * n_dc + jj] * LOG2_E, t)
            tiles.append(jnp.where(col_ok, t, NEG_INF))
        else:
            tiles.append(masked)
    for i in range(NA_PAIRS):
        o_ref[0, i] = jnp.concatenate([tiles[i], tiles[i + 1]], axis=1)
        o_ref[0, NA_PAIRS + i] = jnp.concatenate([tiles[i], masked], axis=1)
        o_ref[0, 2 * NA_PAIRS + i] = jnp.concatenate([masked, tiles[i + 1]], axis=1)
    o_ref[0, 3 * NA_PAIRS] = jnp.concatenate([masked, masked], axis=1)


def _na_bias_table(rpb):
    h = rpb.shape[0]
    return pl.pallas_call(
        _na_table_kernel,
        grid=(h,),
        in_specs=[pl.BlockSpec(memory_space=pltpu.SMEM)],
        out_specs=pl.BlockSpec((1, NA_TABLE, GRID_W, 2 * GRID_W), lambda i: (i, 0, 0, 0)),
        out_shape=jax.ShapeDtypeStruct((h, NA_TABLE, GRID_W, 2 * GRID_W), _F32),
        compiler_params=_params(("arbitrary",)),
        name="na_bias_table",
    )(rpb.reshape(h, -1))


def _na_tile_plan(rows):
    nblk = rows // NA_Q_ROWS
    plan = np.zeros((nblk, NA_Q_ROWS * (NA_K_ROWS // 2)), np.int32)
    for blk in range(nblk):
        rs = blk * NA_Q_ROWS
        ks = min(max(rs - WIN_R // 2, 0), rows - NA_K_ROWS)
        for qi in range(NA_Q_ROWS):
            r0 = min(max(rs + qi - WIN_R // 2, 0), rows - WIN_R)
            for p in range(NA_K_ROWS // 2):
                left = r0 <= ks + 2 * p < r0 + WIN_R
                right = r0 <= ks + 2 * p + 1 < r0 + WIN_R
                pair = ks - rs + 2 * WIN_R - 1 - qi + 2 * p
                if left and right:
                    entry = pair
                elif left:
                    entry = NA_PAIRS + pair
                elif right:
                    entry = 2 * NA_PAIRS + pair
                else:
                    entry = 3 * NA_PAIRS
                plan[blk, qi * (NA_K_ROWS // 2) + p] = entry
    return plan


def _na_attn_kernel(plan_ref, alias_ref, q_ref, k_ref, v_ref, kc_ref, vc_ref, tab_ref, o_ref, *, rows):
    del alias_ref
    blk = pl.program_id(2)
    rs = blk * NA_Q_ROWS
    ks = jnp.clip(rs - WIN_R // 2, 0, rows - NA_K_ROWS)
    k0 = pl.multiple_of(ks * GRID_W, 4 * GRID_W)
    n_keys = NA_K_ROWS * GRID_W
    pairs = NA_K_ROWS // 2
    for hh in range(NA_HEADS):
        cols = slice(hh * HEAD_DIM, (hh + 1) * HEAD_DIM)
        q = q_ref[:, cols]
        k = k_ref[pl.ds(k0, n_keys), cols]
        v = v_ref[pl.ds(k0, n_keys), cols]
        bias = jnp.concatenate(
            [jnp.concatenate([tab_ref[hh, plan_ref[blk, qi * pairs + p]] for p in range(pairs)], axis=1)
             for qi in range(NA_Q_ROWS)], axis=0)
        s = _dot_nt(q, k) + bias
        s_ctx = _dot_nt(q, kc_ref[0, :, cols])
        o_ref[:, cols] = _softmax_pv([s, s_ctx], [v, vc_ref[0, :, cols]]).astype(_BF16)


def _neighbourhood_attention(attn, qkv, k_ctx, v_ctx, table, row0):
    b, l, _ = k_ctx.shape
    t = qkv.shape[0] // b
    rows = t // GRID_W
    tq = NA_Q_ROWS * GRID_W
    nblk = t // tq
    w = NA_HEADS * HEAD_DIM
    hp = N_HEADS // NA_HEADS
    out_blk0 = row0 // tq
    return pl.pallas_call(
        functools.partial(_na_attn_kernel, rows=rows),
        grid_spec=pltpu.PrefetchScalarGridSpec(
            num_scalar_prefetch=1,
            grid=(b, hp, nblk),
            in_specs=[
                pl.BlockSpec(memory_space=pl.ANY),
                pl.BlockSpec((tq, w), lambda bi, h, i, plan: (bi * nblk + i, h)),
                pl.BlockSpec((t, w), lambda bi, h, i, plan: (bi, hp + h)),
                pl.BlockSpec((t, w), lambda bi, h, i, plan: (bi, 2 * hp + h)),
                pl.BlockSpec((1, l, w), lambda bi, h, i, plan: (bi, 0, h)),
                pl.BlockSpec((1, l, w), lambda bi, h, i, plan: (bi, 0, h)),
                pl.BlockSpec((NA_HEADS, NA_TABLE, GRID_W, 2 * GRID_W), lambda bi, h, i, plan: (h, 0, 0, 0)),
            ],
            out_specs=pl.BlockSpec((tq, w), lambda bi, h, i, plan: (out_blk0 + bi * nblk + i, h)),
        ),
        out_shape=jax.ShapeDtypeStruct(attn.shape, attn.dtype),
        input_output_aliases={1: 0},
        compiler_params=_params(("arbitrary", "arbitrary", "arbitrary")),
        name="neighbourhood_attention",
    )(jnp.asarray(_na_tile_plan(rows)), attn, qkv, qkv, qkv, k_ctx, v_ctx, table)


def _gqa_attn_kernel(alias_ref, q_ref, k_ref, v_ref, kc_ref, vc_ref, o_ref, *, group, chunk):
    del alias_ref
    tq = q_ref.shape[0]
    q = jnp.concatenate([q_ref[:, g * HEAD_DIM:(g + 1) * HEAD_DIM] for g in range(group)], axis=0)
    t = k_ref.shape[0]
    m = l = acc = None
    for c in range(t // chunk + 1):
        if c < t // chunk:
            k = k_ref[c * chunk:(c + 1) * chunk, :]
            v = v_ref[c * chunk:(c + 1) * chunk, :]
        else:
            k, v = kc_ref[0], vc_ref[0]
        s = _dot_nt(q, k)
        m_c = jnp.max(s, axis=-1, keepdims=True)
        if m is None:
            m = m_c
            p = jnp.exp2(s - m)
            l = jnp.sum(p, axis=-1, keepdims=True)
            acc = _dot(p.astype(_BF16), v)
        else:
            m_new = jnp.maximum(m, m_c)
            alpha = jnp.exp2(m - m_new)
            p = jnp.exp2(s - m_new)
            l = alpha * l + jnp.sum(p, axis=-1, keepdims=True)
            acc = alpha * acc + _dot(p.astype(_BF16), v)
            m = m_new
    o = acc / l
    for g in range(group):
        o_ref[:, g * HEAD_DIM:(g + 1) * HEAD_DIM] = o[g * tq:(g + 1) * tq].astype(_BF16)


def _gqa_attention(attn, qkv, k_ctx, v_ctx, row0):
    b, l, kv_dim = k_ctx.shape
    n_kv = kv_dim // HEAD_DIM
    group = N_HEADS // n_kv
    t = qkv.shape[0] // b
    tq = 512
    nblk = t // tq
    out_blk0 = row0 // tq
    return pl.pallas_call(
        functools.partial(_gqa_attn_kernel, group=group, chunk=1024),
        grid=(b, n_kv, nblk),
        in_specs=[
            pl.BlockSpec(memory_space=pl.ANY),
            pl.BlockSpec((tq, group * HEAD_DIM), lambda bi, h, i: (bi * nblk + i, h)),
            pl.BlockSpec((t, HEAD_DIM), lambda bi, h, i: (bi, N_HEADS + h)),
            pl.BlockSpec((t, HEAD_DIM), lambda bi, h, i: (bi, N_HEADS + n_kv + h)),
            pl.BlockSpec((1, l, HEAD_DIM), lambda bi, h, i: (bi, 0, h)),
            pl.BlockSpec((1, l, HEAD_DIM), lambda bi, h, i: (bi, 0, h)),
        ],
        out_specs=pl.BlockSpec((tq, group * HEAD_DIM), lambda bi, h, i: (out_blk0 + bi * nblk + i, h)),
        out_shape=jax.ShapeDtypeStruct(attn.shape, attn.dtype),
        input_output_aliases={0: 0},
        compiler_params=_params(("arbitrary", "arbitrary", "arbitrary")),
        name="gqa_attention",
    )(attn, qkv, qkv, qkv, k_ctx, v_ctx)


def _pack_bf16_halves(h):
    half = h.shape[1] // 2

    def rounded(v):
        b = lax.bitcast_convert_type(v, jnp.int32)
        return b + 0x7FFF + ((b >> 16) & 1)

    lo, hi = rounded(h[:, :half]), rounded(h[:, half:])
    return (hi & -65536) | ((lo >> 16) & 0xFFFF)


def _unpack_bf16_halves(p):
    lo = lax.bitcast_convert_type(p << 16, _F32)
    hi = lax.bitcast_convert_type(p & -65536, _F32)
    return jnp.concatenate([lo, hi], axis=1).astype(_BF16)


def _route(logits, lane):
    def masked_softmax(mask):
        z = jnp.where(mask, logits, -jnp.inf)
        e = jnp.exp(z - jnp.max(z, axis=-1, keepdims=True))
        return e / jnp.sum(e, axis=-1, keepdims=True)

    def top1(p, mask):
        best = jnp.max(jnp.where(mask, p, -1.0), axis=-1, keepdims=True)
        idx = jnp.min(jnp.where(mask & (p == best), lane, float(LANES)), axis=-1, keepdims=True)
        return best, idx

    g_mask = lane < N_GROUPS
    g_p, g_idx = top1(masked_softmax(g_mask), g_mask)
    lo = N_GROUPS + g_idx * EXPERTS_PER_GROUP
    e_mask = (lane >= lo) & (lane < lo + EXPERTS_PER_GROUP)
    e_prob = masked_softmax(e_mask)
    p1, i1 = top1(e_prob, e_mask)
    p2, i2 = top1(e_prob, e_mask & (lane != i1))
    return i1 - N_GROUPS, i2 - N_GROUPS, g_p * p1 / (p1 + p2), g_p * p2 / (p1 + p2)


def _proj_router_kernel(*refs, first_blocks):
    a_ref = refs[0]
    if first_blocks is None:
        x_ref = refs[1]
        refs = refs[2:]
    else:
        xa_ref, xb_ref = refs[1:3]
        x_ref = refs[-1]
        refs = refs[3:-1]
    mod_ref, g_ref, wo_ref, wr_ref, br_ref, x1_ref, h_ref, ri_ref, rf_ref, cnt_ref, carry = refs
    i = pl.program_id(0)
    tm = a_ref.shape[0]

    @pl.when(i == 0)
    def _():
        carry[...] = jnp.zeros_like(carry)

    if first_blocks is not None:
        @pl.when(i < first_blocks)
        def _():
            x_ref[...] = xa_ref[...]

        @pl.when(i >= first_blocks)
        def _():
            x_ref[...] = xb_ref[...]

    o = _dot(a_ref[...], wo_ref[...])
    x1 = x_ref[...] + _mod_slices(mod_ref, 2) * o
    x1_ref[...] = x1
    h = _norm_modulate(x1, g_ref[...], _mod_slices(mod_ref, 3), _mod_slices(mod_ref, 4))
    h_ref[...] = _pack_bf16_halves(h)

    logits = _dot(h.astype(_BF16), wr_ref[...]) + br_ref[...]
    lane = lax.broadcasted_iota(jnp.int32, logits.shape, 1).astype(_F32)
    e1, e2, gate1, gate2 = _route(logits, lane)

    hit1, hit2 = lane == e1, lane == e2
    onehot = jnp.where(hit1 | hit2, 1.0, 0.0)
    row = lax.broadcasted_iota(jnp.int32, (tm, tm), 0)
    col = lax.broadcasted_iota(jnp.int32, (tm, tm), 1)
    lower = jnp.where(row > col, 1.0, 0.0).astype(_BF16)
    before = _dot(lower, onehot.astype(_BF16)) + carry[...]
    rank1 = jnp.sum(jnp.where(hit1, before, 0.0), axis=-1, keepdims=True)
    rank2 = jnp.sum(jnp.where(hit2, before, 0.0), axis=-1, keepdims=True)
    carry[...] = carry[...] + jnp.sum(onehot, axis=0, keepdims=True)

    record = jnp.where(lane == 0, e1, jnp.where(lane == 1, e2,
                       jnp.where(lane == 2, rank1, jnp.where(lane == 3, rank2, 0.0))))
    ri_ref[...] = record.T[:MOD_ROWS].astype(jnp.int32)
    rf_ref[...] = jnp.where(lane == 0, gate1, jnp.where(lane == 1, gate2, 0.0))
    cnt_ref[...] = jnp.broadcast_to(carry[...], cnt_ref.shape)


def _proj_router(attn, x_parts, mod, g, w_o, layer, w_r, b_r):
    m, d = attn.shape
    tm = 512
    blocks_per_group = GROUP_ROWS // tm
    row = lambda i: (i, 0)
    const = lambda i: (0, 0)
    if len(x_parts) == 1:
        first_blocks = None
        x_specs = [pl.BlockSpec((tm, d), row)]
        x_scratch = []
    else:
        first_blocks = x_parts[0].shape[0] // tm
        x_specs = [pl.BlockSpec((tm, d), lambda i: (jnp.minimum(i, first_blocks - 1), 0)),
                   pl.BlockSpec((tm, d), lambda i: (jnp.maximum(i - first_blocks, 0), 0))]
        x_scratch = [pltpu.VMEM((tm, d), _F32)]
    return pl.pallas_call(
        functools.partial(_proj_router_kernel, first_blocks=first_blocks),
        grid=(m // tm,),
        in_specs=[
            pl.BlockSpec((tm, d), row),
            *x_specs,
            pl.BlockSpec((1, 1, 6 * d), lambda i: (i // blocks_per_group, 0, 0)),
            pl.BlockSpec((1, d), const),
            pl.BlockSpec((None, d, d), lambda i: (layer, 0, 0)),
            pl.BlockSpec((d, LANES), const),
            pl.BlockSpec((1, LANES), const),
        ],
        out_specs=[
            pl.BlockSpec((tm, d), row),
            pl.BlockSpec((tm, d // 2), row),
            pl.BlockSpec((MOD_ROWS, tm), lambda i: (0, i)),
            pl.BlockSpec((tm, LANES), row),
            pl.BlockSpec((MOD_ROWS, LANES), const),
        ],
        out_shape=[
            jax.ShapeDtypeStruct((m, d), _F32),
            jax.ShapeDtypeStruct((m, d // 2), jnp.int32),
            jax.ShapeDtypeStruct((MOD_ROWS, m), jnp.int32),
            jax.ShapeDtypeStruct((m, LANES), _F32),
            jax.ShapeDtypeStruct((MOD_ROWS, LANES), _F32),
        ],
        scratch_shapes=[pltpu.VMEM((1, LANES), _F32)] + x_scratch,
        compiler_params=_params(("arbitrary",)),
        name="proj_router",
    )(attn, *x_parts, mod, g, w_o, w_r, b_r)


def _record_copy(dest_hbm, idx, isem, blk, slot, tm):
    return pltpu.make_async_copy(dest_hbm.at[:, pl.ds(blk * tm, tm)], idx.at[slot], isem.at[slot])


def _start_row_copies(idx, slot, tm, make_copy):
    for chunk in range(tm // LANES):
        def body(j, carry, chunk=chunk):
            base = chunk * LANES + pl.multiple_of(j * 8, 8)
            for u in range(8):
                for k in range(2):
                    make_copy(base + u, k, idx[slot, k, base + u]).start(priority=k)
            return carry
        lax.fori_loop(0, LANES // 8, body, 0)


def _dispatch_kernel(dest_hbm, h_ref, xs_hbm, idx, isem, sem):
    i = pl.program_id(0)
    n = pl.num_programs(0)
    tm = h_ref.shape[0]

    @pl.when(i == 0)
    def _():
        _record_copy(dest_hbm, idx, isem, 0, 0, tm).start()

    @pl.when(i + 1 < n)
    def _():
        _record_copy(dest_hbm, idx, isem, i + 1, (i + 1) % 2, tm).start()

    _record_copy(dest_hbm, idx, isem, i, i % 2, tm).wait()
    for slot in range(2):
        @pl.when(i % 2 == slot)
        def _(slot=slot):
            _start_row_copies(idx, slot, tm, lambda r, k, d: pltpu.make_async_copy(
                h_ref.at[pl.ds(r, 1)], xs_hbm.at[pl.ds(d, 1)], sem))

    for _ in range(2):
        pltpu.make_async_copy(h_ref, xs_hbm.at[pl.ds(0, tm)], sem).wait()


def _dispatch(h2, dest, p_rows):
    m, d = h2.shape
    tm = 512
    return pl.pallas_call(
        _dispatch_kernel,
        grid=(m // tm,),
        in_specs=[
            pl.BlockSpec(memory_space=pl.ANY),
            pl.BlockSpec((tm, d), lambda i: (i, 0)),
        ],
        out_specs=pl.BlockSpec(memory_space=pl.ANY),
        out_shape=jax.ShapeDtypeStruct((p_rows, d), h2.dtype),
        scratch_shapes=[
            pltpu.SMEM((2, MOD_ROWS, tm), jnp.int32),
            pltpu.SemaphoreType.DMA((2,)),
            pltpu.SemaphoreType.DMA,
        ],
        compiler_params=_params(("arbitrary",)),
        name="dispatch",
    )(dest, h2)


def _expert_kernel(be_ref, valid_ref, next_ref, x_ref, wg_hbm, wu_hbm, wd_hbm, o_ref,
                   stage_g, stage_u, stage_d, wg, wu, wd, sem, *, layer):
    b = pl.program_id(0)
    valid = valid_ref[b]
    expert = be_ref[b]

    def fetch(e):
        return [pltpu.make_async_copy(src.at[layer, e], dst, sem)
                for src, dst in ((wg_hbm, stage_g), (wu_hbm, stage_u), (wd_hbm, stage_d))]

    @pl.when(b == 0)
    def _():
        for copy in fetch(expert):
            copy.start()

    first = (b == 0) | (expert != be_ref[jnp.maximum(b - 1, 0)])

    @pl.when(first & (valid > 0))
    def _():
        for copy in fetch(expert):
            copy.wait()
        rows = 256
        for stage, dst in ((stage_g, wg), (stage_u, wu), (stage_d, wd)):
            for r in range(0, stage.shape[0], rows):
                dst[r:r + rows, :] = stage[r:r + rows, :].astype(_BF16)

        @pl.when(next_ref[b] >= 0)
        def _():
            for copy in fetch(next_ref[b]):
                copy.start()

    @pl.when(valid > 0)
    def _():
        row = lax.broadcasted_iota(jnp.int32, x_ref.shape, 0)
        x = _unpack_bf16_halves(jnp.where(row < valid, x_ref[...], 0))
        hidden = _silu(_dot(x, wg[...])) * _dot(x, wu[...])
        o_ref[...] = _dot(hidden.astype(_BF16), wd[...])

    @pl.when(valid <= 0)
    def _():
        o_ref[...] = jnp.zeros_like(o_ref)


def _experts(xs, block_e, block_valid, block_next, w_gate, w_up, w_down, layer):
    p = xs.shape[0]
    d, de = w_gate.shape[-2:]
    nb = p // MOE_ROWS
    return pl.pallas_call(
        functools.partial(_expert_kernel, layer=layer),
        grid_spec=pltpu.PrefetchScalarGridSpec(
            num_scalar_prefetch=3,
            grid=(nb,),
            in_specs=[
                pl.BlockSpec((MOE_ROWS, d // 2), lambda b, be, nv, nx: (b, 0)),
                pl.BlockSpec(memory_space=pl.ANY),
                pl.BlockSpec(memory_space=pl.ANY),
                pl.BlockSpec(memory_space=pl.ANY),
            ],
            out_specs=pl.BlockSpec((MOE_ROWS, d), lambda b, be, nv, nx: (b, 0)),
            scratch_shapes=[
                pltpu.VMEM((d, de), _F32), pltpu.VMEM((d, de), _F32), pltpu.VMEM((de, d), _F32),
                pltpu.VMEM((d, de), _BF16), pltpu.VMEM((d, de), _BF16), pltpu.VMEM((de, d), _BF16),
                pltpu.SemaphoreType.DMA,
            ],
        ),
        out_shape=jax.ShapeDtypeStruct((p, d), _F32),
        compiler_params=_params(("arbitrary",)),
        name="experts",
    )(block_e, block_valid, block_next, xs, w_gate, w_up, w_down)


def _combine_kernel(dest_hbm, yb_hbm, x_ref, mod_ref, rf_ref, o_ref, idx, buf, isem, sem, *, blk0):
    i = pl.program_id(0)
    n = pl.num_programs(0)
    tm = x_ref.shape[0]

    def records(blk, slot):
        return _record_copy(dest_hbm, idx, isem, blk0 + blk, slot, tm)

    def fetch_rows(slot):
        _start_row_copies(idx, slot, tm, lambda r, k, d: pltpu.make_async_copy(
            yb_hbm.at[pl.ds(d, 1)], buf.at[slot, k, pl.ds(r, 1)], sem.at[slot]))

    @pl.when(i == 0)
    def _():
        records(0, 0).start()

        @pl.when(n > 1)
        def _():
            records(1, 1).start()

        records(0, 0).wait()
        fetch_rows(0)

    @pl.when(i + 2 < n)
    def _():
        records(i + 2, i % 2).start()

    for slot in range(2):
        @pl.when((i + 1 < n) & ((i + 1) % 2 == slot))
        def _(slot=slot):
            records(i + 1, slot).wait()
            fetch_rows(slot)

    slot = i % 2
    for k in range(2):
        pltpu.make_async_copy(yb_hbm.at[pl.ds(0, tm)], buf.at[slot, k], sem.at[slot]).wait()
    gate = rf_ref[...]
    y = buf[slot, 0] * gate[:, 0:1] + buf[slot, 1] * gate[:, 1:2]
    o_ref[...] = x_ref[...] + _mod_slices(mod_ref, 5) * y


def _combine(x1, mod, yb, dest, rf, row0, m):
    d = x1.shape[1]
    tm = 512
    blocks_per_group = GROUP_ROWS // tm
    blk0 = row0 // tm
    row = lambda i: (blk0 + i, 0)
    return pl.pallas_call(
        functools.partial(_combine_kernel, blk0=blk0),
        grid=(m // tm,),
        in_specs=[
            pl.BlockSpec(memory_space=pl.ANY),
            pl.BlockSpec(memory_space=pl.ANY),
            pl.BlockSpec((tm, d), row),
            pl.BlockSpec((1, 1, 6 * d), lambda i: ((blk0 + i) // blocks_per_group, 0, 0)),
            pl.BlockSpec((tm, LANES), row),
        ],
        out_specs=pl.BlockSpec((tm, d), lambda i: (i, 0)),
        out_shape=jax.ShapeDtypeStruct((m, d), _F32),
        scratch_shapes=[
            pltpu.SMEM((2, MOD_ROWS, tm), jnp.int32),
            pltpu.VMEM((2, 2, tm, d), _F32),
            pltpu.SemaphoreType.DMA((2,)),
            pltpu.SemaphoreType.DMA((2,)),
        ],
        compiler_params=_params(("arbitrary",)),
        name="combine",
    )(dest, yb, x1, mod, rf)


def _moe(x1, h2, ri, rf, counts, mod, w_gate, w_up, w_down, layer, splits):
    m = x1.shape[0]
    p_rows = (2 * m + N_EXPERTS * (MOE_ROWS - 1) + MOE_ROWS - 1) // MOE_ROWS * MOE_ROWS
    nb = p_rows // MOE_ROWS
    cnt = counts[0, :N_EXPERTS].astype(jnp.int32)
    blocks_e = (cnt + MOE_ROWS - 1) // MOE_ROWS
    end_blk = jnp.cumsum(blocks_e)
    start_blk = end_blk - blocks_e
    starts = (start_blk * MOE_ROWS).astype(jnp.int32)
    blk = jnp.arange(nb, dtype=jnp.int32)
    block_e = jnp.minimum(jnp.sum(end_blk[None, :] <= blk[:, None], axis=1), N_EXPERTS - 1).astype(jnp.int32)
    mine = block_e[:, None] == jnp.arange(N_EXPERTS, dtype=jnp.int32)[None, :]
    cnt_b = jnp.sum(jnp.where(mine, cnt[None, :], 0), axis=1)
    start_b = jnp.sum(jnp.where(mine, start_blk[None, :], 0), axis=1)
    block_valid = jnp.clip(cnt_b - (blk - start_b) * MOE_ROWS, 0, MOE_ROWS).astype(jnp.int32)
    ids = jnp.arange(N_EXPERTS, dtype=jnp.int32)
    later = (ids[None, :] > ids[:, None]) & (cnt[None, :] > 0)
    next_e = jnp.min(jnp.where(later, ids[None, :], N_EXPERTS), axis=1)
    next_e = jnp.where(next_e == N_EXPERTS, -1, next_e)
    block_next = jnp.sum(jnp.where(mine, next_e[None, :], 0), axis=1).astype(jnp.int32)
    experts = jnp.arange(N_EXPERTS, dtype=jnp.int32)[None, :, None]
    dest = jnp.sum(jnp.where(ri[0:2, None, :] == experts, starts[None, :, None], 0), axis=1) + ri[2:4]
    dest = jnp.concatenate([dest, jnp.zeros((MOD_ROWS - 2, m), jnp.int32)], axis=0)
    xs = _dispatch(h2, dest, p_rows)
    yb = _experts(xs, block_e, block_valid, block_next, w_gate, w_up, w_down, layer)
    return [_combine(x1, mod, yb, dest, rf, row0, rows) for row0, rows in splits]


def kernel(x_prompt, x_sample, c, cache_k_a, cache_v_a, cache_k_b, cache_v_b, c_ctx, w_mod, b_mod, norm1_g, norm2_g, w_qkv_a, w_o_a, q_norm_a, k_norm_a, rpb_a, w_qkv_b, w_o_b, q_norm_b, k_norm_b, w_router_group, b_router_group, w_router_expert, b_router_expert, w_gate, w_up, w_down):
    batch, seq, d = x_prompt.shape
    dec_batch, dec_seq, _ = x_sample.shape
    depth = w_mod.shape[0]
    mp = batch * seq
    ms = dec_batch * dec_seq
    assert mp == GROUP_ROWS and dec_seq == GROUP_ROWS and d == D_MODEL
    x_parts = [x_prompt.reshape(mp, d), x_sample.reshape(ms, d)]

    cond = jnp.concatenate([c_ctx[None], c, jnp.zeros((MOD_ROWS - 1 - dec_batch, d), _F32)], axis=0)
    mod_all = _modulation(cond, w_mod, b_mod)
    rope_tables = _rope_tables(dec_seq)

    w_router = jnp.concatenate([w_router_group, w_router_expert], axis=-1)
    n_route = w_router.shape[-1]
    w_router = jnp.pad(w_router, ((0, 0), (0, 0), (0, LANES - n_route))).astype(_BF16)
    b_router = jnp.pad(jnp.concatenate([b_router_group, b_router_expert], axis=-1),
                       ((0, 0), (0, LANES - n_route)))

    w_qkv_a, w_o_a, w_qkv_b, w_o_b = (_to_bf16(w) for w in (w_qkv_a, w_o_a, w_qkv_b, w_o_b))

    kv_arrays = [None, None]
    for i in range(depth):
        j = i // 2
        kind = i % 2
        slots = (depth + 1 - kind) // 2
        mod = mod_all[i].reshape(MOD_ROWS, 1, 6 * d)
        g1 = norm1_g[i].reshape(1, d)
        g2 = norm2_g[i].reshape(1, d)
        if i % 2 == 0:
            n_kv, w_qkv, w_o, qn, kn = N_HEADS, w_qkv_a, w_o_a, q_norm_a[j], k_norm_a[j]
            cache_k, cache_v, rope = cache_k_a[:, j], cache_v_a[:, j], None
        else:
            n_kv, w_qkv, w_o, qn, kn = N_KV_B, w_qkv_b, w_o_b, q_norm_b[j], k_norm_b[j]
            cache_k, cache_v, rope = cache_k_b[:, j], cache_v_b[:, j], rope_tables
        kv_dim = n_kv * HEAD_DIM
        qn, kn = qn.reshape(1, HEAD_DIM), kn.reshape(1, HEAD_DIM)
        x_p, x_s = (x_parts[0], x_parts[1]) if len(x_parts) == 2 else (x_parts[0], x_parts[0])
        row0_s = 0 if len(x_parts) == 2 else mp
        qkv_p, *kv = _qkv(x_p, mod, g1, w_qkv, j, qn, kn, n_kv, row0=0, m=mp, group0=0, rope_tables=None,
                          kv_out=(seq, slots, j, kv_arrays[kind]))
        kv_arrays[kind] = kv
        (qkv_s,) = _qkv(x_s, mod, g1, w_qkv, j, qn, kn, n_kv, row0=row0_s, m=ms, group0=1, rope_tables=rope)

        attn = _context_attention(qkv_p, n_kv, seq, mp + ms)
        k_ctx = cache_k.reshape(dec_batch, -1, kv_dim).astype(_BF16)
        v_ctx = cache_v.reshape(dec_batch, -1, kv_dim).astype(_BF16)
        if i % 2 == 0:
            attn = _neighbourhood_attention(attn, qkv_s, k_ctx, v_ctx, _na_bias_table(rpb_a[j]), mp)
        else:
            attn = _gqa_attention(attn, qkv_s, k_ctx, v_ctx, mp)

        x1, h2, ri, rf, counts = _proj_router(attn, x_parts, mod, g2, w_o, j, w_router[i],
                                              b_router[i].reshape(1, LANES))
        last = i == depth - 1
        outs = _moe(x1, h2, ri, rf, counts, mod, w_gate, w_up, w_down, i,
                    [(0, mp), (mp, ms)] if last else [(0, mp + ms)])
        x_parts = [outs[0]]

    def per_layer(t, n_kv):
        return t.reshape(batch, -1, seq, n_kv, HEAD_DIM)

    return (outs[0].reshape(batch, seq, d), outs[1].reshape(dec_batch, dec_seq, d),
            per_layer(kv_arrays[0][0], N_HEADS), per_layer(kv_arrays[0][1], N_HEADS),
            per_layer(kv_arrays[1][0], N_KV_B), per_layer(kv_arrays[1][1], N_KV_B))
```

```python
import functools
import math

import numpy as np

import jax
import jax.numpy as jnp
from jax import lax
from jax.experimental import pallas as pl
from jax.experimental.pallas import tpu as pltpu

D_MODEL = 2048
HEAD_DIM = 128
N_HEADS = D_MODEL // HEAD_DIM
N_KV_B = 4
GRID_W = 64
WIN_R = 8
WIN_C = 16
ROPE_THETA = 10000.0
N_GROUPS = 4
EXPERTS_PER_GROUP = 8
N_EXPERTS = N_GROUPS * EXPERTS_PER_GROUP
D_EXPERT = 768
EPS = 1e-6
NEG_INF = -1e30

LANES = 128
MOD_ROWS = 8
GROUP_ROWS = 4096
MOE_ROWS = 256
NA_Q_ROWS = 8
NA_K_ROWS = 16
NA_PAIRS = 30
NA_TABLE = 3 * NA_PAIRS + 1
NA_HEADS = 4
VMEM_LIMIT = 56 * 1024 * 1024
LOG2_E = math.log2(math.e)
Q_SCALE = HEAD_DIM ** -0.5 * LOG2_E

_BF16 = jnp.bfloat16
_F32 = jnp.float32


def _params(sem, vmem=VMEM_LIMIT):
    return pltpu.CompilerParams(dimension_semantics=sem, vmem_limit_bytes=vmem)


def _dot(a, b):
    return jnp.dot(a, b, preferred_element_type=_F32)


def _dot_nt(a, b):
    return lax.dot_general(a, b, (((1,), (1,)), ((), ())), preferred_element_type=_F32)


def _silu(x):
    return x / (1.0 + jnp.exp(-x))


def _cast_kernel(w_ref, o_ref):
    o_ref[...] = w_ref[...].astype(o_ref.dtype)


def _to_bf16(w):
    cols = w.shape[-1]
    w2 = w.reshape(-1, cols)
    rows = w2.shape[0]
    rb = max(8, min(rows, (4 * 1024 * 1024) // (4 * cols)) // 8 * 8)
    while rows % rb:
        rb -= 8
    out = pl.pallas_call(
        _cast_kernel,
        grid=(rows // rb,),
        in_specs=[pl.BlockSpec((rb, cols), lambda i: (i, 0))],
        out_specs=pl.BlockSpec((rb, cols), lambda i: (i, 0)),
        out_shape=jax.ShapeDtypeStruct((rows, cols), _BF16),
        compiler_params=_params(("arbitrary",)),
        name="to_bf16",
    )(w2)
    return out.reshape(w.shape)


def _mod_kernel(cond_ref, w_ref, b_ref, o_ref):
    s = _silu(cond_ref[...]).astype(_BF16)
    o_ref[...] = _dot(s, w_ref[...].astype(_BF16)) + b_ref[...]


def _modulation(cond, w_mod, b_mod):
    depth, d, n = w_mod.shape
    tn = 1024
    return pl.pallas_call(
        _mod_kernel,
        grid=(depth, n // tn),
        in_specs=[
            pl.BlockSpec((MOD_ROWS, d), lambda l, j: (0, 0)),
            pl.BlockSpec((None, d, tn), lambda l, j: (l, 0, j)),
            pl.BlockSpec((None, 1, tn), lambda l, j: (l, 0, j)),
        ],
        out_specs=pl.BlockSpec((None, MOD_ROWS, tn), lambda l, j: (l, 0, j)),
        out_shape=jax.ShapeDtypeStruct((depth, MOD_ROWS, n), _F32),
        compiler_params=_params(("arbitrary", "arbitrary")),
        name="modulation",
    )(cond, w_mod, b_mod.reshape(depth, 1, n))


def _mod_slices(mod_ref, which):
    return mod_ref[0, :, which * D_MODEL:(which + 1) * D_MODEL]


def _norm_modulate(x, g, shift, scale):
    y = x * lax.rsqrt(jnp.mean(x * x, axis=-1, keepdims=True) + EPS)
    return (y * g) * (1.0 + scale) + shift


def _head_norm(a, g):
    return a * lax.rsqrt(jnp.mean(a * a, axis=-1, keepdims=True) + EPS) * g


def _rope_partner_matrix():
    j = np.arange(HEAD_DIM)
    partner = np.where((j % 64) < 32, j + 32, j - 32)
    p = np.zeros((HEAD_DIM, HEAD_DIM), np.float32)
    p[partner, j] = 1.0
    return jnp.asarray(p, _BF16)


def _rope(a, cos, sin, partner_ref):
    return a * cos + _dot(a.astype(_BF16), partner_ref[...]) * sin


def _store_requests(ref, rows):
    seq = ref.shape[1]
    for b in range(ref.shape[0]):
        ref[b] = rows[b * seq:(b + 1) * seq, :]


def _qkv_kernel(*refs, n_q, n_k, heads_per_tile, rope, emit_kv, n_aliased):
    x_ref, mod_ref, g_ref, w_ref, qn_ref, kn_ref = refs[:6]
    refs = refs[6:]
    if rope:
        cos_ref, sin_ref, partner_ref = refs[:3]
        refs = refs[3:]
    refs = refs[n_aliased:]
    o_ref = refs[0]
    k32_ref, v32_ref = refs[1:3] if emit_kv else (None, None)
    h_scr = refs[-1]
    j = pl.program_id(1)

    @pl.when(j == 0)
    def _():
        h = _norm_modulate(x_ref[...], g_ref[...], _mod_slices(mod_ref, 0), _mod_slices(mod_ref, 1))
        h_scr[...] = h.astype(_BF16)

    acc = _dot(h_scr[...], w_ref[...])

    def heads(gain_ref, scale):
        gain = gain_ref[...] * scale
        outs = []
        for hh in range(heads_per_tile):
            a = _head_norm(acc[:, hh * HEAD_DIM:(hh + 1) * HEAD_DIM], gain)
            if rope:
                a = _rope(a, cos_ref[...], sin_ref[...], partner_ref)
            outs.append(a)
        return jnp.concatenate(outs, axis=1)

    @pl.when(j < n_q)
    def _():
        o_ref[...] = heads(qn_ref, Q_SCALE).astype(_BF16)

    @pl.when((j >= n_q) & (j < n_q + n_k))
    def _():
        k = heads(kn_ref, 1.0)
        o_ref[...] = k.astype(_BF16)
        if emit_kv:
            _store_requests(k32_ref, k)

    @pl.when(j >= n_q + n_k)
    def _():
        o_ref[...] = acc.astype(_BF16)
        if emit_kv:
            _store_requests(v32_ref, acc)


def _qkv(x, mod, g, w, layer, qn, kn, n_kv, *, row0, m, group0, rope_tables, kv_out=None):
    d = x.shape[1]
    n = w.shape[-1]
    tm, tn = 1024, 512
    heads_per_tile = tn // HEAD_DIM
    n_q = D_MODEL // tn
    n_k = n_kv * HEAD_DIM // tn
    rope = rope_tables is not None
    emit_kv = kv_out is not None
    blocks_per_group = GROUP_ROWS // tm
    blk0 = row0 // tm

    def group(i):
        return group0 + i // blocks_per_group

    in_specs = [
        pl.BlockSpec((tm, d), lambda i, j: (blk0 + i, 0)),
        pl.BlockSpec((1, 1, 6 * d), lambda i, j: (group(i), 0, 0)),
        pl.BlockSpec((1, d), lambda i, j: (0, 0)),
        pl.BlockSpec((None, d, tn), lambda i, j: (layer, 0, j)),
        pl.BlockSpec((1, HEAD_DIM), lambda i, j: (0, 0)),
        pl.BlockSpec((1, HEAD_DIM), lambda i, j: (0, 0)),
    ]
    args = [x, mod, g, w, qn, kn]
    if rope:
        pos = lambda i, j: (i % blocks_per_group, 0)
        in_specs += [pl.BlockSpec((tm, HEAD_DIM), pos), pl.BlockSpec((tm, HEAD_DIM), pos),
                     pl.BlockSpec((HEAD_DIM, HEAD_DIM), lambda i, j: (0, 0))]
        args += list(rope_tables) + [_rope_partner_matrix()]
    out_specs = [pl.BlockSpec((tm, tn), lambda i, j: (i, j))]
    out_shape = [jax.ShapeDtypeStruct((m, n), _BF16)]
    aliases = {}
    n_aliased = 0
    if emit_kv:
        seq, slots, slot, previous = kv_out
        per_block = tm // seq
        out_specs += [pl.BlockSpec((per_block, seq, tn), lambda i, j: (i, slot, jnp.clip(j - n_q, 0, n_k - 1))),
                      pl.BlockSpec((per_block, seq, tn), lambda i, j: (i, slot, jnp.clip(j - n_q - n_k, 0, n_k - 1)))]
        out_shape += [jax.ShapeDtypeStruct((m // seq, slots * seq, n_k * tn), _F32)] * 2
        if previous is not None:
            n_aliased = 2
            aliases = {len(args): 1, len(args) + 1: 2}
            in_specs += [pl.BlockSpec(memory_space=pl.ANY)] * 2
            args += list(previous)
    return pl.pallas_call(
        functools.partial(_qkv_kernel, n_q=n_q, n_k=n_k, heads_per_tile=heads_per_tile,
                          rope=rope, emit_kv=emit_kv, n_aliased=n_aliased),
        grid=(m // tm, n // tn),
        in_specs=in_specs,
        out_specs=out_specs,
        out_shape=out_shape,
        input_output_aliases=aliases,
        scratch_shapes=[pltpu.VMEM((tm, d), _BF16)],
        compiler_params=_params(("arbitrary", "arbitrary")),
        name="qkv_rope" if rope else "qkv",
    )(*args)


def _rope_tables(n_tokens):
    t = jnp.arange(n_tokens)
    quarter = HEAD_DIM // 4
    inv = ROPE_THETA ** (-jnp.arange(quarter, dtype=_F32) / quarter)
    ang_r = (t // GRID_W).astype(_F32)[:, None] * inv
    ang_c = (t % GRID_W).astype(_F32)[:, None] * inv
    cr, sr, cc, sc = jnp.cos(ang_r), jnp.sin(ang_r), jnp.cos(ang_c), jnp.sin(ang_c)
    return (jnp.concatenate([cr, cr, cc, cc], axis=1), jnp.concatenate([-sr, sr, -sc, sc], axis=1))


def _softmax_pv(scores, values):
    m = functools.reduce(jnp.maximum, [jnp.max(s, axis=-1, keepdims=True) for s in scores])
    ps = [jnp.exp2(s - m) for s in scores]
    l = functools.reduce(jnp.add, [jnp.sum(p, axis=-1, keepdims=True) for p in ps])
    o = functools.reduce(jnp.add, [_dot(p.astype(_BF16), v) for p, v in zip(ps, values)])
    return o / l


def _ctx_attn_kernel(q_ref, k_ref, v_ref, o_ref, *, group):
    for h in range(N_HEADS):
        kv = h // group
        q = q_ref[:, h * HEAD_DIM:(h + 1) * HEAD_DIM]
        k = k_ref[:, kv * HEAD_DIM:(kv + 1) * HEAD_DIM]
        v = v_ref[:, kv * HEAD_DIM:(kv + 1) * HEAD_DIM]
        o = _softmax_pv([_dot_nt(q, k)], [v])
        o_ref[:, h * HEAD_DIM:(h + 1) * HEAD_DIM] = o.astype(_BF16)


def _context_attention(qkv, n_kv, seq, n_rows_total):
    m = qkv.shape[0]
    kv_dim = n_kv * HEAD_DIM
    k_blk = D_MODEL // kv_dim
    return pl.pallas_call(
        functools.partial(_ctx_attn_kernel, group=N_HEADS // n_kv),
        grid=(m // seq,),
        in_specs=[
            pl.BlockSpec((seq, D_MODEL), lambda b: (b, 0)),
            pl.BlockSpec((seq, kv_dim), lambda b: (b, k_blk)),
            pl.BlockSpec((seq, kv_dim), lambda b: (b, k_blk + 1)),
        ],
        out_specs=pl.BlockSpec((seq, D_MODEL), lambda b: (b, 0)),
        out_shape=jax.ShapeDtypeStruct((n_rows_total, D_MODEL), _BF16),
        compiler_params=_params(("arbitrary",)),
        name="context_attention",
    )(qkv, qkv, qkv)


def _na_table_kernel(rpb_ref, o_ref):
    h = pl.program_id(0)
    c = lax.broadcasted_iota(jnp.int32, (GRID_W, GRID_W), 0)
    kc = lax.broadcasted_iota(jnp.int32, (GRID_W, GRID_W), 1)
    start = jnp.clip(c - WIN_C // 2, 0, GRID_W - WIN_C)
    col_ok = (kc >= start) & (kc < start + WIN_C)
    rel = kc - c + WIN_C - 1
    n_dc = 2 * WIN_C - 1
    masked = jnp.full((GRID_W, GRID_W), NEG_INF, _F32)
    tiles = []
    for d in range(-8, NA_PAIRS - 8 + 1):
        if 0 <= d < 2 * WIN_R - 1:
            t = jnp.zeros((GRID_W, GRID_W), _F32)
            for jj in range(n_dc):
                t = jnp.where(rel == jj, rpb_ref[h, d * n_dc + jj] * LOG2_E, t)
            tiles.append(jnp.where(col_ok, t, NEG_INF))
        else:
            tiles.append(masked)
    for i in range(NA_PAIRS):
        o_ref[0, i] = jnp.concatenate([tiles[i], tiles[i + 1]], axis=1)
        o_ref[0, NA_PAIRS + i] = jnp.concatenate([tiles[i], masked], axis=1)
        o_ref[0, 2 * NA_PAIRS + i] = jnp.concatenate([masked, tiles[i + 1]], axis=1)
    o_ref[0, 3 * NA_PAIRS] = jnp.concatenate([masked, masked], axis=1)


def _na_bias_table(rpb):
    h = rpb.shape[0]
    return pl.pallas_call(
        _na_table_kernel,
        grid=(h,),
        in_specs=[pl.BlockSpec(memory_space=pltpu.SMEM)],
        out_specs=pl.BlockSpec((1, NA_TABLE, GRID_W, 2 * GRID_W), lambda i: (i, 0, 0, 0)),
        out_shape=jax.ShapeDtypeStruct((h, NA_TABLE, GRID_W, 2 * GRID_W), _F32),
        compiler_params=_params(("arbitrary",)),
        name="na_bias_table",
    )(rpb.reshape(h, -1))


def _na_tile_plan(rows):
    nblk = rows // NA_Q_ROWS
    plan = np.zeros((nblk, NA_Q_ROWS * (NA_K_ROWS // 2)), np.int32)
    for blk in range(nblk):
        rs = blk * NA_Q_ROWS
        ks = min(max(rs - WIN_R // 2, 0), rows - NA_K_ROWS)
        for qi in range(NA_Q_ROWS):
            r0 = min(max(rs + qi - WIN_R // 2, 0), rows - WIN_R)
            for p in range(NA_K_ROWS // 2):
                left = r0 <= ks + 2 * p < r0 + WIN_R
                right = r0 <= ks + 2 * p + 1 < r0 + WIN_R
                pair = ks - rs + 2 * WIN_R - 1 - qi + 2 * p
                if left and right:
                    entry = pair
                elif left:
                    entry = NA_PAIRS + pair
                elif right:
                    entry = 2 * NA_PAIRS + pair
                else:
                    entry = 3 * NA_PAIRS
                plan[blk, qi * (NA_K_ROWS // 2) + p] = entry
    return plan


def _na_attn_kernel(plan_ref, alias_ref, q_ref, k_ref, v_ref, kc_ref, vc_ref, tab_ref, o_ref, *, rows):
    del alias_ref
    blk = pl.program_id(2)
    rs = blk * NA_Q_ROWS
    ks = jnp.clip(rs - WIN_R // 2, 0, rows - NA_K_ROWS)
    k0 = pl.multiple_of(ks * GRID_W, 4 * GRID_W)
    n_keys = NA_K_ROWS * GRID_W
    pairs = NA_K_ROWS // 2
    for hh in range(NA_HEADS):
        cols = slice(hh * HEAD_DIM, (hh + 1) * HEAD_DIM)
        q = q_ref[:, cols]
        k = k_ref[pl.ds(k0, n_keys), cols]
        v = v_ref[pl.ds(k0, n_keys), cols]
        bias = jnp.concatenate(
            [jnp.concatenate([tab_ref[hh, plan_ref[blk, qi * pairs + p]] for p in range(pairs)], axis=1)
             for qi in range(NA_Q_ROWS)], axis=0)
        s = _dot_nt(q, k) + bias
        s_ctx = _dot_nt(q, kc_ref[0, :, cols])
        o_ref[:, cols] = _softmax_pv([s, s_ctx], [v, vc_ref[0, :, cols]]).astype(_BF16)


def _neighbourhood_attention(attn, qkv, k_ctx, v_ctx, table, row0):
    b, l, _ = k_ctx.shape
    t = qkv.shape[0] // b
    rows = t // GRID_W
    tq = NA_Q_ROWS * GRID_W
    nblk = t // tq
    w = NA_HEADS * HEAD_DIM
    hp = N_HEADS // NA_HEADS
    out_blk0 = row0 // tq
    return pl.pallas_call(
        functools.partial(_na_attn_kernel, rows=rows),
        grid_spec=pltpu.PrefetchScalarGridSpec(
            num_scalar_prefetch=1,
            grid=(b, hp, nblk),
            in_specs=[
                pl.BlockSpec(memory_space=pl.ANY),
                pl.BlockSpec((tq, w), lambda bi, h, i, plan: (bi * nblk + i, h)),
                pl.BlockSpec((t, w), lambda bi, h, i, plan: (bi, hp + h)),
                pl.BlockSpec((t, w), lambda bi, h, i, plan: (bi, 2 * hp + h)),
                pl.BlockSpec((1, l, w), lambda bi, h, i, plan: (bi, 0, h)),
                pl.BlockSpec((1, l, w), lambda bi, h, i, plan: (bi, 0, h)),
                pl.BlockSpec((NA_HEADS, NA_TABLE, GRID_W, 2 * GRID_W), lambda bi, h, i, plan: (h, 0, 0, 0)),
            ],
            out_specs=pl.BlockSpec((tq, w), lambda bi, h, i, plan: (out_blk0 + bi * nblk + i, h)),
        ),
        out_shape=jax.ShapeDtypeStruct(attn.shape, attn.dtype),
        input_output_aliases={1: 0},
        compiler_params=_params(("arbitrary", "arbitrary", "arbitrary")),
        name="neighbourhood_attention",
    )(jnp.asarray(_na_tile_plan(rows)), attn, qkv, qkv, qkv, k_ctx, v_ctx, table)


def _gqa_attn_kernel(alias_ref, q_ref, k_ref, v_ref, kc_ref, vc_ref, o_ref, *, group, chunk):
    del alias_ref
    tq = q_ref.shape[0]
    q = jnp.concatenate([q_ref[:, g * HEAD_DIM:(g + 1) * HEAD_DIM] for g in range(group)], axis=0)
    t = k_ref.shape[0]
    m = l = acc = None
    for c in range(t // chunk + 1):
        if c < t // chunk:
            k = k_ref[c * chunk:(c + 1) * chunk, :]
            v = v_ref[c * chunk:(c + 1) * chunk, :]
        else:
            k, v = kc_ref[0], vc_ref[0]
        s = _dot_nt(q, k)
        m_c = jnp.max(s, axis=-1, keepdims=True)
        if m is None:
            m = m_c
            p = jnp.exp2(s - m)
            l = jnp.sum(p, axis=-1, keepdims=True)
            acc = _dot(p.astype(_BF16), v)
        else:
            m_new = jnp.maximum(m, m_c)
            alpha = jnp.exp2(m - m_new)
            p = jnp.exp2(s - m_new)
            l = alpha * l + jnp.sum(p, axis=-1, keepdims=True)
            acc = alpha * acc + _dot(p.astype(_BF16), v)
            m = m_new
    o = acc / l
    for g in range(group):
        o_ref[:, g * HEAD_DIM:(g + 1) * HEAD_DIM] = o[g * tq:(g + 1) * tq].astype(_BF16)


def _gqa_attention(attn, qkv, k_ctx, v_ctx, row0):
    b, l, kv_dim = k_ctx.shape
    n_kv = kv_dim // HEAD_DIM
    group = N_HEADS // n_kv
    t = qkv.shape[0] // b
    tq = 512
    nblk = t // tq
    out_blk0 = row0 // tq
    return pl.pallas_call(
        functools.partial(_gqa_attn_kernel, group=group, chunk=1024),
        grid=(b, n_kv, nblk),
        in_specs=[
            pl.BlockSpec(memory_space=pl.ANY),
            pl.BlockSpec((tq, group * HEAD_DIM), lambda bi, h, i: (bi * nblk + i, h)),
            pl.BlockSpec((t, HEAD_DIM), lambda bi, h, i: (bi, N_HEADS + h)),
            pl.BlockSpec((t, HEAD_DIM), lambda bi, h, i: (bi, N_HEADS + n_kv + h)),
            pl.BlockSpec((1, l, HEAD_DIM), lambda bi, h, i: (bi, 0, h)),
            pl.BlockSpec((1, l, HEAD_DIM), lambda bi, h, i: (bi, 0, h)),
        ],
        out_specs=pl.BlockSpec((tq, group * HEAD_DIM), lambda bi, h, i: (out_blk0 + bi * nblk + i, h)),
        out_shape=jax.ShapeDtypeStruct(attn.shape, attn.dtype),
        input_output_aliases={0: 0},
        compiler_params=_params(("arbitrary", "arbitrary", "arbitrary")),
        name="gqa_attention",
    )(attn, qkv, qkv, qkv, k_ctx, v_ctx)


def _pack_bf16_halves(h):
    half = h.shape[1] // 2

    def rounded(v):
        b = lax.bitcast_convert_type(v, jnp.int32)
        return b + 0x7FFF + ((b >> 16) & 1)

    lo, hi = rounded(h[:, :half]), rounded(h[:, half:])
    return (hi & -65536) | ((lo >> 16) & 0xFFFF)


def _unpack_bf16_halves(p):
    lo = lax.bitcast_convert_type(p << 16, _F32)
    hi = lax.bitcast_convert_type(p & -65536, _F32)
    return jnp.concatenate([lo, hi], axis=1).astype(_BF16)


def _route(logits, lane):
    def masked_softmax(mask):
        z = jnp.where(mask, logits, -jnp.inf)
        e = jnp.exp(z - jnp.max(z, axis=-1, keepdims=True))
        return e / jnp.sum(e, axis=-1, keepdims=True)

    def top1(p, mask):
        best = jnp.max(jnp.where(mask, p, -1.0), axis=-1, keepdims=True)
        idx = jnp.min(jnp.where(mask & (p == best), lane, float(LANES)), axis=-1, keepdims=True)
        return best, idx

    g_mask = lane < N_GROUPS
    g_p, g_idx = top1(masked_softmax(g_mask), g_mask)
    lo = N_GROUPS + g_idx * EXPERTS_PER_GROUP
    e_mask = (lane >= lo) & (lane < lo + EXPERTS_PER_GROUP)
    e_prob = masked_softmax(e_mask)
    p1, i1 = top1(e_prob, e_mask)
    p2, i2 = top1(e_prob, e_mask & (lane != i1))
    return i1 - N_GROUPS, i2 - N_GROUPS, g_p * p1 / (p1 + p2), g_p * p2 / (p1 + p2)


def _proj_router_kernel(*refs, first_blocks):
    a_ref = refs[0]
    if first_blocks is None:
        x_ref = refs[1]
        refs = refs[2:]
    else:
        xa_ref, xb_ref = refs[1:3]
        x_ref = refs[-1]
        refs = refs[3:-1]
    mod_ref, g_ref, wo_ref, wr_ref, br_ref, x1_ref, h_ref, ri_ref, rf_ref, cnt_ref, carry = refs
    i = pl.program_id(0)
    tm = a_ref.shape[0]

    @pl.when(i == 0)
    def _():
        carry[...] = jnp.zeros_like(carry)

    if first_blocks is not None:
        @pl.when(i < first_blocks)
        def _():
            x_ref[...] = xa_ref[...]

        @pl.when(i >= first_blocks)
        def _():
            x_ref[...] = xb_ref[...]

    o = _dot(a_ref[...], wo_ref[...])
    x1 = x_ref[...] + _mod_slices(mod_ref, 2) * o
    x1_ref[...] = x1
    h = _norm_modulate(x1, g_ref[...], _mod_slices(mod_ref, 3), _mod_slices(mod_ref, 4))
    h_ref[...] = _pack_bf16_halves(h)

    logits = _dot(h.astype(_BF16), wr_ref[...]) + br_ref[...]
    lane = lax.broadcasted_iota(jnp.int32, logits.shape, 1).astype(_F32)
    e1, e2, gate1, gate2 = _route(logits, lane)

    hit1, hit2 = lane == e1, lane == e2
    onehot = jnp.where(hit1 | hit2, 1.0, 0.0)
    row = lax.broadcasted_iota(jnp.int32, (tm, tm), 0)
    col = lax.broadcasted_iota(jnp.int32, (tm, tm), 1)
    lower = jnp.where(row > col, 1.0, 0.0).astype(_BF16)
    before = _dot(lower, onehot.astype(_BF16)) + carry[...]
    rank1 = jnp.sum(jnp.where(hit1, before, 0.0), axis=-1, keepdims=True)
    rank2 = jnp.sum(jnp.where(hit2, before, 0.0), axis=-1, keepdims=True)
    carry[...] = carry[...] + jnp.sum(onehot, axis=0, keepdims=True)

    record = jnp.where(lane == 0, e1, jnp.where(lane == 1, e2,
                       jnp.where(lane == 2, rank1, jnp.where(lane == 3, rank2, 0.0))))
    ri_ref[...] = record.T[:MOD_ROWS].astype(jnp.int32)
    rf_ref[...] = jnp.where(lane == 0, gate1, jnp.where(lane == 1, gate2, 0.0))
    cnt_ref[...] = jnp.broadcast_to(carry[...], cnt_ref.shape)


def _proj_router(attn, x_parts, mod, g, w_o, layer, w_r, b_r):
    m, d = attn.shape
    tm = 512
    blocks_per_group = GROUP_ROWS // tm
    row = lambda i: (i, 0)
    const = lambda i: (0, 0)
    if len(x_parts) == 1:
        first_blocks = None
        x_specs = [pl.BlockSpec((tm, d), row)]
        x_scratch = []
    else:
        first_blocks = x_parts[0].shape[0] // tm
        x_specs = [pl.BlockSpec((tm, d), lambda i: (jnp.minimum(i, first_blocks - 1), 0)),
                   pl.BlockSpec((tm, d), lambda i: (jnp.maximum(i - first_blocks, 0), 0))]
        x_scratch = [pltpu.VMEM((tm, d), _F32)]
    return pl.pallas_call(
        functools.partial(_proj_router_kernel, first_blocks=first_blocks),
        grid=(m // tm,),
        in_specs=[
            pl.BlockSpec((tm, d), row),
            *x_specs,
            pl.BlockSpec((1, 1, 6 * d), lambda i: (i // blocks_per_group, 0, 0)),
            pl.BlockSpec((1, d), const),
            pl.BlockSpec((None, d, d), lambda i: (layer, 0, 0)),
            pl.BlockSpec((d, LANES), const),
            pl.BlockSpec((1, LANES), const),
        ],
        out_specs=[
            pl.BlockSpec((tm, d), row),
            pl.BlockSpec((tm, d // 2), row),
            pl.BlockSpec((MOD_ROWS, tm), lambda i: (0, i)),
            pl.BlockSpec((tm, LANES), row),
            pl.BlockSpec((MOD_ROWS, LANES), const),
        ],
        out_shape=[
            jax.ShapeDtypeStruct((m, d), _F32),
            jax.ShapeDtypeStruct((m, d // 2), jnp.int32),
            jax.ShapeDtypeStruct((MOD_ROWS, m), jnp.int32),
            jax.ShapeDtypeStruct((m, LANES), _F32),
            jax.ShapeDtypeStruct((MOD_ROWS, LANES), _F32),
        ],
        scratch_shapes=[pltpu.VMEM((1, LANES), _F32)] + x_scratch,
        compiler_params=_params(("arbitrary",)),
        name="proj_router",
    )(attn, *x_parts, mod, g, w_o, w_r, b_r)


def _record_copy(dest_hbm, idx, isem, blk, slot, tm):
    return pltpu.make_async_copy(dest_hbm.at[:, pl.ds(blk * tm, tm)], idx.at[slot], isem.at[slot])


def _start_row_copies(idx, slot, tm, make_copy):
    for r in range(tm):
        for k in range(2):
            make_copy(r, k, idx[slot, k, r]).start(priority=k)


def _dispatch_kernel(dest_hbm, h_ref, xs_hbm, idx, isem, sem):
    i = pl.program_id(0)
    n = pl.num_programs(0)
    tm = h_ref.shape[0]

    @pl.when(i == 0)
    def _():
        _record_copy(dest_hbm, idx, isem, 0, 0, tm).start()

    @pl.when(i + 1 < n)
    def _():
        _record_copy(dest_hbm, idx, isem, i + 1, (i + 1) % 2, tm).start()

    _record_copy(dest_hbm, idx, isem, i, i % 2, tm).wait()
    for slot in range(2):
        @pl.when(i % 2 == slot)
        def _(slot=slot):
            _start_row_copies(idx, slot, tm, lambda r, k, d: pltpu.make_async_copy(
                h_ref.at[pl.ds(r, 1)], xs_hbm.at[pl.ds(d, 1)], sem))

    for _ in range(2):
        pltpu.make_async_copy(h_ref, xs_hbm.at[pl.ds(0, tm)], sem).wait()


def _dispatch(h2, dest, p_rows):
    m, d = h2.shape
    tm = 512
    return pl.pallas_call(
        _dispatch_kernel,
        grid=(m // tm,),
        in_specs=[
            pl.BlockSpec(memory_space=pl.ANY),
            pl.BlockSpec((tm, d), lambda i: (i, 0)),
        ],
        out_specs=pl.BlockSpec(memory_space=pl.ANY),
        out_shape=jax.ShapeDtypeStruct((p_rows, d), h2.dtype),
        scratch_shapes=[
            pltpu.SMEM((2, MOD_ROWS, tm), jnp.int32),
            pltpu.SemaphoreType.DMA((2,)),
            pltpu.SemaphoreType.DMA,
        ],
        compiler_params=_params(("arbitrary",)),
        name="dispatch",
    )(dest, h2)


def _expert_kernel(be_ref, valid_ref, next_ref, x_ref, wg_hbm, wu_hbm, wd_hbm, o_ref,
                   stage_g, stage_u, stage_d, wg, wu, wd, sem, *, layer):
    b = pl.program_id(0)
    valid = valid_ref[b]
    expert = be_ref[b]

    def fetch(e):
        return [pltpu.make_async_copy(src.at[layer, e], dst, sem)
                for src, dst in ((wg_hbm, stage_g), (wu_hbm, stage_u), (wd_hbm, stage_d))]

    @pl.when(b == 0)
    def _():
        for copy in fetch(expert):
            copy.start()

    first = (b == 0) | (expert != be_ref[jnp.maximum(b - 1, 0)])

    @pl.when(first & (valid > 0))
    def _():
        for copy in fetch(expert):
            copy.wait()
        rows = 256
        for stage, dst in ((stage_g, wg), (stage_u, wu), (stage_d, wd)):
            for r in range(0, stage.shape[0], rows):
                dst[r:r + rows, :] = stage[r:r + rows, :].astype(_BF16)

        @pl.when(next_ref[b] >= 0)
        def _():
            for copy in fetch(next_ref[b]):
                copy.start()

    @pl.when(valid > 0)
    def _():
        row = lax.broadcasted_iota(jnp.int32, x_ref.shape, 0)
        x = _unpack_bf16_halves(jnp.where(row < valid, x_ref[...], 0))
        hidden = _silu(_dot(x, wg[...])) * _dot(x, wu[...])
        o_ref[...] = _dot(hidden.astype(_BF16), wd[...])

    @pl.when(valid <= 0)
    def _():
        o_ref[...] = jnp.zeros_like(o_ref)


def _experts(xs, block_e, block_valid, block_next, w_gate, w_up, w_down, layer):
    p = xs.shape[0]
    d, de = w_gate.shape[-2:]
    nb = p // MOE_ROWS
    return pl.pallas_call(
        functools.partial(_expert_kernel, layer=layer),
        grid_spec=pltpu.PrefetchScalarGridSpec(
            num_scalar_prefetch=3,
            grid=(nb,),
            in_specs=[
                pl.BlockSpec((MOE_ROWS, d // 2), lambda b, be, nv, nx: (b, 0)),
                pl.BlockSpec(memory_space=pl.ANY),
                pl.BlockSpec(memory_space=pl.ANY),
                pl.BlockSpec(memory_space=pl.ANY),
            ],
            out_specs=pl.BlockSpec((MOE_ROWS, d), lambda b, be, nv, nx: (b, 0)),
            scratch_shapes=[
                pltpu.VMEM((d, de), _F32), pltpu.VMEM((d, de), _F32), pltpu.VMEM((de, d), _F32),
                pltpu.VMEM((d, de), _BF16), pltpu.VMEM((d, de), _BF16), pltpu.VMEM((de, d), _BF16),
                pltpu.SemaphoreType.DMA,
            ],
        ),
        out_shape=jax.ShapeDtypeStruct((p, d), _F32),
        compiler_params=_params(("arbitrary",)),
        name="experts",
    )(block_e, block_valid, block_next, xs, w_gate, w_up, w_down)


def _combine_kernel(dest_hbm, yb_hbm, x_ref, mod_ref, rf_ref, o_ref, idx, buf, isem, sem, *, blk0):
    i = pl.program_id(0)
    n = pl.num_programs(0)
    tm = x_ref.shape[0]

    def records(blk, slot):
        return _record_copy(dest_hbm, idx, isem, blk0 + blk, slot, tm)

    def fetch_rows(slot):
        _start_row_copies(idx, slot, tm, lambda r, k, d: pltpu.make_async_copy(
            yb_hbm.at[pl.ds(d, 1)], buf.at[slot, k, pl.ds(r, 1)], sem.at[slot]))

    @pl.when(i == 0)
    def _():
        records(0, 0).start()

        @pl.when(n > 1)
        def _():
            records(1, 1).start()

        records(0, 0).wait()
        fetch_rows(0)

    @pl.when(i + 2 < n)
    def _():
        records(i + 2, i % 2).start()

    for slot in range(2):
        @pl.when((i + 1 < n) & ((i + 1) % 2 == slot))
        def _(slot=slot):
            records(i + 1, slot).wait()
            fetch_rows(slot)

    slot = i % 2
    for k in range(2):
        pltpu.make_async_copy(yb_hbm.at[pl.ds(0, tm)], buf.at[slot, k], sem.at[slot]).wait()
    gate = rf_ref[...]
    y = buf[slot, 0] * gate[:, 0:1] + buf[slot, 1] * gate[:, 1:2]
    o_ref[...] = x_ref[...] + _mod_slices(mod_ref, 5) * y


def _combine(x1, mod, yb, dest, rf, row0, m):
    d = x1.shape[1]
    tm = 512
    blocks_per_group = GROUP_ROWS // tm
    blk0 = row0 // tm
    row = lambda i: (blk0 + i, 0)
    return pl.pallas_call(
        functools.partial(_combine_kernel, blk0=blk0),
        grid=(m // tm,),
        in_specs=[
            pl.BlockSpec(memory_space=pl.ANY),
            pl.BlockSpec(memory_space=pl.ANY),
            pl.BlockSpec((tm, d), row),
            pl.BlockSpec((1, 1, 6 * d), lambda i: ((blk0 + i) // blocks_per_group, 0, 0)),
            pl.BlockSpec((tm, LANES), row),
        ],
        out_specs=pl.BlockSpec((tm, d), lambda i: (i, 0)),
        out_shape=jax.ShapeDtypeStruct((m, d), _F32),
        scratch_shapes=[
            pltpu.SMEM((2, MOD_ROWS, tm), jnp.int32),
            pltpu.VMEM((2, 2, tm, d), _F32),
            pltpu.SemaphoreType.DMA((2,)),
            pltpu.SemaphoreType.DMA((2,)),
        ],
        compiler_params=_params(("arbitrary",)),
        name="combine",
    )(dest, yb, x1, mod, rf)


def _moe(x1, h2, ri, rf, counts, mod, w_gate, w_up, w_down, layer, splits):
    m = x1.shape[0]
    p_rows = (2 * m + N_EXPERTS * (MOE_ROWS - 1) + MOE_ROWS - 1) // MOE_ROWS * MOE_ROWS
    nb = p_rows // MOE_ROWS
    cnt = counts[0, :N_EXPERTS].astype(jnp.int32)
    blocks_e = (cnt + MOE_ROWS - 1) // MOE_ROWS
    end_blk = jnp.cumsum(blocks_e)
    start_blk = end_blk - blocks_e
    starts = (start_blk * MOE_ROWS).astype(jnp.int32)
    blk = jnp.arange(nb, dtype=jnp.int32)
    block_e = jnp.minimum(jnp.sum(end_blk[None, :] <= blk[:, None], axis=1), N_EXPERTS - 1).astype(jnp.int32)
    mine = block_e[:, None] == jnp.arange(N_EXPERTS, dtype=jnp.int32)[None, :]
    cnt_b = jnp.sum(jnp.where(mine, cnt[None, :], 0), axis=1)
    start_b = jnp.sum(jnp.where(mine, start_blk[None, :], 0), axis=1)
    block_valid = jnp.clip(cnt_b - (blk - start_b) * MOE_ROWS, 0, MOE_ROWS).astype(jnp.int32)
    ids = jnp.arange(N_EXPERTS, dtype=jnp.int32)
    later = (ids[None, :] > ids[:, None]) & (cnt[None, :] > 0)
    next_e = jnp.min(jnp.where(later, ids[None, :], N_EXPERTS), axis=1)
    next_e = jnp.where(next_e == N_EXPERTS, -1, next_e)
    block_next = jnp.sum(jnp.where(mine, next_e[None, :], 0), axis=1).astype(jnp.int32)
    experts = jnp.arange(N_EXPERTS, dtype=jnp.int32)[None, :, None]
    dest = jnp.sum(jnp.where(ri[0:2, None, :] == experts, starts[None, :, None], 0), axis=1) + ri[2:4]
    dest = jnp.concatenate([dest, jnp.zeros((MOD_ROWS - 2, m), jnp.int32)], axis=0)
    xs = _dispatch(h2, dest, p_rows)
    yb = _experts(xs, block_e, block_valid, block_next, w_gate, w_up, w_down, layer)
    return [_combine(x1, mod, yb, dest, rf, row0, rows) for row0, rows in splits]


def kernel(x_prompt, x_sample, c, cache_k_a, cache_v_a, cache_k_b, cache_v_b, c_ctx, w_mod, b_mod, norm1_g, norm2_g, w_qkv_a, w_o_a, q_norm_a, k_norm_a, rpb_a, w_qkv_b, w_o_b, q_norm_b, k_norm_b, w_router_group, b_router_group, w_router_expert, b_router_expert, w_gate, w_up, w_down):
    batch, seq, d = x_prompt.shape
    dec_batch, dec_seq, _ = x_sample.shape
    depth = w_mod.shape[0]
    mp = batch * seq
    ms = dec_batch * dec_seq
    assert mp == GROUP_ROWS and dec_seq == GROUP_ROWS and d == D_MODEL
    x_parts = [x_prompt.reshape(mp, d), x_sample.reshape(ms, d)]

    cond = jnp.concatenate([c_ctx[None], c, jnp.zeros((MOD_ROWS - 1 - dec_batch, d), _F32)], axis=0)
    mod_all = _modulation(cond, w_mod, b_mod)
    rope_tables = _rope_tables(dec_seq)

    w_router = jnp.concatenate([w_router_group, w_router_expert], axis=-1)
    n_route = w_router.shape[-1]
    w_router = jnp.pad(w_router, ((0, 0), (0, 0), (0, LANES - n_route))).astype(_BF16)
    b_router = jnp.pad(jnp.concatenate([b_router_group, b_router_expert], axis=-1),
                       ((0, 0), (0, LANES - n_route)))

    w_qkv_a, w_o_a, w_qkv_b, w_o_b = (_to_bf16(w) for w in (w_qkv_a, w_o_a, w_qkv_b, w_o_b))

    kv_arrays = [None, None]
    for i in range(depth):
        j = i // 2
        kind = i % 2
        slots = (depth + 1 - kind) // 2
        mod = mod_all[i].reshape(MOD_ROWS, 1, 6 * d)
        g1 = norm1_g[i].reshape(1, d)
        g2 = norm2_g[i].reshape(1, d)
        if i % 2 == 0:
            n_kv, w_qkv, w_o, qn, kn = N_HEADS, w_qkv_a, w_o_a, q_norm_a[j], k_norm_a[j]
            cache_k, cache_v, rope = cache_k_a[:, j], cache_v_a[:, j], None
        else:
            n_kv, w_qkv, w_o, qn, kn = N_KV_B, w_qkv_b, w_o_b, q_norm_b[j], k_norm_b[j]
            cache_k, cache_v, rope = cache_k_b[:, j], cache_v_b[:, j], rope_tables
        kv_dim = n_kv * HEAD_DIM
        qn, kn = qn.reshape(1, HEAD_DIM), kn.reshape(1, HEAD_DIM)
        x_p, x_s = (x_parts[0], x_parts[1]) if len(x_parts) == 2 else (x_parts[0], x_parts[0])
        row0_s = 0 if len(x_parts) == 2 else mp
        qkv_p, *kv = _qkv(x_p, mod, g1, w_qkv, j, qn, kn, n_kv, row0=0, m=mp, group0=0, rope_tables=None,
                          kv_out=(seq, slots, j, kv_arrays[kind]))
        kv_arrays[kind] = kv
        (qkv_s,) = _qkv(x_s, mod, g1, w_qkv, j, qn, kn, n_kv, row0=row0_s, m=ms, group0=1, rope_tables=rope)

        attn = _context_attention(qkv_p, n_kv, seq, mp + ms)
        k_ctx = cache_k.reshape(dec_batch, -1, kv_dim).astype(_BF16)
        v_ctx = cache_v.reshape(dec_batch, -1, kv_dim).astype(_BF16)
        if i % 2 == 0:
            attn = _neighbourhood_attention(attn, qkv_s, k_ctx, v_ctx, _na_bias_table(rpb_a[j]), mp)
        else:
            attn = _gqa_attention(attn, qkv_s, k_ctx, v_ctx, mp)

        x1, h2, ri, rf, counts = _proj_router(attn, x_parts, mod, g2, w_o, j, w_router[i],
                                              b_router[i].reshape(1, LANES))
        last = i == depth - 1
        outs = _moe(x1, h2, ri, rf, counts, mod, w_gate, w_up, w_down, i,
                    [(0, mp), (mp, ms)] if last else [(0, mp + ms)])
        x_parts = [outs[0]]

    def per_layer(t, n_kv):
        return t.reshape(batch, -1, seq, n_kv, HEAD_DIM)

    return (outs[0].reshape(batch, seq, d), outs[1].reshape(dec_batch, dec_seq, d),
            per_layer(kv_arrays[0][0], N_HEADS), per_layer(kv_arrays[0][1], N_HEADS),
            per_layer(kv_arrays[1][0], N_KV_B), per_layer(kv_arrays[1][1], N_KV_B))
```
